```python
import math
import jax
import jax.numpy as jnp
from jax import lax
import numpy as np

D_MODEL = 1024
BATCH = 16
SEQ = 2048
DEPTH = 4
DEC_BATCH = 8
DEC_SEQ = 64
PAST_LEN = 4096

CHUNK = 64
GDN_DK = 128
GDN_DV = 128
GDN_HEADS = D_MODEL // (2 * GDN_DV)
GDN_WIDTH = GDN_HEADS * GDN_DV
GDN_QKV = 2 * GDN_HEADS * GDN_DK + GDN_WIDTH
CONV_W = 4
GDN_PROJ = GDN_QKV + GDN_WIDTH + 2 * GDN_HEADS
RW_HD = 64
RW_HEADS = D_MODEL // (2 * RW_HD)
RW_WIDTH = RW_HEADS * RW_HD
W_LORA = 64
A_LORA = 64
G_LORA = 128
RW_PROJ = 3 * RW_WIDTH + W_LORA + A_LORA + G_LORA
P_TOT = GDN_PROJ + RW_PROJ
MIX_WIDTH = GDN_WIDTH + RW_WIDTH
D_FF = 4 * D_MODEL
DN_ALPHA = (2 * DEPTH) ** 0.25
DN_BETA = (8 * DEPTH) ** -0.25
LN_EPS = 1e-5
GN_EPS = 64e-5
NORM_EPS = 1e-6

kernel_name = 'hymba_gdn_rwkv7_deepnorm_adaln_stream'


def _l2norm(t):
    t = t.astype(jnp.float32)
    return t * lax.rsqrt(jnp.sum(t * t, axis=-1, keepdims=True) + NORM_EPS)


def _layer_norm(t, w, b, eps):
    tf = t.astype(jnp.float32)
    mu = jnp.mean(tf, axis=-1, keepdims=True)
    var = jnp.mean(jnp.square(tf - mu), axis=-1, keepdims=True)
    return ((tf - mu) * lax.rsqrt(var + eps) * w + b).astype(t.dtype)


def _rms_norm(t, w):
    tf = t.astype(jnp.float32)
    return tf * lax.rsqrt(jnp.mean(tf * tf, axis=-1, keepdims=True) + NORM_EPS) * w


def _causal_conv(t, buf, w):
    n = t.shape[1]
    tp = jnp.concatenate([buf.astype(t.dtype), t], axis=1)
    y = tp[:, 0:n] * w[0]
    for j in range(1, CONV_W):
        y = y + tp[:, j:j + n] * w[j]
    return y, tp[:, n:]


def _token_shift(t, prev, mu):
    tp = jnp.concatenate([prev[:, None, :].astype(t.dtype), t[:, :-1]], axis=1)
    return t + (tp - t) * mu, t[:, -1]


def _gated_delta_chunked(q, k, v, log_a, beta, s0, chunk):
    bsz, seq, nh, dk = q.shape
    dv = v.shape[-1]
    nb = seq // chunk

    def blk(t):
        t = t.astype(jnp.float32).reshape((bsz, nb, chunk) + t.shape[2:])
        return jnp.moveaxis(t, 3, 2)

    q, k, v, log_a, beta = blk(q), blk(k), blk(v), blk(log_a), blk(beta)
    g = jnp.cumsum(log_a, axis=-1)
    idx = jnp.arange(chunk)
    causal = idx[:, None] >= idx[None, :]
    strict = idx[:, None] > idx[None, :]
    gam = jnp.exp(jnp.where(causal, g[..., :, None] - g[..., None, :], -jnp.inf))
    a_mat = jnp.where(strict, beta[..., :, None] * jnp.einsum('bnhtd,bnhid->bnhti', k, k) * gam, 0.0)
    eye = jnp.eye(chunk, dtype=jnp.float32)
    rhs = jnp.concatenate([v * beta[..., None], k * (beta * jnp.exp(g))[..., None]], axis=-1)
    sol = lax.linalg.triangular_solve(a_mat + eye, rhs, left_side=True, lower=True, unit_diagonal=True)
    u0, wk = sol[..., :dv], sol[..., dv:]
    qk = jnp.where(causal, jnp.einsum('bnhtd,bnhid->bnhti', q, k) * gam, 0.0)
    g_last = g[..., -1:]
    q_dec = q * jnp.exp(g)[..., None]
    k_dec = k * jnp.exp(g_last - g)[..., None]
    d_last = jnp.exp(g_last[..., 0])

    def step(s, xs):
        u0_c, wk_c, qk_c, qd_c, kd_c, dl_c = xs
        u = u0_c - jnp.einsum('bhtk,bhkv->bhtv', wk_c, s)
        o = jnp.einsum('bhtk,bhkv->bhtv', qd_c, s) + jnp.einsum('bhti,bhiv->bhtv', qk_c, u)
        s = s * dl_c[..., None, None] + jnp.einsum('bhtk,bhtv->bhkv', kd_c, u)
        return s, o

    xs = tuple(jnp.moveaxis(t, 1, 0) for t in (u0, wk, qk, q_dec, k_dec, d_last))
    s_fin, o = lax.scan(step, s0.astype(jnp.float32), xs)
    o = jnp.transpose(o, (1, 0, 3, 2, 4)).reshape(bsz, seq, nh, dv)
    return o, s_fin


def _rwkv7_scan(r, decay, k, v, kk, a, s0):
    def tm(t):
        return jnp.moveaxis(t.astype(jnp.float32), 1, 0)

    def step(s, xs):
        r_t, w_t, k_t, v_t, kk_t, a_t = xs
        sa = jnp.einsum('bhij,bhj->bhi', s, -kk_t)
        s = (s * w_t[:, :, None, :] + sa[..., :, None] * (kk_t * a_t)[:, :, None, :]
             + v_t[..., :, None] * k_t[:, :, None, :])
        return s, jnp.einsum('bhij,bhj->bhi', s, r_t)

    s_fin, o = lax.scan(step, s0.astype(jnp.float32), tuple(tm(t) for t in (r, decay, k, v, kk, a)))
    return jnp.moveaxis(o, 0, 1), s_fin


def _mixer(h, l, p, s_gdn, s_conv, s_rw, s_shift, chunk):
    bsz, seq, _ = h.shape
    proj = jnp.einsum('btd,dp->btp', h, p['w_in'][l])
    o1 = GDN_QKV
    o2 = o1 + GDN_WIDTH
    o3 = o2 + GDN_HEADS
    qkv, z, ga, gb, rw = proj[..., :o1], proj[..., o1:o2], proj[..., o2:o3], proj[..., o3:GDN_PROJ], proj[..., GDN_PROJ:]
    qkv, new_conv = _causal_conv(qkv, s_conv, p['gdn_conv_w'][l])
    qkv = jax.nn.silu(qkv)
    nk = GDN_HEADS * GDN_DK
    q = _l2norm(qkv[..., :nk].reshape(bsz, seq, GDN_HEADS, GDN_DK)) * (GDN_DK ** -0.5)
    k = _l2norm(qkv[..., nk:2 * nk].reshape(bsz, seq, GDN_HEADS, GDN_DK))
    v = qkv[..., 2 * nk:].reshape(bsz, seq, GDN_HEADS, GDN_DV)
    log_a = -jnp.exp(p['gdn_a_log'][l].astype(jnp.float32)) * jax.nn.softplus(ga.astype(jnp.float32) + p['gdn_dt_bias'][l])
    beta = jax.nn.sigmoid(gb.astype(jnp.float32))
    o_g, new_gdn = _gated_delta_chunked(q, k, v, log_a, beta, s_gdn, chunk)
    o_g = _rms_norm(o_g, p['gdn_norm_w'][l]) * jax.nn.silu(z.astype(jnp.float32)).reshape(bsz, seq, GDN_HEADS, GDN_DV)
    o_g = o_g.reshape(bsz, seq, GDN_WIDTH)
    rw, new_shift = _token_shift(rw, s_shift, p['rwkv_mu'][l])
    c1 = RW_WIDTH
    c2 = 2 * RW_WIDTH
    c3 = 3 * RW_WIDTH
    c4 = c3 + W_LORA
    c5 = c4 + A_LORA
    r, kr, vr = rw[..., :c1], rw[..., c1:c2], rw[..., c2:c3]
    xw, xa, xg = rw[..., c3:c4], rw[..., c4:c5], rw[..., c5:]
    w_log = -jax.nn.softplus(-(p['rwkv_w0'][l] + jnp.tanh(xw) @ p['rwkv_w2'][l]).astype(jnp.float32)) - 0.5
    decay = jnp.exp(-jnp.exp(w_log))
    a = jax.nn.sigmoid((p['rwkv_a0'][l] + xa @ p['rwkv_a2'][l]).astype(jnp.float32))
    g = jax.nn.sigmoid(xg) @ p['rwkv_g2'][l]

    def heads(t):
        return t.reshape(bsz, seq, RW_HEADS, RW_HD)

    kk = _l2norm(heads(kr * p['rwkv_kk'][l]))
    kr = kr * (1.0 + (a - 1.0) * p['rwkv_ka'][l])
    o_r, new_rw = _rwkv7_scan(heads(r), heads(decay), heads(kr), heads(vr), kk, heads(a), s_rw)
    o_r = _layer_norm(o_r, p['rwkv_ln_w'][l].reshape(RW_HEADS, RW_HD), p['rwkv_ln_b'][l].reshape(RW_HEADS, RW_HD), GN_EPS)
    bonus = jnp.sum(heads(r * kr * p['rwkv_rk'][l]).astype(jnp.float32), axis=-1, keepdims=True) * heads(vr)
    o_r = (o_r + bonus).reshape(bsz, seq, RW_WIDTH) * g
    y = jnp.concatenate([o_g, o_r], axis=-1).astype(h.dtype) @ p['w_out'][l]
    return y, new_gdn, new_conv, new_rw, new_shift


def _trunk(x, c, st_gdn, st_conv, st_rw, st_shift, p, chunk):
    outs = ([], [], [], [])
    for l in range(DEPTH):
        mod = jnp.einsum('bd,de->be', jax.nn.silu(c), p['w_ada'][l]) + p['b_ada'][l]
        sh1, sc1, g1, sh2, sc2, g2 = jnp.split(mod[:, None, :], 6, axis=-1)
        h = x * (1.0 + sc1) + sh1
        y, n_gdn, n_conv, n_rw, n_shift = _mixer(h, l, p, st_gdn[l], st_conv[l], st_rw[l], st_shift[l], chunk)
        for lst, s in zip(outs, (n_gdn, n_conv, n_rw, n_shift)):
            lst.append(s.astype(x.dtype))
        x = _layer_norm(DN_ALPHA * x + g1 * y, p['ln1_w'][l], p['ln1_b'][l], LN_EPS)
        h = x * (1.0 + sc2) + sh2
        f = jnp.square(jax.nn.relu(h @ p['w_ff1'][l])) @ p['w_ff2'][l]
        x = _layer_norm(DN_ALPHA * x + g2 * f, p['ln2_w'][l], p['ln2_b'][l], LN_EPS)
    return x, jnp.stack(outs[0]), jnp.stack(outs[1]), jnp.stack(outs[2]), jnp.stack(outs[3])


def setup_inputs(seed: int = 0) -> dict:
    key = jax.random.key(seed)
    ks = iter(jax.random.split(key, 48))

    def nrm(shape, s):
        return jax.random.normal(next(ks), shape, jnp.float32) * s

    def uni(shape, lo, hi):
        return jax.random.uniform(next(ks), shape, jnp.float32, lo, hi)

    L = DEPTH
    D = D_MODEL
    x_prompt = nrm((BATCH, SEQ, D), 1.0)
    x_sample = nrm((DEC_BATCH, DEC_SEQ, D), 1.0)
    c_prompt = nrm((BATCH, D), 1.0)
    c_sample = nrm((DEC_BATCH, D), 1.0)
    state_gdn = nrm((L, DEC_BATCH, GDN_HEADS, GDN_DK, GDN_DV), 0.1)
    state_gdn_conv = nrm((L, DEC_BATCH, CONV_W - 1, GDN_QKV), 1.0)
    state_rwkv = nrm((L, DEC_BATCH, RW_HEADS, RW_HD, RW_HD), 1.0)
    state_rwkv_shift = nrm((L, DEC_BATCH, RW_PROJ), 1.0)
    w_ada = nrm((L, D, 6 * D), 0.5 * D ** -0.5)
    b_ada = nrm((L, 6 * D), 0.02)
    w_in = nrm((L, D, P_TOT), D ** -0.5)
    gdn_conv_w = nrm((L, CONV_W, GDN_QKV), CONV_W ** -0.5)
    gdn_a_log = jnp.log(uni((L, GDN_HEADS), 1.0, 16.0))
    dt = jnp.exp(uni((L, GDN_HEADS), math.log(1e-3), math.log(1e-1)))
    gdn_dt_bias = dt + jnp.log(-jnp.expm1(-dt))
    gdn_norm_w = 1.0 + nrm((L, GDN_DV), 0.02)
    rwkv_mu = uni((L, RW_PROJ), 0.0, 1.0)
    rwkv_w0 = uni((L, RW_WIDTH), -6.5, -1.5)
    rwkv_w2 = nrm((L, W_LORA, RW_WIDTH), 0.5 * W_LORA ** -0.5)
    rwkv_a0 = nrm((L, RW_WIDTH), 0.1)
    rwkv_a2 = nrm((L, A_LORA, RW_WIDTH), A_LORA ** -0.5)
    rwkv_g2 = nrm((L, G_LORA, RW_WIDTH), G_LORA ** -0.5)
    rwkv_kk = 0.85 + nrm((L, RW_WIDTH), 0.02)
    rwkv_ka = 1.0 + nrm((L, RW_WIDTH), 0.02)
    rwkv_rk = nrm((L, RW_WIDTH), 0.1)
    rwkv_ln_w = 1.0 + nrm((L, RW_WIDTH), 0.02)
    rwkv_ln_b = nrm((L, RW_WIDTH), 0.02)
    w_out = nrm((L, MIX_WIDTH, D), MIX_WIDTH ** -0.5 * DN_BETA)
    ln1_w = 1.0 + nrm((L, D), 0.02)
    ln1_b = nrm((L, D), 0.02)
    w_ff1 = nrm((L, D, D_FF), D ** -0.5)
    w_ff2 = nrm((L, D_FF, D), D_FF ** -0.5 * DN_BETA)
    ln2_w = 1.0 + nrm((L, D), 0.02)
    ln2_b = nrm((L, D), 0.02)
    return {'x_prompt': x_prompt, 'x_sample': x_sample, 'c_prompt': c_prompt, 'c_sample': c_sample,
            'state_gdn': state_gdn, 'state_gdn_conv': state_gdn_conv, 'state_rwkv': state_rwkv,
            'state_rwkv_shift': state_rwkv_shift, 'w_ada': w_ada, 'b_ada': b_ada, 'w_in': w_in,
            'gdn_conv_w': gdn_conv_w, 'gdn_a_log': gdn_a_log, 'gdn_dt_bias': gdn_dt_bias,
            'gdn_norm_w': gdn_norm_w, 'rwkv_mu': rwkv_mu, 'rwkv_w0': rwkv_w0, 'rwkv_w2': rwkv_w2,
            'rwkv_a0': rwkv_a0, 'rwkv_a2': rwkv_a2, 'rwkv_g2': rwkv_g2, 'rwkv_kk': rwkv_kk,
            'rwkv_ka': rwkv_ka, 'rwkv_rk': rwkv_rk, 'rwkv_ln_w': rwkv_ln_w, 'rwkv_ln_b': rwkv_ln_b,
            'w_out': w_out, 'ln1_w': ln1_w, 'ln1_b': ln1_b, 'w_ff1': w_ff1, 'w_ff2': w_ff2,
            'ln2_w': ln2_w, 'ln2_b': ln2_b}


def reference(x_prompt, x_sample, c_prompt, c_sample, state_gdn, state_gdn_conv, state_rwkv,
              state_rwkv_shift, w_ada, b_ada, w_in, gdn_conv_w, gdn_a_log, gdn_dt_bias, gdn_norm_w,
              rwkv_mu, rwkv_w0, rwkv_w2, rwkv_a0, rwkv_a2, rwkv_g2, rwkv_kk, rwkv_ka, rwkv_rk,
              rwkv_ln_w, rwkv_ln_b, w_out, ln1_w, ln1_b, w_ff1, w_ff2, ln2_w, ln2_b):
    p = dict(w_ada=w_ada, b_ada=b_ada, w_in=w_in, gdn_conv_w=gdn_conv_w, gdn_a_log=gdn_a_log,
             gdn_dt_bias=gdn_dt_bias, gdn_norm_w=gdn_norm_w, rwkv_mu=rwkv_mu, rwkv_w0=rwkv_w0,
             rwkv_w2=rwkv_w2, rwkv_a0=rwkv_a0, rwkv_a2=rwkv_a2, rwkv_g2=rwkv_g2, rwkv_kk=rwkv_kk,
             rwkv_ka=rwkv_ka, rwkv_rk=rwkv_rk, rwkv_ln_w=rwkv_ln_w, rwkv_ln_b=rwkv_ln_b, w_out=w_out,
             ln1_w=ln1_w, ln1_b=ln1_b, w_ff1=w_ff1, w_ff2=w_ff2, ln2_w=ln2_w, ln2_b=ln2_b)
    bp = x_prompt.shape[0]
    dt = x_prompt.dtype
    y_prompt, p_gdn, p_conv, p_rw, p_shift = _trunk(
        x_prompt, c_prompt,
        jnp.zeros((DEPTH, bp, GDN_HEADS, GDN_DK, GDN_DV), dt),
        jnp.zeros((DEPTH, bp, CONV_W - 1, GDN_QKV), dt),
        jnp.zeros((DEPTH, bp, RW_HEADS, RW_HD, RW_HD), dt),
        jnp.zeros((DEPTH, bp, RW_PROJ), dt),
        p, CHUNK)
    y_sample, s_gdn, s_conv, s_rw, s_shift = _trunk(
        x_sample, c_sample, state_gdn, state_gdn_conv, state_rwkv, state_rwkv_shift,
        p, x_sample.shape[1])
    return (y_prompt, y_sample, p_gdn, p_conv, p_rw, p_shift, s_gdn, s_conv, s_rw, s_shift)
```

```python
import functools

import jax
import jax.numpy as jnp
from jax import lax
from jax.experimental import pallas as pl
from jax.experimental.pallas import tpu as pltpu

F32 = jnp.float32
BF16 = jnp.bfloat16
HIGHEST = lax.Precision.HIGHEST

D_MODEL = 1024
CHUNK = 64
GDN_HEADS = 4
GDN_D = 128
GDN_WIDTH = GDN_HEADS * GDN_D
GDN_QKV = 3 * GDN_WIDTH
CONV_W = 4
GATE_COLS = 128
GDN_COLS = GDN_QKV + GDN_WIDTH + GATE_COLS
RW_HEADS = 8
RW_HD = 64
RW_WIDTH = RW_HEADS * RW_HD
RW_PAIRS = RW_HEADS // 2
RW_LORA = 128
RW_PROJ = 3 * RW_WIDTH + 2 * RW_LORA
D_FF = 4 * D_MODEL
DEPTH_ALPHA_POW = 0.25
LN_EPS = 1e-5
GN_EPS = 64e-5
NORM_EPS = 1e-6
HALO = 8
ROW_TILE = 512
TIME_TILE = 512
FF_TILE = 1024
VMEM_LIMIT = 56 * 1024 * 1024


def _bdot(a, b):
    return jnp.dot(a.astype(BF16), b.astype(BF16), preferred_element_type=F32)


def _bdot_nt(a, b):
    return lax.dot_general(a.astype(BF16), b.astype(BF16), (((1,), (1,)), ((), ())),
                           preferred_element_type=F32)


def _bdot_tn(a, b):
    return lax.dot_general(a.astype(BF16), b.astype(BF16), (((0,), (0,)), ((), ())),
                           preferred_element_type=F32)


def _hdot(a, b):
    return jnp.dot(a, b, precision=HIGHEST, preferred_element_type=F32)


def _split_dot(x, ones_bf16):
    hi = x.astype(BF16)
    lo = (x - hi.astype(F32)).astype(BF16)
    return (jnp.dot(hi, ones_bf16, preferred_element_type=F32)
            + jnp.dot(lo, ones_bf16, preferred_element_type=F32))


def _sigmoid(x):
    return 1.0 / (1.0 + jnp.exp(-x))


def _silu(x):
    return x * _sigmoid(x)


def _softplus(x):
    return jnp.maximum(x, 0.0) + jnp.log(1.0 + jnp.exp(-jnp.abs(x)))


def _unit_lower_inverse(a, n_terms):
    n = a.shape[0]
    row = lax.broadcasted_iota(jnp.int32, (n, n), 0)
    col = lax.broadcasted_iota(jnp.int32, (n, n), 1)
    x = jnp.where(row == col, 1.0, 0.0).astype(F32) - a
    p = a
    terms = 2
    while terms < n_terms:
        p = _hdot(p, p)
        x = x + _hdot(x, p)
        terms *= 2
    return x


def _layer_norm(t, w, b, eps):
    mu = jnp.mean(t, axis=-1, keepdims=True)
    d = t - mu
    var = jnp.mean(d * d, axis=-1, keepdims=True)
    return d * lax.rsqrt(var + eps) * w + b


def _mod_kernel(c_ref, w_ref, b_ref, o_ref):
    o_ref[0] = _bdot(_silu(c_ref[...]), w_ref[0]) + b_ref[0]


def _mod_call(c_all, w_ada, b_ada):
    depth, d, n = w_ada.shape
    rows = c_all.shape[0]
    tn = 1536
    return pl.pallas_call(
        _mod_kernel,
        grid=(depth, n // tn),
        in_specs=[pl.BlockSpec((rows, d), lambda l, j: (0, 0)),
                  pl.BlockSpec((1, d, tn), lambda l, j: (l, 0, j)),
                  pl.BlockSpec((1, 1, tn), lambda l, j: (l, 0, j))],
        out_specs=pl.BlockSpec((1, rows, tn), lambda l, j: (l, 0, j)),
        out_shape=jax.ShapeDtypeStruct((depth, rows, n), F32),
        compiler_params=pltpu.CompilerParams(dimension_semantics=("parallel", "parallel"),
                                             vmem_limit_bytes=VMEM_LIMIT),
        name="adaln_mod",
    )(c_all, w_ada, b_ada.reshape(depth, 1, n))


def _inproj_kernel(x_ref, sc_ref, sh_ref, wg_ref, wr_ref, og_ref, or_ref):
    bb, tt, d = x_ref.shape
    h = x_ref[...] * (1.0 + sc_ref[...]) + sh_ref[...]
    hb = h.reshape(bb * tt, d).astype(BF16)
    og_ref[...] = jnp.dot(hb, wg_ref[...], preferred_element_type=F32).reshape(og_ref.shape)
    or_ref[...] = jnp.dot(hb, wr_ref[...], preferred_element_type=F32).reshape(or_ref.shape)


def _row_blocks(bsz, seq):
    tt = min(seq, ROW_TILE)
    bb = max(1, min(bsz, ROW_TILE // tt))
    assert seq % tt == 0 and bsz % bb == 0 and tt % 8 == 0
    return bb, tt


def _inproj_call(x, sc, sh, wg, wr):
    bsz, seq, d = x.shape
    bb, tt = _row_blocks(bsz, seq)
    xmap = lambda b, t: (b, t, 0)
    mmap = lambda b, t: (b, 0, 0)
    wmap = lambda b, t: (0, 0)
    return pl.pallas_call(
        _inproj_kernel,
        grid=(bsz // bb, seq // tt),
        in_specs=[pl.BlockSpec((bb, tt, d), xmap),
                  pl.BlockSpec((bb, 1, d), mmap),
                  pl.BlockSpec((bb, 1, d), mmap),
                  pl.BlockSpec(wg.shape, wmap, pipeline_mode=pl.Buffered(1)),
                  pl.BlockSpec(wr.shape, wmap, pipeline_mode=pl.Buffered(1))],
        out_specs=[pl.BlockSpec((bb, tt, GDN_COLS), xmap),
                   pl.BlockSpec((bb, tt, RW_PROJ), xmap)],
        out_shape=[jax.ShapeDtypeStruct((bsz, seq, GDN_COLS), F32),
                   jax.ShapeDtypeStruct((bsz, seq, RW_PROJ), F32)],
        compiler_params=pltpu.CompilerParams(dimension_semantics=("parallel", "parallel"),
                                             vmem_limit_bytes=VMEM_LIMIT),
        name="in_proj",
    )(x, sc, sh, wg, wr)


def _gdn_kernel(pg_ref, cst_ref, s0_ref, cw_ref, gp_ref, nw_ref, og_ref, sout_ref, xbuf, s_scr, *, tt):
    c = CHUNK
    t_idx = pl.program_id(1)

    @pl.when(t_idx == 0)
    def _():
        xbuf[0:HALO, :] = cst_ref[0]
        s_scr[...] = s0_ref[0]

    xbuf[HALO:HALO + tt, :] = pg_ref[0, :, 0:GDN_QKV]

    row = lax.broadcasted_iota(jnp.int32, (c, c), 0)
    col = lax.broadcasted_iota(jnp.int32, (c, c), 1)
    causal = col <= row
    strict = col < row
    ltri = jnp.where(causal, 1.0, 0.0).astype(F32)
    ones_d = jnp.ones((GDN_D, GDN_D), BF16)
    cw = cw_ref[...]
    neg_a = -jnp.exp(gp_ref[0:1, :])
    dt_bias = gp_ref[1:2, :]
    nw = nw_ref[...]

    def chunk_body(ci, carry):
        r0 = pl.multiple_of(ci * c, c)
        blk = xbuf[pl.ds(r0, c + HALO), :]
        acc = blk[HALO:, :] * cw[CONV_W - 1:CONV_W, :]
        for j in range(1, CONV_W):
            acc = acc + blk[HALO - j:HALO - j + c, :] * cw[CONV_W - 1 - j:CONV_W - j, :]
        qkv = _silu(acc)
        gates = pg_ref[0, pl.ds(r0, c), GDN_QKV + GDN_WIDTH:GDN_COLS]
        log_a = neg_a * _softplus(gates + dt_bias)
        beta_all = _sigmoid(gates)
        g_all = _hdot(ltri, log_a)
        g_rows = g_all.T
        for h in range(GDN_HEADS):
            q = qkv[:, h * GDN_D:(h + 1) * GDN_D]
            k = qkv[:, GDN_WIDTH + h * GDN_D:GDN_WIDTH + (h + 1) * GDN_D]
            v = qkv[:, 2 * GDN_WIDTH + h * GDN_D:2 * GDN_WIDTH + (h + 1) * GDN_D]
            z = pg_ref[0, pl.ds(r0, c), GDN_QKV + h * GDN_D:GDN_QKV + (h + 1) * GDN_D]
            qn = q * lax.rsqrt(_split_dot(q * q, ones_d) + NORM_EPS) * (GDN_D ** -0.5)
            kn = k * lax.rsqrt(_split_dot(k * k, ones_d) + NORM_EPS)
            g_col = g_all[:, h:h + 1]
            g_row = g_rows[h:h + 1, :]
            beta = beta_all[:, GDN_HEADS + h:GDN_HEADS + h + 1]
            g_last = g_col[c - 1:c, :]
            gam = jnp.where(causal, jnp.exp(jnp.where(causal, g_col - g_row, 0.0)), 0.0)
            kb = kn.astype(BF16)
            a_mat = jnp.where(strict, beta * _bdot_nt(kb, kb) * gam, 0.0)
            t_inv = _unit_lower_inverse(a_mat, c)
            eg = jnp.exp(g_col)
            rhs = jnp.concatenate([v * beta, kn * (beta * eg)], axis=-1)
            sol = _hdot(t_inv, rhs)
            u0 = sol[:, :GDN_D]
            wk = sol[:, GDN_D:]
            qk = jnp.where(causal, _bdot_nt(qn, kb) * gam, 0.0)
            q_dec = qn * eg
            k_dec = kn * jnp.exp(g_last - g_col)
            s = s_scr[h]
            sb = s.astype(BF16)
            u = u0 - _bdot(wk, sb)
            o = _bdot(q_dec, sb) + _bdot(qk, u)
            s_scr[h] = s * jnp.exp(g_last) + _bdot_tn(k_dec, u)
            ms = _split_dot(o * o, ones_d) * (1.0 / GDN_D)
            og_ref[0, pl.ds(r0, c), h * GDN_D:(h + 1) * GDN_D] = (
                o * lax.rsqrt(ms + NORM_EPS) * nw * _silu(z))
        return carry

    lax.fori_loop(0, tt // c, chunk_body, 0)
    xbuf[0:HALO, :] = xbuf[tt:tt + HALO, :]

    @pl.when(t_idx == pl.num_programs(1) - 1)
    def _():
        sout_ref[0] = s_scr[...]


def _gdn_call(pg, conv_halo, s0, conv_w, gate_params, norm_w):
    bsz, seq, _ = pg.shape
    tt = min(seq, TIME_TILE)
    assert seq % tt == 0 and tt % CHUNK == 0
    smap = lambda b, t: (b, 0, 0, 0)
    return pl.pallas_call(
        functools.partial(_gdn_kernel, tt=tt),
        grid=(bsz, seq // tt),
        in_specs=[pl.BlockSpec((1, tt, GDN_COLS), lambda b, t: (b, t, 0)),
                  pl.BlockSpec((1, HALO, GDN_QKV), lambda b, t: (b, 0, 0)),
                  pl.BlockSpec((1, GDN_HEADS, GDN_D, GDN_D), smap),
                  pl.BlockSpec((CONV_W, GDN_QKV), lambda b, t: (0, 0)),
                  pl.BlockSpec((8, GATE_COLS), lambda b, t: (0, 0)),
                  pl.BlockSpec((1, GDN_D), lambda b, t: (0, 0))],
        out_specs=[pl.BlockSpec((1, tt, GDN_WIDTH), lambda b, t: (b, t, 0)),
                   pl.BlockSpec((1, GDN_HEADS, GDN_D, GDN_D), smap)],
        out_shape=[jax.ShapeDtypeStruct((bsz, seq, GDN_WIDTH), F32),
                   jax.ShapeDtypeStruct((bsz, GDN_HEADS, GDN_D, GDN_D), F32)],
        scratch_shapes=[pltpu.VMEM((HALO + tt, GDN_QKV), F32),
                        pltpu.VMEM((GDN_HEADS, GDN_D, GDN_D), F32)],
        compiler_params=pltpu.CompilerParams(dimension_semantics=("parallel", "arbitrary"),
                                             vmem_limit_bytes=VMEM_LIMIT),
        name="gated_deltanet",
    )(pg, conv_halo, s0, conv_w, gate_params, norm_w)


def _rwkv_kernel(pr_ref, sh0_ref, s0_ref, mu_ref, pv_ref, w2_ref, a2_ref, g2_ref,
                 or_ref, sout_ref, xbuf, s_scr, *, tt):
    c = CHUNK
    lanes = 2 * RW_HD
    t_idx = pl.program_id(1)

    @pl.when(t_idx == 0)
    def _():
        xbuf[0:HALO, :] = sh0_ref[0]
        s_scr[...] = s0_ref[0]

    xbuf[HALO:HALO + tt, :] = pr_ref[0]

    row = lax.broadcasted_iota(jnp.int32, (c, c), 0)
    col = lax.broadcasted_iota(jnp.int32, (c, c), 1)
    ltri = jnp.where(col <= row, 1.0, 0.0).astype(F32)
    prow = lax.broadcasted_iota(jnp.int32, (c, lanes), 0)
    plane = lax.broadcasted_iota(jnp.int32, (c, lanes), 1)
    first = plane < RW_HD
    tri_strict = (plane & (RW_HD - 1)) < prow
    tri_incl = (plane & (RW_HD - 1)) <= prow
    srow = lax.broadcasted_iota(jnp.int32, (lanes, lanes), 0)
    scol = lax.broadcasted_iota(jnp.int32, (lanes, lanes), 1)
    same_head = (srow < RW_HD) == (scol < RW_HD)
    seg_ones = jnp.where(same_head, 1.0, 0.0).astype(BF16)
    mu = mu_ref[...]
    w0 = pv_ref[0:1, :]
    a0 = pv_ref[1:2, :]
    kk_p = pv_ref[2:3, :]
    ka_p = pv_ref[3:4, :]
    rk_p = pv_ref[4:5, :]
    ln_w = pv_ref[5:6, :]
    ln_b = pv_ref[6:7, :]

    def chunk_body(ci, carry):
        r0 = pl.multiple_of(ci * c, c)
        blk = xbuf[pl.ds(r0, c + HALO), :]
        cur = blk[HALO:, :]
        xs = cur + (blk[HALO - 1:HALO - 1 + c, :] - cur) * mu
        r = xs[:, 0:RW_WIDTH]
        kr = xs[:, RW_WIDTH:2 * RW_WIDTH]
        vr = xs[:, 2 * RW_WIDTH:3 * RW_WIDTH]
        lora_in = xs[:, 3 * RW_WIDTH:3 * RW_WIDTH + RW_LORA]
        xg = xs[:, 3 * RW_WIDTH + RW_LORA:RW_PROJ]
        w_log = -_softplus(-(w0 + _bdot(jnp.tanh(lora_in), w2_ref[...]))) - 0.5
        log_w = -jnp.exp(w_log)
        a = _sigmoid(a0 + _bdot(lora_in, a2_ref[...]))
        gate = _bdot(_sigmoid(xg), g2_ref[...])
        kk_raw = kr * kk_p
        k2 = kr * (1.0 + (a - 1.0) * ka_p)
        rk = r * k2 * rk_p
        g_cum = _hdot(ltri, log_w)
        g_mid = g_cum[c // 2 - 1:c // 2, :]
        g_end = g_cum[c - 1:c, :]
        e_left = jnp.exp(g_cum - g_mid)
        e_left_prev = jnp.exp(g_cum - log_w - g_mid)
        e_right = jnp.exp(g_mid - g_cum)
        e_mid = jnp.exp(g_mid)
        e_end_mid = jnp.exp(g_end - g_mid)
        e_end = jnp.exp(g_end)
        for p in range(RW_PAIRS):
            sl = slice(p * lanes, (p + 1) * lanes)
            kkp = kk_raw[:, sl]
            kappa = kkp * lax.rsqrt(_split_dot(kkp * kkp, seg_ones) + NORM_EPS)
            b = kappa * a[:, sl]
            r_l = r[:, sl] * e_left[:, sl]
            kap_l = kappa * e_left_prev[:, sl]
            k_r = k2[:, sl] * e_right[:, sl]
            b_r = b * e_right[:, sl]
            v = vr[:, sl]
            lhs = jnp.concatenate([kap_l, r_l], axis=0)
            rhs_stack = jnp.concatenate([jnp.where(first, b_r, 0.0), jnp.where(first, k_r, 0.0),
                                         jnp.where(first, 0.0, k_r), jnp.where(first, 0.0, b_r)], axis=0)
            res = _bdot_nt(lhs, rhs_stack)
            kap_1, kap_2 = res[0:c, 0:lanes], res[0:c, lanes:2 * lanes]
            r_1, r_2 = res[c:2 * c, 0:lanes], res[c:2 * c, lanes:2 * lanes]
            a_bd = jnp.concatenate([jnp.where(first & tri_strict, kap_1, 0.0),
                                    jnp.where((~first) & tri_strict, kap_2, 0.0)], axis=0)
            t_inv = _unit_lower_inverse(a_bd, c)
            kb_cat = jnp.where(tri_strict, jnp.where(first, kap_2, kap_1), 0.0)
            qk_cat = jnp.where(tri_incl, jnp.where(first, r_2, r_1), 0.0)
            qb_cat = jnp.where(tri_incl, jnp.where(first, r_1, r_2), 0.0)
            v_swap = jnp.concatenate([jnp.where(first, 0.0, v), jnp.where(first, v, 0.0)], axis=0)
            s = s_scr[p]
            sb = s.astype(BF16)
            kg = kap_l * e_mid[:, sl]
            rg = r_l * e_mid[:, sl]
            rhs = _bdot_nt(kg, sb) + _bdot(kb_cat, v_swap)
            rhs_full = jnp.concatenate([jnp.where(first, rhs, 0.0), jnp.where(first, 0.0, rhs)], axis=0)
            p_full = _hdot(t_inv, rhs_full)
            pp = p_full[0:c, :] + p_full[c:2 * c, :]
            p_bd = jnp.concatenate([jnp.where(first, pp, 0.0), jnp.where(first, 0.0, pp)], axis=0)
            o = _bdot_nt(rg, sb) + _bdot(qk_cat, v_swap) - _bdot(qb_cat, p_bd)
            k_d = k_r * e_end_mid[:, sl]
            b_d = b_r * e_end_mid[:, sl]
            upd = _bdot_tn(jnp.concatenate([v, -pp], axis=0), jnp.concatenate([k_d, b_d], axis=0))
            s_scr[p] = s * e_end[:, sl] + jnp.where(same_head, upd, 0.0)
            mean = _split_dot(o, seg_ones) * (1.0 / RW_HD)
            dev = o - mean
            var = _split_dot(dev * dev, seg_ones) * (1.0 / RW_HD)
            o_n = dev * lax.rsqrt(var + GN_EPS) * ln_w[:, sl] + ln_b[:, sl]
            bonus = _split_dot(rk[:, sl], seg_ones) * v
            or_ref[0, pl.ds(r0, c), sl] = (o_n + bonus) * gate[:, sl]
        return carry

    lax.fori_loop(0, tt // c, chunk_body, 0)
    xbuf[0:HALO, :] = xbuf[tt:tt + HALO, :]

    @pl.when(t_idx == pl.num_programs(1) - 1)
    def _():
        sout_ref[0] = s_scr[...]


def _rwkv_call(pr, shift_halo, s0, mu, pvec, w2, a2, g2):
    bsz, seq, _ = pr.shape
    tt = min(seq, TIME_TILE)
    assert seq % tt == 0 and tt % CHUNK == 0
    lanes = 2 * RW_HD
    smap = lambda b, t: (b, 0, 0, 0)
    cmap = lambda b, t: (0, 0)
    return pl.pallas_call(
        functools.partial(_rwkv_kernel, tt=tt),
        grid=(bsz, seq // tt),
        in_specs=[pl.BlockSpec((1, tt, RW_PROJ), lambda b, t: (b, t, 0)),
                  pl.BlockSpec((1, HALO, RW_PROJ), lambda b, t: (b, 0, 0)),
                  pl.BlockSpec((1, RW_PAIRS, lanes, lanes), smap),
                  pl.BlockSpec((1, RW_PROJ), cmap),
                  pl.BlockSpec((8, RW_WIDTH), cmap),
                  pl.BlockSpec((RW_LORA, RW_WIDTH), cmap),
                  pl.BlockSpec((RW_LORA, RW_WIDTH), cmap),
                  pl.BlockSpec((RW_LORA, RW_WIDTH), cmap)],
        out_specs=[pl.BlockSpec((1, tt, RW_WIDTH), lambda b, t: (b, t, 0)),
                   pl.BlockSpec((1, RW_PAIRS, lanes, lanes), smap)],
        out_shape=[jax.ShapeDtypeStruct((bsz, seq, RW_WIDTH), F32),
                   jax.ShapeDtypeStruct((bsz, RW_PAIRS, lanes, lanes), F32)],
        scratch_shapes=[pltpu.VMEM((HALO + tt, RW_PROJ), F32),
                        pltpu.VMEM((RW_PAIRS, lanes, lanes), F32)],
        compiler_params=pltpu.CompilerParams(dimension_semantics=("parallel", "arbitrary"),
                                             vmem_limit_bytes=VMEM_LIMIT),
        name="rwkv7",
    )(pr, shift_halo, s0, mu, pvec, w2, a2, g2)


def _outffn_kernel(og_ref, or_ref, x_ref, g1_ref, sc_ref, sh_ref, g2_ref, wo_ref, lnp_ref,
                   w1_ref, w2_ref, o_ref, *, alpha):
    bb, tt, d = x_ref.shape
    rows = bb * tt
    y = (jnp.dot(og_ref[...].reshape(rows, GDN_WIDTH).astype(BF16), wo_ref[0:GDN_WIDTH, :],
                 preferred_element_type=F32)
         + jnp.dot(or_ref[...].reshape(rows, RW_WIDTH).astype(BF16), wo_ref[GDN_WIDTH:, :],
                   preferred_element_type=F32)).reshape(bb, tt, d)
    x1 = _layer_norm(alpha * x_ref[...] + g1_ref[...] * y, lnp_ref[0:1, :], lnp_ref[1:2, :], LN_EPS)
    hb = (x1 * (1.0 + sc_ref[...]) + sh_ref[...]).reshape(rows, d).astype(BF16)
    f = jnp.zeros((rows, d), F32)
    for j in range(D_FF // FF_TILE):
        hid = jnp.dot(hb, w1_ref[:, j * FF_TILE:(j + 1) * FF_TILE], preferred_element_type=F32)
        hid = jnp.square(jnp.maximum(hid, 0.0)).astype(BF16)
        f = f + jnp.dot(hid, w2_ref[j * FF_TILE:(j + 1) * FF_TILE, :], preferred_element_type=F32)
    o_ref[...] = _layer_norm(alpha * x1 + g2_ref[...] * f.reshape(bb, tt, d),
                             lnp_ref[2:3, :], lnp_ref[3:4, :], LN_EPS)


def _outffn_call(og, orr, x, g1, sc2, sh2, g2, wo, lnp, w1, w2, alpha):
    bsz, seq, d = x.shape
    bb, tt = _row_blocks(bsz, seq)
    xmap = lambda b, t: (b, t, 0)
    mmap = lambda b, t: (b, 0, 0)
    wmap = lambda b, t: (0, 0)
    mod_spec = pl.BlockSpec((bb, 1, d), mmap)
    return pl.pallas_call(
        functools.partial(_outffn_kernel, alpha=alpha),
        grid=(bsz // bb, seq // tt),
        in_specs=[pl.BlockSpec((bb, tt, GDN_WIDTH), xmap),
                  pl.BlockSpec((bb, tt, RW_WIDTH), xmap),
                  pl.BlockSpec((bb, tt, d), xmap),
                  mod_spec, mod_spec, mod_spec, mod_spec,
                  pl.BlockSpec(wo.shape, wmap, pipeline_mode=pl.Buffered(1)),
                  pl.BlockSpec(lnp.shape, wmap),
                  pl.BlockSpec(w1.shape, wmap, pipeline_mode=pl.Buffered(1)),
                  pl.BlockSpec(w2.shape, wmap, pipeline_mode=pl.Buffered(1))],
        out_specs=pl.BlockSpec((bb, tt, d), xmap),
        out_shape=jax.ShapeDtypeStruct((bsz, seq, d), F32),
        compiler_params=pltpu.CompilerParams(dimension_semantics=("parallel", "parallel"),
                                             vmem_limit_bytes=VMEM_LIMIT),
        name="out_ffn",
    )(og, orr, x, g1, sc2, sh2, g2, wo, lnp, w1, w2)


def _prep_layer(l, w_in, gdn_conv_w, gdn_a_log, gdn_dt_bias, gdn_norm_w, rwkv_mu, rwkv_w0, rwkv_w2,
                rwkv_a0, rwkv_a2, rwkv_g2, rwkv_kk, rwkv_ka, rwkv_rk, rwkv_ln_w, rwkv_ln_b, w_out,
                ln1_w, ln1_b, w_ff1, w_ff2, ln2_w, ln2_b):
    d = w_in.shape[1]
    n_gdn = GDN_QKV + GDN_WIDTH + 2 * GDN_HEADS
    wg = jnp.concatenate([w_in[l, :, :n_gdn], jnp.zeros((d, GDN_COLS - n_gdn), F32)], axis=1).astype(BF16)
    wr = w_in[l, :, n_gdn:].astype(BF16)
    gate_params = jnp.zeros((8, GATE_COLS), F32)
    gate_params = gate_params.at[0, :GDN_HEADS].set(gdn_a_log[l]).at[1, :GDN_HEADS].set(gdn_dt_bias[l])
    pvec = jnp.stack([rwkv_w0[l], rwkv_a0[l], rwkv_kk[l], rwkv_ka[l], rwkv_rk[l], rwkv_ln_w[l],
                      rwkv_ln_b[l], jnp.zeros((RW_WIDTH,), F32)])
    half = RW_LORA // 2
    zeros = jnp.zeros((half, RW_WIDTH), F32)
    w2 = jnp.concatenate([rwkv_w2[l], zeros], axis=0).astype(BF16)
    a2 = jnp.concatenate([zeros, rwkv_a2[l]], axis=0).astype(BF16)
    lnp = jnp.stack([ln1_w[l], ln1_b[l], ln2_w[l], ln2_b[l]] + [jnp.zeros((d,), F32)] * 4)
    return dict(wg=wg, wr=wr, conv_w=gdn_conv_w[l], gate_params=gate_params,
                norm_w=gdn_norm_w[l].reshape(1, GDN_D), mu=rwkv_mu[l].reshape(1, RW_PROJ), pvec=pvec,
                w2=w2, a2=a2, g2=rwkv_g2[l].astype(BF16), wo=w_out[l].astype(BF16), lnp=lnp,
                w1=w_ff1[l].astype(BF16), w2f=w_ff2[l].astype(BF16))


def _rwkv_state_to_pairs(s):
    bsz = s.shape[0]
    s = s.reshape(bsz, RW_PAIRS, 2, RW_HD, RW_HD)
    z = jnp.zeros_like(s[:, :, 0])
    top = jnp.concatenate([s[:, :, 0], z], axis=-1)
    bot = jnp.concatenate([z, s[:, :, 1]], axis=-1)
    return jnp.concatenate([top, bot], axis=-2)


def _rwkv_state_from_pairs(sp):
    bsz = sp.shape[0]
    return jnp.stack([sp[:, :, :RW_HD, :RW_HD], sp[:, :, RW_HD:, RW_HD:]], axis=2).reshape(
        bsz, RW_HEADS, RW_HD, RW_HD)


def _trunk(x, mods, st_gdn, st_conv, st_rw, st_shift, layers, alpha):
    bsz, seq, d = x.shape
    assert seq >= CONV_W - 1
    outs = ([], [], [], [])
    for l, p in enumerate(layers):
        sh1, sc1, g1, sh2, sc2, g2 = [mods[l, :, None, i * d:(i + 1) * d] for i in range(6)]
        pg, pr = _inproj_call(x, sc1, sh1, p["wg"], p["wr"])
        conv_halo = jnp.pad(st_conv[l], ((0, 0), (HALO - (CONV_W - 1), 0), (0, 0)))
        og, n_gdn = _gdn_call(pg, conv_halo, st_gdn[l], p["conv_w"], p["gate_params"], p["norm_w"])
        shift_halo = jnp.pad(st_shift[l][:, None, :], ((0, 0), (HALO - 1, 0), (0, 0)))
        orr, n_rw = _rwkv_call(pr, shift_halo, _rwkv_state_to_pairs(st_rw[l]), p["mu"], p["pvec"],
                               p["w2"], p["a2"], p["g2"])
        outs[0].append(n_gdn)
        outs[1].append(pg[:, seq - (CONV_W - 1):, :GDN_QKV])
        outs[2].append(_rwkv_state_from_pairs(n_rw))
        outs[3].append(pr[:, seq - 1, :])
        x = _outffn_call(og, orr, x, g1, sc2, sh2, g2, p["wo"], p["lnp"], p["w1"], p["w2f"], alpha)
    return (x,) + tuple(jnp.stack(o) for o in outs)


def kernel(x_prompt, x_sample, c_prompt, c_sample, state_gdn, state_gdn_conv, state_rwkv, state_rwkv_shift, w_ada, b_ada, w_in, gdn_conv_w, gdn_a_log, gdn_dt_bias, gdn_norm_w, rwkv_mu, rwkv_w0, rwkv_w2, rwkv_a0, rwkv_a2, rwkv_g2, rwkv_kk, rwkv_ka, rwkv_rk, rwkv_ln_w, rwkv_ln_b, w_out, ln1_w, ln1_b, w_ff1, w_ff2, ln2_w, ln2_b):
    depth = w_in.shape[0]
    alpha = (2 * depth) ** DEPTH_ALPHA_POW
    bp = x_prompt.shape[0]
    layers = [_prep_layer(l, w_in, gdn_conv_w, gdn_a_log, gdn_dt_bias, gdn_norm_w, rwkv_mu, rwkv_w0,
                          rwkv_w2, rwkv_a0, rwkv_a2, rwkv_g2, rwkv_kk, rwkv_ka, rwkv_rk, rwkv_ln_w,
                          rwkv_ln_b, w_out, ln1_w, ln1_b, w_ff1, w_ff2, ln2_w, ln2_b)
              for l in range(depth)]
    mods = _mod_call(jnp.concatenate([c_prompt, c_sample], axis=0), w_ada, b_ada)
    zeros = lambda *shape: jnp.zeros((depth, bp) + shape, F32)
    y_p, p_gdn, p_conv, p_rw, p_shift = _trunk(
        x_prompt, mods[:, :bp],
        zeros(GDN_HEADS, GDN_D, GDN_D), zeros(CONV_W - 1, GDN_QKV),
        zeros(RW_HEADS, RW_HD, RW_HD), zeros(RW_PROJ), layers, alpha)
    y_s, s_gdn, s_conv, s_rw, s_shift = _trunk(
        x_sample, mods[:, bp:], state_gdn, state_gdn_conv, state_rwkv, state_rwkv_shift, layers, alpha)
    return (y_p, y_s, p_gdn, p_conv, p_rw, p_shift, s_gdn, s_conv, s_rw, s_shift)
```

```python
import functools

import jax
import jax.numpy as jnp
from jax import lax
from jax.experimental import pallas as pl
from jax.experimental.pallas import tpu as pltpu

F32 = jnp.float32
BF16 = jnp.bfloat16

D_MODEL = 1024
CHUNK = 64
GDN_HEADS = 4
GDN_D = 128
GDN_WIDTH = GDN_HEADS * GDN_D
GDN_QKV = 3 * GDN_WIDTH
CONV_W = 4
GATE_COLS = 128
GDN_COLS = GDN_QKV + GDN_WIDTH + GATE_COLS
RW_HEADS = 8
RW_HD = 64
RW_WIDTH = RW_HEADS * RW_HD
RW_PAIRS = RW_HEADS // 2
RW_LORA = 128
RW_PROJ = 3 * RW_WIDTH + 2 * RW_LORA
D_FF = 4 * D_MODEL
DEPTH_ALPHA_POW = 0.25
LN_EPS = 1e-5
GN_EPS = 64e-5
NORM_EPS = 1e-6
HALO = 8
ROW_TILE = 512
TIME_TILE = 512
FF_TILE = 1024
GDN_GROUP = 4
RW_GROUP = 4
VMEM_LIMIT = 56 * 1024 * 1024


def _bdot(a, b):
    return jnp.dot(a.astype(BF16), b.astype(BF16), preferred_element_type=F32)


def _bdot_nt(a, b):
    return lax.dot_general(a.astype(BF16), b.astype(BF16), (((1,), (1,)), ((), ())),
                           preferred_element_type=F32)


def _bdot_tn(a, b):
    return lax.dot_general(a.astype(BF16), b.astype(BF16), (((0,), (0,)), ((), ())),
                           preferred_element_type=F32)


def _split2(x):
    hi = x.astype(BF16)
    return hi, (x - hi.astype(F32)).astype(BF16)


def _dot3(a, b):
    ah, al = _split2(a)
    bh, bl = _split2(b)
    return (jnp.dot(ah, bh, preferred_element_type=F32) + jnp.dot(ah, bl, preferred_element_type=F32)
            + jnp.dot(al, bh, preferred_element_type=F32))


def _cumsum_rows(ltri_bf16, x):
    hi = x.astype(BF16)
    rest = x - hi.astype(F32)
    mid = rest.astype(BF16)
    lo = (rest - mid.astype(F32)).astype(BF16)
    return (jnp.dot(ltri_bf16, hi, preferred_element_type=F32)
            + jnp.dot(ltri_bf16, mid, preferred_element_type=F32)
            + jnp.dot(ltri_bf16, lo, preferred_element_type=F32))


def _split_dot(x, ones_bf16):
    hi, lo = _split2(x)
    return (jnp.dot(hi, ones_bf16, preferred_element_type=F32)
            + jnp.dot(lo, ones_bf16, preferred_element_type=F32))


def _sigmoid(x):
    return 1.0 / (1.0 + jnp.exp(-x))


def _silu(x):
    return x * _sigmoid(x)


def _softplus(x):
    return jnp.maximum(x, 0.0) + jnp.log(1.0 + jnp.exp(-jnp.abs(x)))


def _pair_blockdiag(x, first):
    return jnp.concatenate([jnp.where(first, x, 0.0), jnp.where(first, 0.0, x)], axis=0)


def _pair_unit_lower_inverse(a_list, first, eye):
    n = eye.shape[0]
    xs = [eye - a for a in a_list]
    ps = list(a_list)
    terms = 2
    while terms < n:
        ps = [_bdot(p, _pair_blockdiag(p, first)) for p in ps]
        xs = [x + _bdot(x, _pair_blockdiag(p, first)) for x, p in zip(xs, ps)]
        terms *= 2
    es = [_dot3(eye + a, _pair_blockdiag(x, first)) - eye for a, x in zip(a_list, xs)]
    return [x - _bdot(x, _pair_blockdiag(e, first)) for x, e in zip(xs, es)]


def _layer_norm(t, w, b, eps):
    mu = jnp.mean(t, axis=-1, keepdims=True)
    d = t - mu
    var = jnp.mean(d * d, axis=-1, keepdims=True)
    return d * lax.rsqrt(var + eps) * w + b


def _mod_kernel(c_ref, w_ref, b_ref, o_ref):
    o_ref[0] = _bdot(_silu(c_ref[...]), w_ref[0]) + b_ref[0]


def _mod_call(c_all, w_ada, b_ada):
    depth, d, n = w_ada.shape
    rows = c_all.shape[0]
    tn = 1536
    return pl.pallas_call(
        _mod_kernel,
        grid=(depth, n // tn),
        in_specs=[pl.BlockSpec((rows, d), lambda l, j: (0, 0)),
                  pl.BlockSpec((1, d, tn), lambda l, j: (l, 0, j)),
                  pl.BlockSpec((1, 1, tn), lambda l, j: (l, 0, j))],
        out_specs=pl.BlockSpec((1, rows, tn), lambda l, j: (l, 0, j)),
        out_shape=jax.ShapeDtypeStruct((depth, rows, n), F32),
        compiler_params=pltpu.CompilerParams(dimension_semantics=("parallel", "parallel"),
                                             vmem_limit_bytes=VMEM_LIMIT),
        name="adaln_mod",
    )(c_all, w_ada, b_ada.reshape(depth, 1, n))


def _inproj_kernel(x_ref, sc_ref, sh_ref, wg_ref, wr_ref, og_ref, or_ref):
    bb, tt, d = x_ref.shape
    h = x_ref[...] * (1.0 + sc_ref[...]) + sh_ref[...]
    hb = h.reshape(bb * tt, d).astype(BF16)
    og_ref[...] = jnp.dot(hb, wg_ref[...], preferred_element_type=F32).reshape(og_ref.shape)
    or_ref[...] = jnp.dot(hb, wr_ref[...], preferred_element_type=F32).reshape(or_ref.shape)


def _row_blocks(bsz, seq):
    tt = min(seq, ROW_TILE)
    bb = max(1, min(bsz, ROW_TILE // tt))
    assert seq % tt == 0 and bsz % bb == 0 and tt % 8 == 0
    return bb, tt


def _inproj_call(x, sc, sh, wg, wr):
    bsz, seq, d = x.shape
    bb, tt = _row_blocks(bsz, seq)
    xmap = lambda b, t: (b, t, 0)
    mmap = lambda b, t: (b, 0, 0)
    wmap = lambda b, t: (0, 0)
    return pl.pallas_call(
        _inproj_kernel,
        grid=(bsz // bb, seq // tt),
        in_specs=[pl.BlockSpec((bb, tt, d), xmap),
                  pl.BlockSpec((bb, 1, d), mmap),
                  pl.BlockSpec((bb, 1, d), mmap),
                  pl.BlockSpec(wg.shape, wmap, pipeline_mode=pl.Buffered(1)),
                  pl.BlockSpec(wr.shape, wmap, pipeline_mode=pl.Buffered(1))],
        out_specs=[pl.BlockSpec((bb, tt, GDN_COLS), xmap),
                   pl.BlockSpec((bb, tt, RW_PROJ), xmap)],
        out_shape=[jax.ShapeDtypeStruct((bsz, seq, GDN_COLS), F32),
                   jax.ShapeDtypeStruct((bsz, seq, RW_PROJ), F32)],
        compiler_params=pltpu.CompilerParams(dimension_semantics=("parallel", "parallel"),
                                             vmem_limit_bytes=VMEM_LIMIT),
        name="in_proj",
    )(x, sc, sh, wg, wr)


def _gdn_kernel(pg_ref, cst_ref, s0_ref, cw_ref, gp_ref, nw_ref, og_ref, sout_ref,
                xbuf, s_scr, qd_scr, ov_scr, m_scr, u_scr, dl_scr, *, tt, group):
    c = CHUNK
    gc = group * c
    n_chunks = tt // c
    d = GDN_D
    t_idx = pl.program_id(1)

    @pl.when(t_idx == 0)
    def _():
        xbuf[0:HALO, :] = cst_ref[0]
        s_scr[...] = s0_ref[0]

    xbuf[HALO:HALO + tt, :] = pg_ref[0, :, 0:GDN_QKV]

    row = lax.broadcasted_iota(jnp.int32, (c, 2 * c), 0)
    lane = lax.broadcasted_iota(jnp.int32, (c, 2 * c), 1)
    first = lane < c
    col = lane & (c - 1)
    causal = col <= row
    strict = col < row
    eye = jnp.where(col == row, 1.0, 0.0).astype(F32)
    grow = lax.broadcasted_iota(jnp.int32, (gc, gc), 0)
    gcol = lax.broadcasted_iota(jnp.int32, (gc, gc), 1)
    ltri = jnp.where(((grow & -c) == (gcol & -c)) & (gcol <= grow), 1.0, 0.0).astype(BF16)
    ones_d = jnp.ones((d, d), BF16)
    zero_2d = jnp.zeros((c, 2 * d), F32)
    cw = cw_ref[...]
    neg_a = -jnp.exp(gp_ref[0:1, :])
    dt_bias = gp_ref[1:2, :]
    nw = nw_ref[...]
    heads = range(GDN_HEADS)

    def per_chunk_row(x, r):
        return jnp.concatenate([jnp.broadcast_to(x[j * c + r:j * c + r + 1, :], (c, x.shape[1]))
                                for j in range(group)], axis=0)

    def pass1(gi, carry):
        r0 = pl.multiple_of(gi * gc, gc)
        blk = xbuf[pl.ds(r0, gc + HALO), :]
        acc = blk[HALO:, :] * cw[CONV_W - 1:CONV_W, :]
        for j in range(1, CONV_W):
            acc = acc + blk[HALO - j:HALO - j + gc, :] * cw[CONV_W - 1 - j:CONV_W - j, :]
        qkv = _silu(acc)
        gates = pg_ref[0, pl.ds(r0, gc), GDN_QKV + GDN_WIDTH:GDN_COLS]
        log_a = neg_a * _softplus(gates + dt_bias)
        beta_all = _sigmoid(gates)
        g_all = _cumsum_rows(ltri, log_a)
        g_rows = g_all.T
        eg_all = jnp.exp(g_all)
        g_last = per_chunk_row(g_all, c - 1)
        dec_all = jnp.exp(g_last - g_all)
        for j in range(group):
            dl_scr[gi * group + j] = jnp.broadcast_to(
                jnp.exp(g_all[(j + 1) * c - 1:(j + 1) * c, :]), (8, GATE_COLS))
        q = [qkv[:, h * d:(h + 1) * d] for h in heads]
        k = [qkv[:, GDN_WIDTH + h * d:GDN_WIDTH + (h + 1) * d] for h in heads]
        v = [qkv[:, 2 * GDN_WIDTH + h * d:2 * GDN_WIDTH + (h + 1) * d] for h in heads]
        ss = _split_dot(jnp.concatenate([t * t for t in q + k], axis=0), ones_d)
        qn = [q[h] * lax.rsqrt(ss[h * gc:(h + 1) * gc] + NORM_EPS) * (d ** -0.5) for h in heads]
        kn = [k[h] * lax.rsqrt(ss[(GDN_HEADS + h) * gc:(GDN_HEADS + h + 1) * gc] + NORM_EPS) for h in heads]
        items = []
        for j in range(group):
            rs = slice(j * c, (j + 1) * c)
            for p in range(GDN_HEADS // 2):
                pair = (2 * p, 2 * p + 1)
                items.append(dict(
                    j=j, pair=pair, rs=rs,
                    ks2=jnp.concatenate([kn[h][rs] for h in pair], axis=0).astype(BF16),
                    qs2=jnp.concatenate([qn[h][rs] for h in pair], axis=0).astype(BF16)))
        kk2s = [_bdot_nt(it["ks2"], it["ks2"]) for it in items]
        qk2s = [_bdot_nt(it["qs2"], it["ks2"]) for it in items]
        for it, kk2, qk2 in zip(items, kk2s, qk2s):
            (h1, h2), rs = it["pair"], it["rs"]
            kk_cat = jnp.where(first, kk2[0:c], kk2[c:2 * c])
            qk_cat = jnp.where(first, qk2[0:c], qk2[c:2 * c])
            g_col_cat = jnp.where(first, g_all[rs, h1:h1 + 1], g_all[rs, h2:h2 + 1])
            g_row_cat = jnp.concatenate([g_rows[h1:h1 + 1, rs], g_rows[h2:h2 + 1, rs]], axis=1)
            beta_cat = jnp.where(first, beta_all[rs, GDN_HEADS + h1:GDN_HEADS + h1 + 1],
                                 beta_all[rs, GDN_HEADS + h2:GDN_HEADS + h2 + 1])
            gam = jnp.where(causal, jnp.exp(jnp.where(causal, g_col_cat - g_row_cat, 0.0)), 0.0)
            it["a_cat"] = jnp.where(strict, beta_cat * kk_cat * gam, 0.0)
            it["qk_cat"] = jnp.where(causal, qk_cat * gam, 0.0)
            rhs = []
            for h in (h1, h2):
                beta = beta_all[rs, GDN_HEADS + h:GDN_HEADS + h + 1]
                rhs.append(jnp.concatenate([v[h][rs] * beta, kn[h][rs] * (beta * eg_all[rs, h:h + 1])], axis=-1))
            it["rhs_bd"] = jnp.concatenate([jnp.concatenate([rhs[0], zero_2d], axis=-1),
                                            jnp.concatenate([zero_2d, rhs[1]], axis=-1)], axis=0)
        t_invs = _pair_unit_lower_inverse([it["a_cat"] for it in items], first, eye)
        sols = [_bdot(t_inv, it["rhs_bd"]) for it, t_inv in zip(items, t_invs)]
        corrs = [_bdot(it["qk_cat"], jnp.concatenate(
            [jnp.concatenate([sol[:, d:2 * d], sol[:, 0:d], zero_2d], axis=-1),
             jnp.concatenate([zero_2d, sol[:, 3 * d:4 * d], sol[:, 2 * d:3 * d]], axis=-1)], axis=0))
            for it, sol in zip(items, sols)]
        mus = [[_bdot_tn(kn[h][it["rs"]] * dec_all[it["rs"], h:h + 1],
                         jnp.concatenate([sol[:, (2 * i + 1) * d:(2 * i + 2) * d],
                                          sol[:, 2 * i * d:(2 * i + 1) * d]], axis=-1))
                for i, h in enumerate(it["pair"])] for it, sol in zip(items, sols)]
        for it, corr, mu_pair in zip(items, corrs, mus):
            ci = gi * group + it["j"]
            rs = it["rs"]
            for i, h in enumerate(it["pair"]):
                qd_scr[ci, h] = (qn[h][rs] * eg_all[rs, h:h + 1] - corr[:, 2 * i * d:(2 * i + 1) * d]).astype(BF16)
                ov_scr[pl.ds(r0 + it["j"] * c, c), h * d:(h + 1) * d] = corr[:, (2 * i + 1) * d:(2 * i + 2) * d]
                m_scr[ci, h] = mu_pair[i][:, 0:d].astype(BF16)
                u_scr[ci, h] = mu_pair[i][:, d:2 * d]
        return carry

    lax.fori_loop(0, n_chunks // group, pass1, 0)

    def pass2(ci, carry):
        r0 = pl.multiple_of(ci * c, c)
        dl = dl_scr[ci][0:1, :]
        s_old = [s_scr[h] for h in heads]
        s_bf = [s.astype(BF16) for s in s_old]
        o = [jnp.dot(qd_scr[ci, h], s_bf[h], preferred_element_type=F32) for h in heads]
        ms = [jnp.dot(m_scr[ci, h], s_bf[h], preferred_element_type=F32) for h in heads]
        ov_scr[pl.ds(r0, c), :] = jnp.concatenate(o, axis=-1) + ov_scr[pl.ds(r0, c), :]
        s_scr[...] = jnp.stack([s_old[h] * dl[:, h:h + 1] - ms[h] + u_scr[ci, h] for h in heads])
        return carry

    lax.fori_loop(0, n_chunks, pass2, 0)

    def pass3(gi, carry):
        r0 = pl.multiple_of(gi * gc, gc)
        o = ov_scr[pl.ds(r0, gc), :]
        o_all = jnp.concatenate([o[:, h * d:(h + 1) * d] for h in heads], axis=0)
        ms = _split_dot(o_all * o_all, ones_d) * (1.0 / d)
        o_all = o_all * lax.rsqrt(ms + NORM_EPS) * nw
        z = pg_ref[0, pl.ds(r0, gc), GDN_QKV:GDN_QKV + GDN_WIDTH]
        og_ref[0, pl.ds(r0, gc), :] = jnp.concatenate(
            [o_all[h * gc:(h + 1) * gc] for h in heads], axis=-1) * _silu(z)
        return carry

    lax.fori_loop(0, n_chunks // group, pass3, 0)
    xbuf[0:HALO, :] = xbuf[tt:tt + HALO, :]

    @pl.when(t_idx == pl.num_programs(1) - 1)
    def _():
        sout_ref[0] = s_scr[...]


def _gdn_call(pg, conv_halo, s0, conv_w, gate_params, norm_w):
    bsz, seq, _ = pg.shape
    tt = min(seq, TIME_TILE)
    n_chunks = tt // CHUNK
    group = GDN_GROUP if n_chunks % GDN_GROUP == 0 else 1
    assert seq % tt == 0 and tt % CHUNK == 0
    smap = lambda b, t: (b, 0, 0, 0)
    return pl.pallas_call(
        functools.partial(_gdn_kernel, tt=tt, group=group),
        grid=(bsz, seq // tt),
        in_specs=[pl.BlockSpec((1, tt, GDN_COLS), lambda b, t: (b, t, 0)),
                  pl.BlockSpec((1, HALO, GDN_QKV), lambda b, t: (b, 0, 0)),
                  pl.BlockSpec((1, GDN_HEADS, GDN_D, GDN_D), smap),
                  pl.BlockSpec((CONV_W, GDN_QKV), lambda b, t: (0, 0)),
                  pl.BlockSpec((8, GATE_COLS), lambda b, t: (0, 0)),
                  pl.BlockSpec((1, GDN_D), lambda b, t: (0, 0))],
        out_specs=[pl.BlockSpec((1, tt, GDN_WIDTH), lambda b, t: (b, t, 0)),
                   pl.BlockSpec((1, GDN_HEADS, GDN_D, GDN_D), smap)],
        out_shape=[jax.ShapeDtypeStruct((bsz, seq, GDN_WIDTH), F32),
                   jax.ShapeDtypeStruct((bsz, GDN_HEADS, GDN_D, GDN_D), F32)],
        scratch_shapes=[pltpu.VMEM((HALO + tt, GDN_QKV), F32),
                        pltpu.VMEM((GDN_HEADS, GDN_D, GDN_D), F32),
                        pltpu.VMEM((n_chunks, GDN_HEADS, CHUNK, GDN_D), BF16),
                        pltpu.VMEM((tt, GDN_WIDTH), F32),
                        pltpu.VMEM((n_chunks, GDN_HEADS, GDN_D, GDN_D), BF16),
                        pltpu.VMEM((n_chunks, GDN_HEADS, GDN_D, GDN_D), F32),
                        pltpu.VMEM((n_chunks, 8, GATE_COLS), F32)],
        compiler_params=pltpu.CompilerParams(dimension_semantics=("parallel", "arbitrary"),
                                             vmem_limit_bytes=VMEM_LIMIT),
        name="gated_deltanet",
    )(pg, conv_halo, s0, conv_w, gate_params, norm_w)


def _rwkv_kernel(pr_ref, sh0_ref, s0_ref, mu_ref, pv_ref, w2_ref, a2_ref, g2_ref,
                 or_ref, sout_ref, xbuf, s_scr, rq_scr, ov_scr, m_scr, u_scr, dec_scr, gate_scr, bonus_scr,
                 *, tt, group):
    c = CHUNK
    gc = group * c
    lanes = 2 * RW_HD
    n_chunks = tt // c
    t_idx = pl.program_id(1)

    @pl.when(t_idx == 0)
    def _():
        xbuf[0:HALO, :] = sh0_ref[0]
        s_scr[...] = s0_ref[0]

    xbuf[HALO:HALO + tt, :] = pr_ref[0]

    grow = lax.broadcasted_iota(jnp.int32, (gc, gc), 0)
    gcol = lax.broadcasted_iota(jnp.int32, (gc, gc), 1)
    same_chunk = (grow & -c) == (gcol & -c)
    ltri = jnp.where(same_chunk & (gcol <= grow), 1.0, 0.0).astype(BF16)
    prow = lax.broadcasted_iota(jnp.int32, (c, lanes), 0)
    plane = lax.broadcasted_iota(jnp.int32, (c, lanes), 1)
    first = plane < RW_HD
    pcol = plane & (RW_HD - 1)
    tri_strict = pcol < prow
    tri_incl = pcol <= prow
    eye = jnp.where(pcol == prow, 1.0, 0.0).astype(F32)
    srow = lax.broadcasted_iota(jnp.int32, (lanes, lanes), 0)
    scol = lax.broadcasted_iota(jnp.int32, (lanes, lanes), 1)
    same_head = (srow < RW_HD) == (scol < RW_HD)
    seg_ones = jnp.where(same_head, 1.0, 0.0).astype(BF16)
    mu = mu_ref[...]
    w0 = pv_ref[0:1, :]
    a0 = pv_ref[1:2, :]
    kk_p = pv_ref[2:3, :]
    ka_p = pv_ref[3:4, :]
    rk_p = pv_ref[4:5, :]
    ln_w = pv_ref[5:6, :]
    ln_b = pv_ref[6:7, :]
    pairs = range(RW_PAIRS)
    sls = [slice(p * lanes, (p + 1) * lanes) for p in pairs]

    def per_chunk_row(x, row):
        return jnp.concatenate([jnp.broadcast_to(x[j * c + row:j * c + row + 1, :], (c, x.shape[1]))
                                for j in range(group)], axis=0)

    def pass1(gi, carry):
        r0 = pl.multiple_of(gi * gc, gc)
        blk = xbuf[pl.ds(r0, gc + HALO), :]
        cur = blk[HALO:, :]
        xs = cur + (blk[HALO - 1:HALO - 1 + gc, :] - cur) * mu
        r = xs[:, 0:RW_WIDTH]
        kr = xs[:, RW_WIDTH:2 * RW_WIDTH]
        vr = xs[:, 2 * RW_WIDTH:3 * RW_WIDTH]
        lora_in = xs[:, 3 * RW_WIDTH:3 * RW_WIDTH + RW_LORA]
        xg = xs[:, 3 * RW_WIDTH + RW_LORA:RW_PROJ]
        w_log = -_softplus(-(w0 + _bdot(jnp.tanh(lora_in), w2_ref[...]))) - 0.5
        log_w = -jnp.exp(w_log)
        a = _sigmoid(a0 + _bdot(lora_in, a2_ref[...]))
        gate_scr[pl.ds(r0, gc), :] = _bdot(_sigmoid(xg), g2_ref[...])
        kk_raw = kr * kk_p
        k2 = kr * (1.0 + (a - 1.0) * ka_p)
        rk = r * k2 * rk_p
        g_cum = _cumsum_rows(ltri, log_w)
        mid_rows = [g_cum[j * c + c // 2 - 1:j * c + c // 2, :] for j in range(group)]
        end_rows = [g_cum[(j + 1) * c - 1:(j + 1) * c, :] for j in range(group)]
        g_mid = per_chunk_row(g_cum, c // 2 - 1)
        e_left = jnp.exp(g_cum - g_mid)
        e_left_prev = jnp.exp(g_cum - log_w - g_mid)
        e_right = jnp.exp(g_mid - g_cum)
        e_mid = [jnp.exp(m) for m in mid_rows]
        e_end_mid = [jnp.exp(e - m) for e, m in zip(end_rows, mid_rows)]
        for j in range(group):
            dec_scr[gi * group + j] = jnp.broadcast_to(
                jnp.exp(g_cum[(j + 1) * c - 1:(j + 1) * c, :]), (8, RW_WIDTH))
        seg = _split_dot(jnp.concatenate([kk_raw[:, sl] * kk_raw[:, sl] for sl in sls]
                                         + [rk[:, sl] for sl in sls], axis=0), seg_ones)
        bonus_scr[pl.ds(r0, gc), :] = jnp.concatenate(
            [seg[(RW_PAIRS + p) * gc:(RW_PAIRS + p + 1) * gc] for p in pairs], axis=-1) * vr
        items = []
        for j in range(group):
            rs = slice(j * c, (j + 1) * c)
            for p, sl in enumerate(sls):
                kappa = kk_raw[rs, sl] * lax.rsqrt(seg[p * gc + j * c:p * gc + (j + 1) * c] + NORM_EPS)
                b = kappa * a[rs, sl]
                r_l = r[rs, sl] * e_left[rs, sl]
                kap_l = kappa * e_left_prev[rs, sl]
                k_r = k2[rs, sl] * e_right[rs, sl]
                b_r = b * e_right[rs, sl]
                items.append(dict(
                    j=j, p=p, v=vr[rs, sl], kg=kap_l * e_mid[j][:, sl], rg=r_l * e_mid[j][:, sl],
                    k_d=k_r * e_end_mid[j][:, sl], b_d=b_r * e_end_mid[j][:, sl],
                    lhs=jnp.concatenate([kap_l, r_l], axis=0),
                    rhs=jnp.concatenate([jnp.where(first, b_r, 0.0), jnp.where(first, k_r, 0.0),
                                         jnp.where(first, 0.0, k_r), jnp.where(first, 0.0, b_r)], axis=0)))
        prods = [_bdot_nt(it["lhs"], it["rhs"]) for it in items]
        for it, res in zip(items, prods):
            kap_1, kap_2 = res[0:c, 0:lanes], res[0:c, lanes:2 * lanes]
            r_1, r_2 = res[c:2 * c, 0:lanes], res[c:2 * c, lanes:2 * lanes]
            it["a_cat"] = jnp.where(tri_strict, jnp.where(first, kap_1, kap_2), 0.0)
            kb_cat = jnp.where(tri_strict, jnp.where(first, kap_2, kap_1), 0.0)
            qk_cat = jnp.where(tri_incl, jnp.where(first, r_2, r_1), 0.0)
            it["kq"] = jnp.concatenate([kb_cat, qk_cat], axis=0)
            it["qb_cat"] = jnp.where(tri_incl, jnp.where(first, r_1, r_2), 0.0)
            v = it["v"]
            it["v_swap"] = jnp.concatenate([jnp.where(first, 0.0, v), jnp.where(first, v, 0.0)], axis=0)
        intras = [_bdot(it["kq"], it["v_swap"]) for it in items]
        t_invs = _pair_unit_lower_inverse([it["a_cat"] for it in items], first, eye)
        sols = [_bdot(t_inv, jnp.concatenate([_pair_blockdiag(it["kg"], first),
                                              _pair_blockdiag(intra[0:c], first)], axis=-1))
                for it, t_inv, intra in zip(items, t_invs, intras)]
        corrs = [_bdot(it["qb_cat"], jnp.concatenate([_pair_blockdiag(sol[:, 0:lanes], first),
                                                      _pair_blockdiag(sol[:, lanes:2 * lanes], first)], axis=-1))
                 for it, sol in zip(items, sols)]
        us = [_bdot_tn(jnp.concatenate([it["v"], -sol[:, lanes:2 * lanes]], axis=0),
                       jnp.concatenate([it["k_d"], it["b_d"]], axis=0)) for it, sol in zip(items, sols)]
        ms = [_bdot_tn(sol[:, 0:lanes], it["b_d"]) for it, sol in zip(items, sols)]
        for it, intra, corr, u, m in zip(items, intras, corrs, us, ms):
            ci = gi * group + it["j"]
            p = it["p"]
            rq_scr[ci, p] = (it["rg"] - corr[:, 0:lanes]).astype(BF16)
            ov_scr[pl.ds(r0 + it["j"] * c, c), sls[p]] = intra[c:2 * c] - corr[:, lanes:2 * lanes]
            u_scr[ci, p] = jnp.where(same_head, u, 0.0)
            m_scr[ci, p] = jnp.where(same_head, m, 0.0).astype(BF16)
        return carry

    lax.fori_loop(0, n_chunks // group, pass1, 0)

    def pass2(ci, carry):
        r0 = pl.multiple_of(ci * c, c)
        dec = dec_scr[ci][0:1, :]
        s_old = [s_scr[p] for p in pairs]
        s_bf = [s.astype(BF16) for s in s_old]
        o = [_bdot_nt(rq_scr[ci, p], s_bf[p]) for p in pairs]
        sm = [jnp.dot(s_bf[p], m_scr[ci, p], preferred_element_type=F32) for p in pairs]
        ov_scr[pl.ds(r0, c), :] = jnp.concatenate(o, axis=-1) + ov_scr[pl.ds(r0, c), :]
        s_scr[...] = jnp.stack([s_old[p] * dec[:, sls[p]] - sm[p] + u_scr[ci, p] for p in pairs])
        return carry

    lax.fori_loop(0, n_chunks, pass2, 0)

    def pass3(gi, carry):
        r0 = pl.multiple_of(gi * gc, gc)
        o = ov_scr[pl.ds(r0, gc), :]
        o_all = jnp.concatenate([o[:, sl] for sl in sls], axis=0)
        mean = _split_dot(o_all, seg_ones) * (1.0 / RW_HD)
        dev = o_all - mean
        var = _split_dot(dev * dev, seg_ones) * (1.0 / RW_HD)
        o_n = dev * lax.rsqrt(var + GN_EPS)
        o_n = jnp.concatenate([o_n[p * gc:(p + 1) * gc] for p in pairs], axis=-1) * ln_w + ln_b
        or_ref[0, pl.ds(r0, gc), :] = (o_n + bonus_scr[pl.ds(r0, gc), :]) * gate_scr[pl.ds(r0, gc), :]
        return carry

    lax.fori_loop(0, n_chunks // group, pass3, 0)
    xbuf[0:HALO, :] = xbuf[tt:tt + HALO, :]

    @pl.when(t_idx == pl.num_programs(1) - 1)
    def _():
        sout_ref[0] = s_scr[...]


def _rwkv_call(pr, shift_halo, s0, mu, pvec, w2, a2, g2):
    bsz, seq, _ = pr.shape
    tt = min(seq, TIME_TILE)
    n_chunks = tt // CHUNK
    group = RW_GROUP if n_chunks % RW_GROUP == 0 else 1
    assert seq % tt == 0 and tt % CHUNK == 0
    lanes = 2 * RW_HD
    smap = lambda b, t: (b, 0, 0, 0)
    cmap = lambda b, t: (0, 0)
    return pl.pallas_call(
        functools.partial(_rwkv_kernel, tt=tt, group=group),
        grid=(bsz, seq // tt),
        in_specs=[pl.BlockSpec((1, tt, RW_PROJ), lambda b, t: (b, t, 0)),
                  pl.BlockSpec((1, HALO, RW_PROJ), lambda b, t: (b, 0, 0)),
                  pl.BlockSpec((1, RW_PAIRS, lanes, lanes), smap),
                  pl.BlockSpec((1, RW_PROJ), cmap),
                  pl.BlockSpec((8, RW_WIDTH), cmap),
                  pl.BlockSpec((RW_LORA, RW_WIDTH), cmap),
                  pl.BlockSpec((RW_LORA, RW_WIDTH), cmap),
                  pl.BlockSpec((RW_LORA, RW_WIDTH), cmap)],
        out_specs=[pl.BlockSpec((1, tt, RW_WIDTH), lambda b, t: (b, t, 0)),
                   pl.BlockSpec((1, RW_PAIRS, lanes, lanes), smap)],
        out_shape=[jax.ShapeDtypeStruct((bsz, seq, RW_WIDTH), F32),
                   jax.ShapeDtypeStruct((bsz, RW_PAIRS, lanes, lanes), F32)],
        scratch_shapes=[pltpu.VMEM((HALO + tt, RW_PROJ), F32),
                        pltpu.VMEM((RW_PAIRS, lanes, lanes), F32),
                        pltpu.VMEM((n_chunks, RW_PAIRS, CHUNK, lanes), BF16),
                        pltpu.VMEM((tt, RW_WIDTH), F32),
                        pltpu.VMEM((n_chunks, RW_PAIRS, lanes, lanes), BF16),
                        pltpu.VMEM((n_chunks, RW_PAIRS, lanes, lanes), F32),
                        pltpu.VMEM((n_chunks, 8, RW_WIDTH), F32),
                        pltpu.VMEM((tt, RW_WIDTH), F32),
                        pltpu.VMEM((tt, RW_WIDTH), F32)],
        compiler_params=pltpu.CompilerParams(dimension_semantics=("parallel", "arbitrary"),
                                             vmem_limit_bytes=VMEM_LIMIT),
        name="rwkv7",
    )(pr, shift_halo, s0, mu, pvec, w2, a2, g2)


def _outffn_kernel(og_ref, or_ref, x_ref, g1_ref, sc_ref, sh_ref, g2_ref, wo_ref, lnp_ref,
                   w1_ref, w2_ref, o_ref, *, alpha):
    bb, tt, d = x_ref.shape
    rows = bb * tt
    y = (jnp.dot(og_ref[...].reshape(rows, GDN_WIDTH).astype(BF16), wo_ref[0:GDN_WIDTH, :],
                 preferred_element_type=F32)
         + jnp.dot(or_ref[...].reshape(rows, RW_WIDTH).astype(BF16), wo_ref[GDN_WIDTH:, :],
                   preferred_element_type=F32)).reshape(bb, tt, d)
    x1 = _layer_norm(alpha * x_ref[...] + g1_ref[...] * y, lnp_ref[0:1, :], lnp_ref[1:2, :], LN_EPS)
    hb = (x1 * (1.0 + sc_ref[...]) + sh_ref[...]).reshape(rows, d).astype(BF16)
    f = jnp.zeros((rows, d), F32)
    for j in range(D_FF // FF_TILE):
        hid = jnp.dot(hb, w1_ref[:, j * FF_TILE:(j + 1) * FF_TILE], preferred_element_type=F32)
        hid = jnp.square(jnp.maximum(hid, 0.0)).astype(BF16)
        f = f + jnp.dot(hid, w2_ref[j * FF_TILE:(j + 1) * FF_TILE, :], preferred_element_type=F32)
    o_ref[...] = _layer_norm(alpha * x1 + g2_ref[...] * f.reshape(bb, tt, d),
                             lnp_ref[2:3, :], lnp_ref[3:4, :], LN_EPS)


def _outffn_call(og, orr, x, g1, sc2, sh2, g2, wo, lnp, w1, w2, alpha):
    bsz, seq, d = x.shape
    bb, tt = _row_blocks(bsz, seq)
    xmap = lambda b, t: (b, t, 0)
    mmap = lambda b, t: (b, 0, 0)
    wmap = lambda b, t: (0, 0)
    mod_spec = pl.BlockSpec((bb, 1, d), mmap)
    return pl.pallas_call(
        functools.partial(_outffn_kernel, alpha=alpha),
        grid=(bsz // bb, seq // tt),
        in_specs=[pl.BlockSpec((bb, tt, GDN_WIDTH), xmap),
                  pl.BlockSpec((bb, tt, RW_WIDTH), xmap),
                  pl.BlockSpec((bb, tt, d), xmap),
                  mod_spec, mod_spec, mod_spec, mod_spec,
                  pl.BlockSpec(wo.shape, wmap, pipeline_mode=pl.Buffered(1)),
                  pl.BlockSpec(lnp.shape, wmap),
                  pl.BlockSpec(w1.shape, wmap, pipeline_mode=pl.Buffered(1)),
                  pl.BlockSpec(w2.shape, wmap, pipeline_mode=pl.Buffered(1))],
        out_specs=pl.BlockSpec((bb, tt, d), xmap),
        out_shape=jax.ShapeDtypeStruct((bsz, seq, d), F32),
        compiler_params=pltpu.CompilerParams(dimension_semantics=("parallel", "parallel"),
                                             vmem_limit_bytes=VMEM_LIMIT),
        name="out_ffn",
    )(og, orr, x, g1, sc2, sh2, g2, wo, lnp, w1, w2)


def _prep_layer(l, w_in, gdn_conv_w, gdn_a_log, gdn_dt_bias, gdn_norm_w, rwkv_mu, rwkv_w0, rwkv_w2,
                rwkv_a0, rwkv_a2, rwkv_g2, rwkv_kk, rwkv_ka, rwkv_rk, rwkv_ln_w, rwkv_ln_b, w_out,
                ln1_w, ln1_b, w_ff1, w_ff2, ln2_w, ln2_b):
    d = w_in.shape[1]
    n_gdn = GDN_QKV + GDN_WIDTH + 2 * GDN_HEADS
    wg = jnp.concatenate([w_in[l, :, :n_gdn], jnp.zeros((d, GDN_COLS - n_gdn), F32)], axis=1).astype(BF16)
    wr = w_in[l, :, n_gdn:].astype(BF16)
    gate_params = jnp.zeros((8, GATE_COLS), F32)
    gate_params = gate_params.at[0, :GDN_HEADS].set(gdn_a_log[l]).at[1, :GDN_HEADS].set(gdn_dt_bias[l])
    pvec = jnp.stack([rwkv_w0[l], rwkv_a0[l], rwkv_kk[l], rwkv_ka[l], rwkv_rk[l], rwkv_ln_w[l],
                      rwkv_ln_b[l], jnp.zeros((RW_WIDTH,), F32)])
    half = RW_LORA // 2
    zeros = jnp.zeros((half, RW_WIDTH), F32)
    w2 = jnp.concatenate([rwkv_w2[l], zeros], axis=0).astype(BF16)
    a2 = jnp.concatenate([zeros, rwkv_a2[l]], axis=0).astype(BF16)
    lnp = jnp.stack([ln1_w[l], ln1_b[l], ln2_w[l], ln2_b[l]] + [jnp.zeros((d,), F32)] * 4)
    return dict(wg=wg, wr=wr, conv_w=gdn_conv_w[l], gate_params=gate_params,
                norm_w=gdn_norm_w[l].reshape(1, GDN_D), mu=rwkv_mu[l].reshape(1, RW_PROJ), pvec=pvec,
                w2=w2, a2=a2, g2=rwkv_g2[l].astype(BF16), wo=w_out[l].astype(BF16), lnp=lnp,
                w1=w_ff1[l].astype(BF16), w2f=w_ff2[l].astype(BF16))


def _rwkv_state_to_pairs(s):
    bsz = s.shape[0]
    s = s.reshape(bsz, RW_PAIRS, 2, RW_HD, RW_HD)
    z = jnp.zeros_like(s[:, :, 0])
    top = jnp.concatenate([s[:, :, 0], z], axis=-1)
    bot = jnp.concatenate([z, s[:, :, 1]], axis=-1)
    return jnp.concatenate([top, bot], axis=-2)


def _rwkv_state_from_pairs(sp):
    bsz = sp.shape[0]
    return jnp.stack([sp[:, :, :RW_HD, :RW_HD], sp[:, :, RW_HD:, RW_HD:]], axis=2).reshape(
        bsz, RW_HEADS, RW_HD, RW_HD)


def _trunk(x, mods, st_gdn, st_conv, st_rw, st_shift, layers, alpha):
    bsz, seq, d = x.shape
    assert seq >= CONV_W - 1
    outs = ([], [], [], [])
    for l, p in enumerate(layers):
        sh1, sc1, g1, sh2, sc2, g2 = [mods[l, :, None, i * d:(i + 1) * d] for i in range(6)]
        pg, pr = _inproj_call(x, sc1, sh1, p["wg"], p["wr"])
        conv_halo = jnp.pad(st_conv[l], ((0, 0), (HALO - (CONV_W - 1), 0), (0, 0)))
        og, n_gdn = _gdn_call(pg, conv_halo, st_gdn[l], p["conv_w"], p["gate_params"], p["norm_w"])
        shift_halo = jnp.pad(st_shift[l][:, None, :], ((0, 0), (HALO - 1, 0), (0, 0)))
        orr, n_rw = _rwkv_call(pr, shift_halo, _rwkv_state_to_pairs(st_rw[l]), p["mu"], p["pvec"],
                               p["w2"], p["a2"], p["g2"])
        outs[0].append(n_gdn)
        outs[1].append(pg[:, seq - (CONV_W - 1):, :GDN_QKV])
        outs[2].append(_rwkv_state_from_pairs(n_rw))
        outs[3].append(pr[:, seq - 1, :])
        x = _outffn_call(og, orr, x, g1, sc2, sh2, g2, p["wo"], p["lnp"], p["w1"], p["w2f"], alpha)
    return (x,) + tuple(jnp.stack(o) for o in outs)


def kernel(x_prompt, x_sample, c_prompt, c_sample, state_gdn, state_gdn_conv, state_rwkv, state_rwkv_shift, w_ada, b_ada, w_in, gdn_conv_w, gdn_a_log, gdn_dt_bias, gdn_norm_w, rwkv_mu, rwkv_w0, rwkv_w2, rwkv_a0, rwkv_a2, rwkv_g2, rwkv_kk, rwkv_ka, rwkv_rk, rwkv_ln_w, rwkv_ln_b, w_out, ln1_w, ln1_b, w_ff1, w_ff2, ln2_w, ln2_b):
    depth = w_in.shape[0]
    alpha = (2 * depth) ** DEPTH_ALPHA_POW
    bp = x_prompt.shape[0]
    layers = [_prep_layer(l, w_in, gdn_conv_w, gdn_a_log, gdn_dt_bias, gdn_norm_w, rwkv_mu, rwkv_w0,
                          rwkv_w2, rwkv_a0, rwkv_a2, rwkv_g2, rwkv_kk, rwkv_ka, rwkv_rk, rwkv_ln_w,
                          rwkv_ln_b, w_out, ln1_w, ln1_b, w_ff1, w_ff2, ln2_w, ln2_b)
              for l in range(depth)]
    mods = _mod_call(jnp.concatenate([c_prompt, c_sample], axis=0), w_ada, b_ada)
    zeros = lambda *shape: jnp.zeros((depth, bp) + shape, F32)
    y_p, p_gdn, p_conv, p_rw, p_shift = _trunk(
        x_prompt, mods[:, :bp],
        zeros(GDN_HEADS, GDN_D, GDN_D), zeros(CONV_W - 1, GDN_QKV),
        zeros(RW_HEADS, RW_HD, RW_HD), zeros(RW_PROJ), layers, alpha)
    y_s, s_gdn, s_conv, s_rw, s_shift = _trunk(
        x_sample, mods[:, bp:], state_gdn, state_gdn_conv, state_rwkv, state_rwkv_shift, layers, alpha)
    return (y_p, y_s, p_gdn, p_conv, p_rw, p_shift, s_gdn, s_conv, s_rw, s_shift)
```

```python
import functools

import jax
import jax.numpy as jnp
from jax import lax
from jax.experimental import pallas as pl
from jax.experimental.pallas import tpu as pltpu

F32 = jnp.float32
BF16 = jnp.bfloat16

D_MODEL = 1024
CHUNK = 64
GDN_HEADS = 4
GDN_D = 128
GDN_WIDTH = GDN_HEADS * GDN_D
GDN_QKV = 3 * GDN_WIDTH
CONV_W = 4
GATE_COLS = 128
GDN_COLS = GDN_QKV + GDN_WIDTH + GATE_COLS
RW_HEADS = 8
RW_HD = 64
RW_WIDTH = RW_HEADS * RW_HD
RW_PAIRS = RW_HEADS // 2
RW_LORA = 128
RW_PROJ = 3 * RW_WIDTH + 2 * RW_LORA
D_FF = 4 * D_MODEL
DEPTH_ALPHA_POW = 0.25
LN_EPS = 1e-5
GN_EPS = 64e-5
NORM_EPS = 1e-6
HALO = 8
ROW_TILE = 512
TIME_TILE = 512
FF_TILE = 1024
GDN_GROUP = 4
RW_GROUP = 4
SEQ_BLOCK = 2
VMEM_LIMIT = 56 * 1024 * 1024


def _bdot(a, b):
    return jnp.dot(a.astype(BF16), b.astype(BF16), preferred_element_type=F32)


def _bdot_nt(a, b):
    return lax.dot_general(a.astype(BF16), b.astype(BF16), (((1,), (1,)), ((), ())),
                           preferred_element_type=F32)


def _bdot_tn(a, b):
    return lax.dot_general(a.astype(BF16), b.astype(BF16), (((0,), (0,)), ((), ())),
                           preferred_element_type=F32)


def _split2(x):
    hi = x.astype(BF16)
    return hi, (x - hi.astype(F32)).astype(BF16)


def _dot3(a, b):
    ah, al = _split2(a)
    bh, bl = _split2(b)
    return (jnp.dot(ah, bh, preferred_element_type=F32) + jnp.dot(ah, bl, preferred_element_type=F32)
            + jnp.dot(al, bh, preferred_element_type=F32))


def _cumsum_rows(ltri_bf16, x):
    hi = x.astype(BF16)
    rest = x - hi.astype(F32)
    mid = rest.astype(BF16)
    lo = (rest - mid.astype(F32)).astype(BF16)
    return (jnp.dot(ltri_bf16, hi, preferred_element_type=F32)
            + jnp.dot(ltri_bf16, mid, preferred_element_type=F32)
            + jnp.dot(ltri_bf16, lo, preferred_element_type=F32))


def _sigmoid(x):
    return 1.0 / (1.0 + jnp.exp(-x))


def _silu(x):
    return x * _sigmoid(x)


def _softplus(x):
    return jnp.maximum(x, 0.0) + jnp.log(1.0 + jnp.exp(-jnp.abs(x)))


def _pair_blockdiag(x, first):
    zero = jnp.zeros_like(x)
    return jnp.concatenate([jnp.where(first, x, zero), jnp.where(first, zero, x)], axis=0)


def _pair_unit_lower_inverse(a_list, first, eye, refine):
    n = eye.shape[0]
    xs = [eye - a for a in a_list]
    ps = list(a_list)
    bds = [_pair_blockdiag(p.astype(BF16), first) for p in ps]
    terms = 2
    while terms < n:
        ps = [jnp.dot(p.astype(BF16), bd, preferred_element_type=F32) for p, bd in zip(ps, bds)]
        bds = [_pair_blockdiag(p.astype(BF16), first) for p in ps]
        xs = [x + jnp.dot(x.astype(BF16), bd, preferred_element_type=F32) for x, bd in zip(xs, bds)]
        terms *= 2
    if not refine:
        return xs
    es = [_dot3(eye + a, _pair_blockdiag(x, first)) - eye for a, x in zip(a_list, xs)]
    return [x - _bdot(x, _pair_blockdiag(e, first)) for x, e in zip(xs, es)]


def _layer_norm(t, w, b, eps):
    mu = jnp.mean(t, axis=-1, keepdims=True)
    d = t - mu
    var = jnp.mean(d * d, axis=-1, keepdims=True)
    return d * lax.rsqrt(var + eps) * w + b


def _mod_kernel(c_ref, w_ref, b_ref, o_ref):
    o_ref[0] = _bdot(_silu(c_ref[...]), w_ref[0]) + b_ref[0]


def _mod_call(c_all, w_ada, b_ada):
    depth, d, n = w_ada.shape
    rows = c_all.shape[0]
    tn = 1536
    return pl.pallas_call(
        _mod_kernel,
        grid=(depth, n // tn),
        in_specs=[pl.BlockSpec((rows, d), lambda l, j: (0, 0)),
                  pl.BlockSpec((1, d, tn), lambda l, j: (l, 0, j)),
                  pl.BlockSpec((1, 1, tn), lambda l, j: (l, 0, j))],
        out_specs=pl.BlockSpec((1, rows, tn), lambda l, j: (l, 0, j)),
        out_shape=jax.ShapeDtypeStruct((depth, rows, n), F32),
        compiler_params=pltpu.CompilerParams(dimension_semantics=("parallel", "parallel"),
                                             vmem_limit_bytes=VMEM_LIMIT),
        name="adaln_mod",
    )(c_all, w_ada, b_ada.reshape(depth, 1, n))


def _inproj_kernel(x_ref, sc_ref, sh_ref, wg_ref, wr_ref, og_ref, or_ref):
    bb, tt, d = x_ref.shape
    h = x_ref[...] * (1.0 + sc_ref[...]) + sh_ref[...]
    hb = h.reshape(bb * tt, d).astype(BF16)
    og_ref[...] = jnp.dot(hb, wg_ref[...], preferred_element_type=F32).reshape(og_ref.shape)
    or_ref[...] = jnp.dot(hb, wr_ref[...], preferred_element_type=F32).reshape(or_ref.shape)


def _row_blocks(bsz, seq):
    tt = min(seq, ROW_TILE)
    bb = max(1, min(bsz, ROW_TILE // tt))
    assert seq % tt == 0 and bsz % bb == 0 and tt % 8 == 0
    return bb, tt


def _inproj_call(x, sc, sh, wg, wr):
    bsz, seq, d = x.shape
    bb, tt = _row_blocks(bsz, seq)
    xmap = lambda b, t: (b, t, 0)
    mmap = lambda b, t: (b, 0, 0)
    wmap = lambda b, t: (0, 0)
    return pl.pallas_call(
        _inproj_kernel,
        grid=(bsz // bb, seq // tt),
        in_specs=[pl.BlockSpec((bb, tt, d), xmap),
                  pl.BlockSpec((bb, 1, d), mmap),
                  pl.BlockSpec((bb, 1, d), mmap),
                  pl.BlockSpec(wg.shape, wmap, pipeline_mode=pl.Buffered(1)),
                  pl.BlockSpec(wr.shape, wmap, pipeline_mode=pl.Buffered(1))],
        out_specs=[pl.BlockSpec((bb, tt, GDN_COLS), xmap),
                   pl.BlockSpec((bb, tt, RW_PROJ), xmap)],
        out_shape=[jax.ShapeDtypeStruct((bsz, seq, GDN_COLS), F32),
                   jax.ShapeDtypeStruct((bsz, seq, RW_PROJ), F32)],
        compiler_params=pltpu.CompilerParams(dimension_semantics=("parallel", "parallel"),
                                             vmem_limit_bytes=VMEM_LIMIT),
        name="in_proj",
    )(x, sc, sh, wg, wr)


def _gdn_kernel(pg_ref, cst_ref, s0_ref, cw_ref, gp_ref, nw_ref, og_ref, sout_ref,
                xbuf, s_scr, qd_scr, ov_scr, m_scr, u_scr, dl_scr, *, tt, group, bb):
    c = CHUNK
    gc = group * c
    n_chunks = tt // c
    n_groups = n_chunks // group
    xrows = HALO + tt
    d = GDN_D
    t_idx = pl.program_id(1)

    @pl.when(t_idx == 0)
    def _():
        for b in range(bb):
            xbuf[b * xrows:b * xrows + HALO, :] = cst_ref[b]
            s_scr[b * GDN_HEADS:(b + 1) * GDN_HEADS] = s0_ref[b]

    for b in range(bb):
        xbuf[b * xrows + HALO:(b + 1) * xrows, :] = pg_ref[b, :, 0:GDN_QKV]

    row = lax.broadcasted_iota(jnp.int32, (c, 2 * c), 0)
    lane = lax.broadcasted_iota(jnp.int32, (c, 2 * c), 1)
    first = lane < c
    col = lane & (c - 1)
    causal = col <= row
    strict = col < row
    eye = jnp.where(col == row, 1.0, 0.0).astype(F32)
    grow = lax.broadcasted_iota(jnp.int32, (gc, gc), 0)
    gcol = lax.broadcasted_iota(jnp.int32, (gc, gc), 1)
    ltri = jnp.where(((grow & -c) == (gcol & -c)) & (gcol <= grow), 1.0, 0.0).astype(BF16)
    ones_d = jnp.ones((d, d), BF16)
    zero_2d = jnp.zeros((c, 2 * d), F32)
    cw = cw_ref[...]
    neg_a = -jnp.exp(gp_ref[0:1, :])
    dt_bias = gp_ref[1:2, :]
    nw = nw_ref[...]
    heads = range(GDN_HEADS)

    def per_chunk_row(x, r):
        return jnp.concatenate([jnp.broadcast_to(x[j * c + r:j * c + r + 1, :], (c, x.shape[1]))
                                for j in range(group)], axis=0)

    def pass1(step, carry):
        b = step // n_groups
        gi = step - b * n_groups
        r0 = pl.multiple_of(step * gc, gc)
        t0 = pl.multiple_of(gi * gc, gc)
        blk = xbuf[pl.ds(pl.multiple_of(b * xrows + gi * gc, 8), gc + HALO), :]
        acc = blk[HALO:, :] * cw[CONV_W - 1:CONV_W, :]
        for j in range(1, CONV_W):
            acc = acc + blk[HALO - j:HALO - j + gc, :] * cw[CONV_W - 1 - j:CONV_W - j, :]
        qkv = _silu(acc)
        gates = pg_ref[b, pl.ds(t0, gc), GDN_QKV + GDN_WIDTH:GDN_COLS]
        log_a = neg_a * _softplus(gates + dt_bias)
        beta_all = _sigmoid(gates)
        g_all = _cumsum_rows(ltri, log_a)
        g_rows = g_all.T
        eg_all = jnp.exp(g_all)
        g_last = per_chunk_row(g_all, c - 1)
        dec_all = jnp.exp(g_last - g_all)
        for j in range(group):
            dl_scr[step * group + j] = jnp.broadcast_to(
                jnp.exp(g_all[(j + 1) * c - 1:(j + 1) * c, :]), (8, GATE_COLS))
        q = [qkv[:, h * d:(h + 1) * d] for h in heads]
        k = [qkv[:, GDN_WIDTH + h * d:GDN_WIDTH + (h + 1) * d] for h in heads]
        v = [qkv[:, 2 * GDN_WIDTH + h * d:2 * GDN_WIDTH + (h + 1) * d] for h in heads]
        ss = _bdot(jnp.concatenate([t * t for t in q + k], axis=0), ones_d)
        qn = [q[h] * lax.rsqrt(ss[h * gc:(h + 1) * gc] + NORM_EPS) * (d ** -0.5) for h in heads]
        kn = [k[h] * lax.rsqrt(ss[(GDN_HEADS + h) * gc:(GDN_HEADS + h + 1) * gc] + NORM_EPS) for h in heads]
        items = []
        for j in range(group):
            rs = slice(j * c, (j + 1) * c)
            for p in range(GDN_HEADS // 2):
                pair = (2 * p, 2 * p + 1)
                items.append(dict(
                    j=j, pair=pair, rs=rs,
                    ks2=jnp.concatenate([kn[h][rs] for h in pair], axis=0).astype(BF16),
                    qs2=jnp.concatenate([qn[h][rs] for h in pair], axis=0).astype(BF16)))
        kk2s = [_bdot_nt(it["ks2"], it["ks2"]) for it in items]
        qk2s = [_bdot_nt(it["qs2"], it["ks2"]) for it in items]
        for it, kk2, qk2 in zip(items, kk2s, qk2s):
            (h1, h2), rs = it["pair"], it["rs"]
            kk_cat = jnp.where(first, kk2[0:c], kk2[c:2 * c])
            qk_cat = jnp.where(first, qk2[0:c], qk2[c:2 * c])
            g_col_cat = jnp.where(first, g_all[rs, h1:h1 + 1], g_all[rs, h2:h2 + 1])
            g_row_cat = jnp.concatenate([g_rows[h1:h1 + 1, rs], g_rows[h2:h2 + 1, rs]], axis=1)
            beta_cat = jnp.where(first, beta_all[rs, GDN_HEADS + h1:GDN_HEADS + h1 + 1],
                                 beta_all[rs, GDN_HEADS + h2:GDN_HEADS + h2 + 1])
            gam = jnp.where(causal, jnp.exp(jnp.where(causal, g_col_cat - g_row_cat, 0.0)), 0.0)
            it["a_cat"] = jnp.where(strict, beta_cat * kk_cat * gam, 0.0)
            it["qk_cat"] = jnp.where(causal, qk_cat * gam, 0.0)
            rhs = []
            for h in (h1, h2):
                beta = beta_all[rs, GDN_HEADS + h:GDN_HEADS + h + 1]
                rhs.append(jnp.concatenate([v[h][rs] * beta, kn[h][rs] * (beta * eg_all[rs, h:h + 1])], axis=-1))
            it["rhs_bd"] = jnp.concatenate([jnp.concatenate([rhs[0], zero_2d], axis=-1),
                                            jnp.concatenate([zero_2d, rhs[1]], axis=-1)], axis=0)
        t_invs = _pair_unit_lower_inverse([it["a_cat"] for it in items], first, eye, refine=True)
        sols = [_bdot(t_inv, it["rhs_bd"]) for it, t_inv in zip(items, t_invs)]
        corrs = [_bdot(it["qk_cat"], jnp.concatenate(
            [jnp.concatenate([sol[:, d:2 * d], sol[:, 0:d], zero_2d], axis=-1),
             jnp.concatenate([zero_2d, sol[:, 3 * d:4 * d], sol[:, 2 * d:3 * d]], axis=-1)], axis=0))
            for it, sol in zip(items, sols)]
        mus = [[_bdot_tn(kn[h][it["rs"]] * dec_all[it["rs"], h:h + 1],
                         jnp.concatenate([sol[:, (2 * i + 1) * d:(2 * i + 2) * d],
                                          sol[:, 2 * i * d:(2 * i + 1) * d]], axis=-1))
                for i, h in enumerate(it["pair"])] for it, sol in zip(items, sols)]
        for it, corr, mu_pair in zip(items, corrs, mus):
            ci = step * group + it["j"]
            rs = it["rs"]
            for i, h in enumerate(it["pair"]):
                qd_scr[ci, h] = (qn[h][rs] * eg_all[rs, h:h + 1] - corr[:, 2 * i * d:(2 * i + 1) * d]).astype(BF16)
                ov_scr[pl.ds(r0 + it["j"] * c, c), h * d:(h + 1) * d] = corr[:, (2 * i + 1) * d:(2 * i + 2) * d]
                m_scr[ci, h] = mu_pair[i][:, 0:d].astype(BF16)
                u_scr[ci, h] = mu_pair[i][:, d:2 * d]
        return carry

    lax.fori_loop(0, bb * n_groups, pass1, 0)

    def pass2(ci, carry):
        cis = [b * n_chunks + ci for b in range(bb)]
        rows = [pl.ds(pl.multiple_of(b * tt + ci * c, c), c) for b in range(bb)]
        dls = [dl_scr[cb][0:1, :] for cb in cis]
        s_old = [[s_scr[b * GDN_HEADS + h] for h in heads] for b in range(bb)]
        s_bf = [[s.astype(BF16) for s in sb] for sb in s_old]
        o = [[jnp.dot(qd_scr[cis[b], h], s_bf[b][h], preferred_element_type=F32) for h in heads]
             for b in range(bb)]
        ms = [[jnp.dot(m_scr[cis[b], h], s_bf[b][h], preferred_element_type=F32) for h in heads]
              for b in range(bb)]
        for b in range(bb):
            ov_scr[rows[b], :] = jnp.concatenate(o[b], axis=-1) + ov_scr[rows[b], :]
        s_scr[...] = jnp.stack([s_old[b][h] * dls[b][:, h:h + 1] - ms[b][h] + u_scr[cis[b], h]
                                for b in range(bb) for h in heads])
        return carry

    lax.fori_loop(0, n_chunks, pass2, 0)

    def pass3(step, carry):
        b = step // n_groups
        t0 = pl.multiple_of((step - b * n_groups) * gc, gc)
        r0 = pl.multiple_of(step * gc, gc)
        o = ov_scr[pl.ds(r0, gc), :]
        o_all = jnp.concatenate([o[:, h * d:(h + 1) * d] for h in heads], axis=0)
        ms = _bdot(o_all * o_all, ones_d) * (1.0 / d)
        o_all = o_all * lax.rsqrt(ms + NORM_EPS) * nw
        z = pg_ref[b, pl.ds(t0, gc), GDN_QKV:GDN_QKV + GDN_WIDTH]
        og_ref[b, pl.ds(t0, gc), :] = jnp.concatenate(
            [o_all[h * gc:(h + 1) * gc] for h in heads], axis=-1) * _silu(z)
        return carry

    lax.fori_loop(0, bb * n_groups, pass3, 0)
    for b in range(bb):
        xbuf[b * xrows:b * xrows + HALO, :] = xbuf[b * xrows + tt:(b + 1) * xrows, :]

    @pl.when(t_idx == pl.num_programs(1) - 1)
    def _():
        for b in range(bb):
            sout_ref[b] = s_scr[b * GDN_HEADS:(b + 1) * GDN_HEADS]


def _gdn_call(pg, conv_halo, s0, conv_w, gate_params, norm_w):
    bsz, seq, _ = pg.shape
    tt = min(seq, TIME_TILE)
    n_chunks = tt // CHUNK
    group = GDN_GROUP if n_chunks % GDN_GROUP == 0 else 1
    bb = SEQ_BLOCK if bsz % SEQ_BLOCK == 0 else 1
    assert seq % tt == 0 and tt % CHUNK == 0
    smap = lambda b, t: (b, 0, 0, 0)
    return pl.pallas_call(
        functools.partial(_gdn_kernel, tt=tt, group=group, bb=bb),
        grid=(bsz // bb, seq // tt),
        in_specs=[pl.BlockSpec((bb, tt, GDN_COLS), lambda b, t: (b, t, 0)),
                  pl.BlockSpec((bb, HALO, GDN_QKV), lambda b, t: (b, 0, 0)),
                  pl.BlockSpec((bb, GDN_HEADS, GDN_D, GDN_D), smap),
                  pl.BlockSpec((CONV_W, GDN_QKV), lambda b, t: (0, 0)),
                  pl.BlockSpec((8, GATE_COLS), lambda b, t: (0, 0)),
                  pl.BlockSpec((1, GDN_D), lambda b, t: (0, 0))],
        out_specs=[pl.BlockSpec((bb, tt, GDN_WIDTH), lambda b, t: (b, t, 0)),
                   pl.BlockSpec((bb, GDN_HEADS, GDN_D, GDN_D), smap)],
        out_shape=[jax.ShapeDtypeStruct((bsz, seq, GDN_WIDTH), F32),
                   jax.ShapeDtypeStruct((bsz, GDN_HEADS, GDN_D, GDN_D), F32)],
        scratch_shapes=[pltpu.VMEM((bb * (HALO + tt), GDN_QKV), F32),
                        pltpu.VMEM((bb * GDN_HEADS, GDN_D, GDN_D), F32),
                        pltpu.VMEM((bb * n_chunks, GDN_HEADS, CHUNK, GDN_D), BF16),
                        pltpu.VMEM((bb * tt, GDN_WIDTH), F32),
                        pltpu.VMEM((bb * n_chunks, GDN_HEADS, GDN_D, GDN_D), BF16),
                        pltpu.VMEM((bb * n_chunks, GDN_HEADS, GDN_D, GDN_D), F32),
                        pltpu.VMEM((bb * n_chunks, 8, GATE_COLS), F32)],
        compiler_params=pltpu.CompilerParams(dimension_semantics=("parallel", "arbitrary"),
                                             vmem_limit_bytes=VMEM_LIMIT),
        name="gated_deltanet",
    )(pg, conv_halo, s0, conv_w, gate_params, norm_w)


def _rwkv_kernel(pr_ref, sh0_ref, s0_ref, mu_ref, pv_ref, w2_ref, a2_ref, g2_ref,
                 or_ref, sout_ref, xbuf, s_scr, rq_scr, ov_scr, m_scr, u_scr, dec_scr, gate_scr, bonus_scr,
                 *, tt, group, bb):
    c = CHUNK
    gc = group * c
    lanes = 2 * RW_HD
    n_chunks = tt // c
    n_groups = n_chunks // group
    xrows = HALO + tt
    t_idx = pl.program_id(1)

    @pl.when(t_idx == 0)
    def _():
        for b in range(bb):
            xbuf[b * xrows:b * xrows + HALO, :] = sh0_ref[b]
            s_scr[b * RW_PAIRS:(b + 1) * RW_PAIRS] = s0_ref[b]

    for b in range(bb):
        xbuf[b * xrows + HALO:(b + 1) * xrows, :] = pr_ref[b]

    grow = lax.broadcasted_iota(jnp.int32, (gc, gc), 0)
    gcol = lax.broadcasted_iota(jnp.int32, (gc, gc), 1)
    same_chunk = (grow & -c) == (gcol & -c)
    ltri = jnp.where(same_chunk & (gcol <= grow), 1.0, 0.0).astype(BF16)
    prow = lax.broadcasted_iota(jnp.int32, (c, lanes), 0)
    plane = lax.broadcasted_iota(jnp.int32, (c, lanes), 1)
    first = plane < RW_HD
    pcol = plane & (RW_HD - 1)
    tri_strict = pcol < prow
    tri_incl = pcol <= prow
    eye = jnp.where(pcol == prow, 1.0, 0.0).astype(F32)
    srow = lax.broadcasted_iota(jnp.int32, (lanes, lanes), 0)
    scol = lax.broadcasted_iota(jnp.int32, (lanes, lanes), 1)
    same_head = (srow < RW_HD) == (scol < RW_HD)
    seg_ones = jnp.where(same_head, 1.0, 0.0).astype(BF16)
    mu = mu_ref[...]
    w0 = pv_ref[0:1, :]
    a0 = pv_ref[1:2, :]
    kk_p = pv_ref[2:3, :]
    ka_p = pv_ref[3:4, :]
    rk_p = pv_ref[4:5, :]
    ln_w = pv_ref[5:6, :]
    ln_b = pv_ref[6:7, :]
    pairs = range(RW_PAIRS)
    sls = [slice(p * lanes, (p + 1) * lanes) for p in pairs]

    def per_chunk_row(x, row):
        return jnp.concatenate([jnp.broadcast_to(x[j * c + row:j * c + row + 1, :], (c, x.shape[1]))
                                for j in range(group)], axis=0)

    def pass1(step, carry):
        b = step // n_groups
        gi = step - b * n_groups
        r0 = pl.multiple_of(step * gc, gc)
        blk = xbuf[pl.ds(pl.multiple_of(b * xrows + gi * gc, 8), gc + HALO), :]
        cur = blk[HALO:, :]
        xs = cur + (blk[HALO - 1:HALO - 1 + gc, :] - cur) * mu
        r = xs[:, 0:RW_WIDTH]
        kr = xs[:, RW_WIDTH:2 * RW_WIDTH]
        vr = xs[:, 2 * RW_WIDTH:3 * RW_WIDTH]
        lora_in = xs[:, 3 * RW_WIDTH:3 * RW_WIDTH + RW_LORA]
        xg = xs[:, 3 * RW_WIDTH + RW_LORA:RW_PROJ]
        w_log = -_softplus(-(w0 + _bdot(jnp.tanh(lora_in), w2_ref[...]))) - 0.5
        log_w = -jnp.exp(w_log)
        a = _sigmoid(a0 + _bdot(lora_in, a2_ref[...]))
        gate_scr[pl.ds(r0, gc), :] = _bdot(_sigmoid(xg), g2_ref[...])
        kk_raw = kr * kk_p
        k2 = kr * (1.0 + (a - 1.0) * ka_p)
        rk = r * k2 * rk_p
        g_cum = _cumsum_rows(ltri, log_w)
        mid_rows = [g_cum[j * c + c // 2 - 1:j * c + c // 2, :] for j in range(group)]
        end_rows = [g_cum[(j + 1) * c - 1:(j + 1) * c, :] for j in range(group)]
        g_mid = per_chunk_row(g_cum, c // 2 - 1)
        e_left = jnp.exp(g_cum - g_mid)
        e_left_prev = jnp.exp(g_cum - log_w - g_mid)
        e_right = jnp.exp(g_mid - g_cum)
        e_mid = [jnp.exp(m) for m in mid_rows]
        e_end_mid = [jnp.exp(e - m) for e, m in zip(end_rows, mid_rows)]
        for j in range(group):
            dec_scr[step * group + j] = jnp.broadcast_to(jnp.exp(end_rows[j]), (8, RW_WIDTH))
        seg = _bdot(jnp.concatenate([kk_raw[:, sl] * kk_raw[:, sl] for sl in sls]
                                    + [rk[:, sl] for sl in sls], axis=0), seg_ones)
        bonus_scr[pl.ds(r0, gc), :] = jnp.concatenate(
            [seg[(RW_PAIRS + p) * gc:(RW_PAIRS + p + 1) * gc] for p in pairs], axis=-1) * vr
        items = []
        for j in range(group):
            rs = slice(j * c, (j + 1) * c)
            for p, sl in enumerate(sls):
                kappa = kk_raw[rs, sl] * lax.rsqrt(seg[p * gc + j * c:p * gc + (j + 1) * c] + NORM_EPS)
                r_l = r[rs, sl] * e_left[rs, sl]
                kap_l = kappa * e_left_prev[rs, sl]
                k_r = k2[rs, sl] * e_right[rs, sl]
                b_r = kappa * a[rs, sl] * e_right[rs, sl]
                items.append(dict(
                    j=j, p=p, v=vr[rs, sl], kg=kap_l * e_mid[j][:, sl], rg=r_l * e_mid[j][:, sl],
                    k_d=k_r * e_end_mid[j][:, sl], b_d=b_r * e_end_mid[j][:, sl],
                    lhs=jnp.concatenate([kap_l, r_l], axis=0),
                    rhs=jnp.concatenate([jnp.where(first, b_r, 0.0), jnp.where(first, k_r, 0.0),
                                         jnp.where(first, 0.0, k_r), jnp.where(first, 0.0, b_r)], axis=0)))
        prods = [_bdot_nt(it["lhs"], it["rhs"]) for it in items]
        for it, res in zip(items, prods):
            kap_1, kap_2 = res[0:c, 0:lanes], res[0:c, lanes:2 * lanes]
            r_1, r_2 = res[c:2 * c, 0:lanes], res[c:2 * c, lanes:2 * lanes]
            it["a_cat"] = jnp.where(tri_strict, jnp.where(first, kap_1, kap_2), 0.0)
            kb_cat = jnp.where(tri_strict, jnp.where(first, kap_2, kap_1), 0.0)
            qk_cat = jnp.where(tri_incl, jnp.where(first, r_2, r_1), 0.0)
            it["kq"] = jnp.concatenate([kb_cat, qk_cat], axis=0)
            it["qb_cat"] = jnp.where(tri_incl, jnp.where(first, r_1, r_2), 0.0)
            v = it["v"]
            it["v_swap"] = jnp.concatenate([jnp.where(first, 0.0, v), jnp.where(first, v, 0.0)], axis=0)
        intras = [_bdot(it["kq"], it["v_swap"]) for it in items]
        t_invs = _pair_unit_lower_inverse([it["a_cat"] for it in items], first, eye, refine=False)
        sols = [_bdot(t_inv, jnp.concatenate([_pair_blockdiag(it["kg"], first),
                                              _pair_blockdiag(intra[0:c], first)], axis=-1))
                for it, t_inv, intra in zip(items, t_invs, intras)]
        corrs = [_bdot(it["qb_cat"], jnp.concatenate([_pair_blockdiag(sol[:, 0:lanes], first),
                                                      _pair_blockdiag(sol[:, lanes:2 * lanes], first)], axis=-1))
                 for it, sol in zip(items, sols)]
        us = [_bdot_tn(jnp.concatenate([it["v"], -sol[:, lanes:2 * lanes]], axis=0),
                       jnp.concatenate([it["k_d"], it["b_d"]], axis=0)) for it, sol in zip(items, sols)]
        ms = [_bdot_tn(sol[:, 0:lanes], it["b_d"]) for it, sol in zip(items, sols)]
        for item, intra, corr, u, m in zip(items, intras, corrs, us, ms):
            ci = step * group + item["j"]
            p = item["p"]
            rq_scr[ci, p] = (item["rg"] - corr[:, 0:lanes]).astype(BF16)
            ov_scr[pl.ds(r0 + item["j"] * c, c), sls[p]] = intra[c:2 * c] - corr[:, lanes:2 * lanes]
            u_scr[ci, p] = jnp.where(same_head, u, 0.0)
            m_scr[ci, p] = jnp.where(same_head, m, 0.0).astype(BF16)
        return carry

    lax.fori_loop(0, bb * n_groups, pass1, 0)

    def pass2(ci, carry):
        cis = [b * n_chunks + ci for b in range(bb)]
        rows = [pl.ds(pl.multiple_of(b * tt + ci * c, c), c) for b in range(bb)]
        decs = [dec_scr[cb][0:1, :] for cb in cis]
        s_old = [[s_scr[b * RW_PAIRS + p] for p in pairs] for b in range(bb)]
        s_bf = [[s.astype(BF16) for s in sb] for sb in s_old]
        o = [[_bdot_nt(rq_scr[cis[b], p], s_bf[b][p]) for p in pairs] for b in range(bb)]
        sm = [[jnp.dot(s_bf[b][p], m_scr[cis[b], p], preferred_element_type=F32) for p in pairs]
              for b in range(bb)]
        for b in range(bb):
            ov_scr[rows[b], :] = jnp.concatenate(o[b], axis=-1) + ov_scr[rows[b], :]
        s_scr[...] = jnp.stack([s_old[b][p] * decs[b][:, sls[p]] - sm[b][p] + u_scr[cis[b], p]
                                for b in range(bb) for p in pairs])
        return carry

    lax.fori_loop(0, n_chunks, pass2, 0)

    def pass3(step, carry):
        b = step // n_groups
        gi = step - b * n_groups
        r0 = pl.multiple_of(step * gc, gc)
        o = ov_scr[pl.ds(r0, gc), :]
        o_all = jnp.concatenate([o[:, sl] for sl in sls], axis=0)
        mean = _bdot(o_all, seg_ones) * (1.0 / RW_HD)
        dev = o_all - mean
        var = _bdot(dev * dev, seg_ones) * (1.0 / RW_HD)
        o_n = dev * lax.rsqrt(var + GN_EPS)
        o_n = jnp.concatenate([o_n[p * gc:(p + 1) * gc] for p in pairs], axis=-1) * ln_w + ln_b
        or_ref[b, pl.ds(pl.multiple_of(gi * gc, gc), gc), :] = (
            (o_n + bonus_scr[pl.ds(r0, gc), :]) * gate_scr[pl.ds(r0, gc), :])
        return carry

    lax.fori_loop(0, bb * n_groups, pass3, 0)
    for b in range(bb):
        xbuf[b * xrows:b * xrows + HALO, :] = xbuf[b * xrows + tt:(b + 1) * xrows, :]

    @pl.when(t_idx == pl.num_programs(1) - 1)
    def _():
        for b in range(bb):
            sout_ref[b] = s_scr[b * RW_PAIRS:(b + 1) * RW_PAIRS]


def _rwkv_call(pr, shift_halo, s0, mu, pvec, w2, a2, g2):
    bsz, seq, _ = pr.shape
    tt = min(seq, TIME_TILE)
    n_chunks = tt // CHUNK
    group = RW_GROUP if n_chunks % RW_GROUP == 0 else 1
    bb = SEQ_BLOCK if bsz % SEQ_BLOCK == 0 else 1
    assert seq % tt == 0 and tt % CHUNK == 0
    lanes = 2 * RW_HD
    smap = lambda b, t: (b, 0, 0, 0)
    cmap = lambda b, t: (0, 0)
    return pl.pallas_call(
        functools.partial(_rwkv_kernel, tt=tt, group=group, bb=bb),
        grid=(bsz // bb, seq // tt),
        in_specs=[pl.BlockSpec((bb, tt, RW_PROJ), lambda b, t: (b, t, 0)),
                  pl.BlockSpec((bb, HALO, RW_PROJ), lambda b, t: (b, 0, 0)),
                  pl.BlockSpec((bb, RW_PAIRS, lanes, lanes), smap),
                  pl.BlockSpec((1, RW_PROJ), cmap),
                  pl.BlockSpec((8, RW_WIDTH), cmap),
                  pl.BlockSpec((RW_LORA, RW_WIDTH), cmap),
                  pl.BlockSpec((RW_LORA, RW_WIDTH), cmap),
                  pl.BlockSpec((RW_LORA, RW_WIDTH), cmap)],
        out_specs=[pl.BlockSpec((bb, tt, RW_WIDTH), lambda b, t: (b, t, 0)),
                   pl.BlockSpec((bb, RW_PAIRS, lanes, lanes), smap)],
        out_shape=[jax.ShapeDtypeStruct((bsz, seq, RW_WIDTH), F32),
                   jax.ShapeDtypeStruct((bsz, RW_PAIRS, lanes, lanes), F32)],
        scratch_shapes=[pltpu.VMEM((bb * (HALO + tt), RW_PROJ), F32),
                        pltpu.VMEM((bb * RW_PAIRS, lanes, lanes), F32),
                        pltpu.VMEM((bb * n_chunks, RW_PAIRS, CHUNK, lanes), BF16),
                        pltpu.VMEM((bb * tt, RW_WIDTH), F32),
                        pltpu.VMEM((bb * n_chunks, RW_PAIRS, lanes, lanes), BF16),
                        pltpu.VMEM((bb * n_chunks, RW_PAIRS, lanes, lanes), F32),
                        pltpu.VMEM((bb * n_chunks, 8, RW_WIDTH), F32),
                        pltpu.VMEM((bb * tt, RW_WIDTH), F32),
                        pltpu.VMEM((bb * tt, RW_WIDTH), F32)],
        compiler_params=pltpu.CompilerParams(dimension_semantics=("parallel", "arbitrary"),
                                             vmem_limit_bytes=VMEM_LIMIT),
        name="rwkv7",
    )(pr, shift_halo, s0, mu, pvec, w2, a2, g2)


def _outffn_kernel(og_ref, or_ref, x_ref, g1_ref, sc_ref, sh_ref, g2_ref, wo_ref, lnp_ref,
                   w1_ref, w2_ref, o_ref, *, alpha):
    bb, tt, d = x_ref.shape
    rows = bb * tt
    y = (jnp.dot(og_ref[...].reshape(rows, GDN_WIDTH).astype(BF16), wo_ref[0:GDN_WIDTH, :],
                 preferred_element_type=F32)
         + jnp.dot(or_ref[...].reshape(rows, RW_WIDTH).astype(BF16), wo_ref[GDN_WIDTH:, :],
                   preferred_element_type=F32)).reshape(bb, tt, d)
    x1 = _layer_norm(alpha * x_ref[...] + g1_ref[...] * y, lnp_ref[0:1, :], lnp_ref[1:2, :], LN_EPS)
    hb = (x1 * (1.0 + sc_ref[...]) + sh_ref[...]).reshape(rows, d).astype(BF16)
    f = jnp.zeros((rows, d), F32)
    for j in range(D_FF // FF_TILE):
        hid = jnp.dot(hb, w1_ref[:, j * FF_TILE:(j + 1) * FF_TILE], preferred_element_type=F32)
        hid = jnp.square(jnp.maximum(hid, 0.0)).astype(BF16)
        f = f + jnp.dot(hid, w2_ref[j * FF_TILE:(j + 1) * FF_TILE, :], preferred_element_type=F32)
    o_ref[...] = _layer_norm(alpha * x1 + g2_ref[...] * f.reshape(bb, tt, d),
                             lnp_ref[2:3, :], lnp_ref[3:4, :], LN_EPS)


def _outffn_call(og, orr, x, g1, sc2, sh2, g2, wo, lnp, w1, w2, alpha):
    bsz, seq, d = x.shape
    bb, tt = _row_blocks(bsz, seq)
    xmap = lambda b, t: (b, t, 0)
    mmap = lambda b, t: (b, 0, 0)
    wmap = lambda b, t: (0, 0)
    mod_spec = pl.BlockSpec((bb, 1, d), mmap)
    return pl.pallas_call(
        functools.partial(_outffn_kernel, alpha=alpha),
        grid=(bsz // bb, seq // tt),
        in_specs=[pl.BlockSpec((bb, tt, GDN_WIDTH), xmap),
                  pl.BlockSpec((bb, tt, RW_WIDTH), xmap),
                  pl.BlockSpec((bb, tt, d), xmap),
                  mod_spec, mod_spec, mod_spec, mod_spec,
                  pl.BlockSpec(wo.shape, wmap, pipeline_mode=pl.Buffered(1)),
                  pl.BlockSpec(lnp.shape, wmap),
                  pl.BlockSpec(w1.shape, wmap, pipeline_mode=pl.Buffered(1)),
                  pl.BlockSpec(w2.shape, wmap, pipeline_mode=pl.Buffered(1))],
        out_specs=pl.BlockSpec((bb, tt, d), xmap),
        out_shape=jax.ShapeDtypeStruct((bsz, seq, d), F32),
        compiler_params=pltpu.CompilerParams(dimension_semantics=("parallel", "parallel"),
                                             vmem_limit_bytes=VMEM_LIMIT),
        name="out_ffn",
    )(og, orr, x, g1, sc2, sh2, g2, wo, lnp, w1, w2)


def _prep_layer(l, w_in, gdn_conv_w, gdn_a_log, gdn_dt_bias, gdn_norm_w, rwkv_mu, rwkv_w0, rwkv_w2,
                rwkv_a0, rwkv_a2, rwkv_g2, rwkv_kk, rwkv_ka, rwkv_rk, rwkv_ln_w, rwkv_ln_b, w_out,
                ln1_w, ln1_b, w_ff1, w_ff2, ln2_w, ln2_b):
    d = w_in.shape[1]
    n_gdn = GDN_QKV + GDN_WIDTH + 2 * GDN_HEADS
    wg = jnp.concatenate([w_in[l, :, :n_gdn], jnp.zeros((d, GDN_COLS - n_gdn), F32)], axis=1).astype(BF16)
    wr = w_in[l, :, n_gdn:].astype(BF16)
    gate_params = jnp.zeros((8, GATE_COLS), F32)
    gate_params = gate_params.at[0, :GDN_HEADS].set(gdn_a_log[l]).at[1, :GDN_HEADS].set(gdn_dt_bias[l])
    pvec = jnp.stack([rwkv_w0[l], rwkv_a0[l], rwkv_kk[l], rwkv_ka[l], rwkv_rk[l], rwkv_ln_w[l],
                      rwkv_ln_b[l], jnp.zeros((RW_WIDTH,), F32)])
    half = RW_LORA // 2
    zeros = jnp.zeros((half, RW_WIDTH), F32)
    w2 = jnp.concatenate([rwkv_w2[l], zeros], axis=0).astype(BF16)
    a2 = jnp.concatenate([zeros, rwkv_a2[l]], axis=0).astype(BF16)
    lnp = jnp.stack([ln1_w[l], ln1_b[l], ln2_w[l], ln2_b[l]] + [jnp.zeros((d,), F32)] * 4)
    return dict(wg=wg, wr=wr, conv_w=gdn_conv_w[l], gate_params=gate_params,
                norm_w=gdn_norm_w[l].reshape(1, GDN_D), mu=rwkv_mu[l].reshape(1, RW_PROJ), pvec=pvec,
                w2=w2, a2=a2, g2=rwkv_g2[l].astype(BF16), wo=w_out[l].astype(BF16), lnp=lnp,
                w1=w_ff1[l].astype(BF16), w2f=w_ff2[l].astype(BF16))


def _rwkv_state_to_pairs(s):
    bsz = s.shape[0]
    s = s.reshape(bsz, RW_PAIRS, 2, RW_HD, RW_HD)
    z = jnp.zeros_like(s[:, :, 0])
    top = jnp.concatenate([s[:, :, 0], z], axis=-1)
    bot = jnp.concatenate([z, s[:, :, 1]], axis=-1)
    return jnp.concatenate([top, bot], axis=-2)


def _rwkv_state_from_pairs(sp):
    bsz = sp.shape[0]
    return jnp.stack([sp[:, :, :RW_HD, :RW_HD], sp[:, :, RW_HD:, RW_HD:]], axis=2).reshape(
        bsz, RW_HEADS, RW_HD, RW_HD)


def _trunk(x, mods, st_gdn, st_conv, st_rw, st_shift, layers, alpha):
    bsz, seq, d = x.shape
    assert seq >= CONV_W - 1
    outs = ([], [], [], [])
    for l, p in enumerate(layers):
        sh1, sc1, g1, sh2, sc2, g2 = [mods[l, :, None, i * d:(i + 1) * d] for i in range(6)]
        pg, pr = _inproj_call(x, sc1, sh1, p["wg"], p["wr"])
        conv_halo = jnp.pad(st_conv[l], ((0, 0), (HALO - (CONV_W - 1), 0), (0, 0)))
        og, n_gdn = _gdn_call(pg, conv_halo, st_gdn[l], p["conv_w"], p["gate_params"], p["norm_w"])
        shift_halo = jnp.pad(st_shift[l][:, None, :], ((0, 0), (HALO - 1, 0), (0, 0)))
        orr, n_rw = _rwkv_call(pr, shift_halo, _rwkv_state_to_pairs(st_rw[l]), p["mu"], p["pvec"],
                               p["w2"], p["a2"], p["g2"])
        outs[0].append(n_gdn)
        outs[1].append(pg[:, seq - (CONV_W - 1):, :GDN_QKV])
        outs[2].append(_rwkv_state_from_pairs(n_rw))
        outs[3].append(pr[:, seq - 1, :])
        x = _outffn_call(og, orr, x, g1, sc2, sh2, g2, p["wo"], p["lnp"], p["w1"], p["w2f"], alpha)
    return (x,) + tuple(jnp.stack(o) for o in outs)


def kernel(x_prompt, x_sample, c_prompt, c_sample, state_gdn, state_gdn_conv, state_rwkv, state_rwkv_shift, w_ada, b_ada, w_in, gdn_conv_w, gdn_a_log, gdn_dt_bias, gdn_norm_w, rwkv_mu, rwkv_w0, rwkv_w2, rwkv_a0, rwkv_a2, rwkv_g2, rwkv_kk, rwkv_ka, rwkv_rk, rwkv_ln_w, rwkv_ln_b, w_out, ln1_w, ln1_b, w_ff1, w_ff2, ln2_w, ln2_b):
    depth = w_in.shape[0]
    alpha = (2 * depth) ** DEPTH_ALPHA_POW
    bp = x_prompt.shape[0]
    layers = [_prep_layer(l, w_in, gdn_conv_w, gdn_a_log, gdn_dt_bias, gdn_norm_w, rwkv_mu, rwkv_w0,
                          rwkv_w2, rwkv_a0, rwkv_a2, rwkv_g2, rwkv_kk, rwkv_ka, rwkv_rk, rwkv_ln_w,
                          rwkv_ln_b, w_out, ln1_w, ln1_b, w_ff1, w_ff2, ln2_w, ln2_b)
              for l in range(depth)]
    mods = _mod_call(jnp.concatenate([c_prompt, c_sample], axis=0), w_ada, b_ada)
    zeros = lambda *shape: jnp.zeros((depth, bp) + shape, F32)
    y_p, p_gdn, p_conv, p_rw, p_shift = _trunk(
        x_prompt, mods[:, :bp],
        zeros(GDN_HEADS, GDN_D, GDN_D), zeros(CONV_W - 1, GDN_QKV),
        zeros(RW_HEADS, RW_HD, RW_HD), zeros(RW_PROJ), layers, alpha)
    y_s, s_gdn, s_conv, s_rw, s_shift = _trunk(
        x_sample, mods[:, bp:], state_gdn, state_gdn_conv, state_rwkv, state_rwkv_shift, layers, alpha)
    return (y_p, y_s, p_gdn, p_conv, p_rw, p_shift, s_gdn, s_conv, s_rw, s_shift)
```

```python
import functools

import jax
import jax.numpy as jnp
from jax import lax
from jax.experimental import pallas as pl
from jax.experimental.pallas import tpu as pltpu

F32 = jnp.float32
BF16 = jnp.bfloat16

D_MODEL = 1024
CHUNK = 64
GDN_HEADS = 4
GDN_D = 128
GDN_WIDTH = GDN_HEADS * GDN_D
GDN_QKV = 3 * GDN_WIDTH
CONV_W = 4
GATE_COLS = 128
GDN_COLS = GDN_QKV + GDN_WIDTH + GATE_COLS
RW_HEADS = 8
RW_HD = 64
RW_WIDTH = RW_HEADS * RW_HD
RW_PAIRS = RW_HEADS // 2
RW_LORA = 128
RW_PROJ = 3 * RW_WIDTH + 2 * RW_LORA
D_FF = 4 * D_MODEL
DEPTH_ALPHA_POW = 0.25
LN_EPS = 1e-5
GN_EPS = 64e-5
NORM_EPS = 1e-6
HALO = 8
ROW_TILE = 512
TIME_TILE = 512
FF_TILE = 1024
GDN_GROUP = 8
RW_GROUP = 8
SEQ_BLOCK = 2
VMEM_LIMIT = 56 * 1024 * 1024


def _bdot(a, b):
    return jnp.dot(a.astype(BF16), b.astype(BF16), preferred_element_type=F32)


def _bdot_nt(a, b):
    return lax.dot_general(a.astype(BF16), b.astype(BF16), (((1,), (1,)), ((), ())),
                           preferred_element_type=F32)


def _bdot_tn(a, b):
    return lax.dot_general(a.astype(BF16), b.astype(BF16), (((0,), (0,)), ((), ())),
                           preferred_element_type=F32)


def _split2(x):
    hi = x.astype(BF16)
    return hi, (x - hi.astype(F32)).astype(BF16)


def _dot3(a, b):
    ah, al = _split2(a)
    bh, bl = _split2(b)
    return (jnp.dot(ah, bh, preferred_element_type=F32) + jnp.dot(ah, bl, preferred_element_type=F32)
            + jnp.dot(al, bh, preferred_element_type=F32))


def _cumsum_rows(ltri_bf16, x):
    hi = x.astype(BF16)
    rest = x - hi.astype(F32)
    mid = rest.astype(BF16)
    lo = (rest - mid.astype(F32)).astype(BF16)
    return (jnp.dot(ltri_bf16, hi, preferred_element_type=F32)
            + jnp.dot(ltri_bf16, mid, preferred_element_type=F32)
            + jnp.dot(ltri_bf16, lo, preferred_element_type=F32))


def _sigmoid(x):
    return 1.0 / (1.0 + jnp.exp(-x))


def _silu(x):
    return x * _sigmoid(x)


def _softplus(x):
    return jnp.maximum(x, 0.0) + jnp.log(1.0 + jnp.exp(-jnp.abs(x)))


def _pair_blockdiag(x, first):
    zero = jnp.zeros_like(x)
    return jnp.concatenate([jnp.where(first, x, zero), jnp.where(first, zero, x)], axis=0)


def _pair_unit_lower_inverse(a_list, first, eye, refine):
    n = eye.shape[0]
    xs = [eye - a for a in a_list]
    ps = list(a_list)
    bds = [_pair_blockdiag(p.astype(BF16), first) for p in ps]
    terms = 2
    while terms < n:
        ps = [jnp.dot(p.astype(BF16), bd, preferred_element_type=F32) for p, bd in zip(ps, bds)]
        bds = [_pair_blockdiag(p.astype(BF16), first) for p in ps]
        xs = [x + jnp.dot(x.astype(BF16), bd, preferred_element_type=F32) for x, bd in zip(xs, bds)]
        terms *= 2
    if not refine:
        return xs
    es = [_dot3(eye + a, _pair_blockdiag(x, first)) - eye for a, x in zip(a_list, xs)]
    return [x - _bdot(x, _pair_blockdiag(e, first)) for x, e in zip(xs, es)]


def _layer_norm(t, w, b, eps):
    mu = jnp.mean(t, axis=-1, keepdims=True)
    d = t - mu
    var = jnp.mean(d * d, axis=-1, keepdims=True)
    return d * lax.rsqrt(var + eps) * w + b


def _mod_kernel(c_ref, w_ref, b_ref, o_ref):
    o_ref[0] = _bdot(_silu(c_ref[...]), w_ref[0]) + b_ref[0]


def _mod_call(c_all, w_ada, b_ada):
    depth, d, n = w_ada.shape
    rows = c_all.shape[0]
    tn = 1536
    return pl.pallas_call(
        _mod_kernel,
        grid=(depth, n // tn),
        in_specs=[pl.BlockSpec((rows, d), lambda l, j: (0, 0)),
                  pl.BlockSpec((1, d, tn), lambda l, j: (l, 0, j)),
                  pl.BlockSpec((1, 1, tn), lambda l, j: (l, 0, j))],
        out_specs=pl.BlockSpec((1, rows, tn), lambda l, j: (l, 0, j)),
        out_shape=jax.ShapeDtypeStruct((depth, rows, n), F32),
        compiler_params=pltpu.CompilerParams(dimension_semantics=("parallel", "parallel"),
                                             vmem_limit_bytes=VMEM_LIMIT),
        name="adaln_mod",
    )(c_all, w_ada, b_ada.reshape(depth, 1, n))


def _inproj_kernel(x_ref, sc_ref, sh_ref, wg_ref, wr_ref, og_ref, or_ref):
    bb, tt, d = x_ref.shape
    h = x_ref[...] * (1.0 + sc_ref[...]) + sh_ref[...]
    hb = h.reshape(bb * tt, d).astype(BF16)
    og_ref[...] = jnp.dot(hb, wg_ref[0], preferred_element_type=F32).reshape(og_ref.shape)
    or_ref[...] = jnp.dot(hb, wr_ref[0], preferred_element_type=F32).reshape(or_ref.shape)


def _row_blocks(bsz, seq):
    tt = min(seq, ROW_TILE)
    bb = max(1, min(bsz, ROW_TILE // tt))
    assert seq % tt == 0 and bsz % bb == 0 and tt % 8 == 0
    return bb, tt


def _layer_spec(arr, layer, single_buffer=False):
    kwargs = dict(pipeline_mode=pl.Buffered(1)) if single_buffer else {}
    return pl.BlockSpec((1,) + arr.shape[1:], lambda b, t: (layer,) + (0,) * (arr.ndim - 1), **kwargs)


def _inproj_call(x, sc, sh, wg, wr, layer):
    bsz, seq, d = x.shape
    bb, tt = _row_blocks(bsz, seq)
    xmap = lambda b, t: (b, t, 0)
    mmap = lambda b, t: (b, 0, 0)
    return pl.pallas_call(
        _inproj_kernel,
        grid=(bsz // bb, seq // tt),
        in_specs=[pl.BlockSpec((bb, tt, d), xmap),
                  pl.BlockSpec((bb, 1, d), mmap),
                  pl.BlockSpec((bb, 1, d), mmap),
                  _layer_spec(wg, layer, single_buffer=True),
                  _layer_spec(wr, layer, single_buffer=True)],
        out_specs=[pl.BlockSpec((bb, tt, GDN_COLS), xmap),
                   pl.BlockSpec((bb, tt, RW_PROJ), xmap)],
        out_shape=[jax.ShapeDtypeStruct((bsz, seq, GDN_COLS), F32),
                   jax.ShapeDtypeStruct((bsz, seq, RW_PROJ), F32)],
        compiler_params=pltpu.CompilerParams(dimension_semantics=("parallel", "parallel"),
                                             vmem_limit_bytes=VMEM_LIMIT),
        name="in_proj",
    )(x, sc, sh, wg, wr)


def _gdn_kernel(pg_ref, cst_ref, s0_ref, cw_ref, gp_ref, nw_ref, og_ref, sout_ref,
                xbuf, s_scr, qd_scr, ov_scr, m_scr, u_scr, dl_scr, *, tt, group, bb):
    c = CHUNK
    gc = group * c
    n_chunks = tt // c
    n_groups = n_chunks // group
    xrows = HALO + tt
    d = GDN_D
    t_idx = pl.program_id(1)

    @pl.when(t_idx == 0)
    def _():
        for b in range(bb):
            xbuf[b * xrows:b * xrows + HALO, :] = cst_ref[b]
            s_scr[b * GDN_HEADS:(b + 1) * GDN_HEADS] = s0_ref[b]

    for b in range(bb):
        xbuf[b * xrows + HALO:(b + 1) * xrows, :] = pg_ref[b, :, 0:GDN_QKV]

    row = lax.broadcasted_iota(jnp.int32, (c, 2 * c), 0)
    lane = lax.broadcasted_iota(jnp.int32, (c, 2 * c), 1)
    first = lane < c
    col = lane & (c - 1)
    causal = col <= row
    strict = col < row
    eye = jnp.where(col == row, 1.0, 0.0).astype(F32)
    grow = lax.broadcasted_iota(jnp.int32, (gc, gc), 0)
    gcol = lax.broadcasted_iota(jnp.int32, (gc, gc), 1)
    ltri = jnp.where(((grow & -c) == (gcol & -c)) & (gcol <= grow), 1.0, 0.0).astype(BF16)
    ones_d = jnp.ones((d, d), BF16)
    zero_2d = jnp.zeros((c, 2 * d), F32)
    cw = cw_ref[0]
    neg_a = -jnp.exp(gp_ref[0, 0:1, :])
    dt_bias = gp_ref[0, 1:2, :]
    nw = nw_ref[0]
    heads = range(GDN_HEADS)

    def per_chunk_row(x, r):
        return jnp.concatenate([jnp.broadcast_to(x[j * c + r:j * c + r + 1, :], (c, x.shape[1]))
                                for j in range(group)], axis=0)

    def pass1(step, carry):
        b = step // n_groups
        gi = step - b * n_groups
        r0 = pl.multiple_of(step * gc, gc)
        t0 = pl.multiple_of(gi * gc, gc)
        blk = xbuf[pl.ds(pl.multiple_of(b * xrows + gi * gc, 8), gc + HALO), :]
        prev = pltpu.roll(blk, 1, axis=0)
        acc = (blk * cw[3:4, :] + prev * cw[2:3, :]
               + pltpu.roll(blk * cw[1:2, :] + prev * cw[0:1, :], 2, axis=0))
        qkv = _silu(acc[HALO:, :])
        gates = pg_ref[b, pl.ds(t0, gc), GDN_QKV + GDN_WIDTH:GDN_COLS]
        log_a = neg_a * _softplus(gates + dt_bias)
        beta_all = _sigmoid(gates)
        g_all = _cumsum_rows(ltri, log_a)
        g_rows = g_all.T
        eg_all = jnp.exp(g_all)
        g_last = per_chunk_row(g_all, c - 1)
        dec_all = jnp.exp(g_last - g_all)
        for j in range(group):
            dl_scr[step * group + j] = jnp.broadcast_to(
                jnp.exp(g_all[(j + 1) * c - 1:(j + 1) * c, :]), (8, GATE_COLS))
        q = [qkv[:, h * d:(h + 1) * d] for h in heads]
        k = [qkv[:, GDN_WIDTH + h * d:GDN_WIDTH + (h + 1) * d] for h in heads]
        v = [qkv[:, 2 * GDN_WIDTH + h * d:2 * GDN_WIDTH + (h + 1) * d] for h in heads]
        ss = _bdot(jnp.concatenate([t * t for t in q + k], axis=0), ones_d)
        qn = [q[h] * lax.rsqrt(ss[h * gc:(h + 1) * gc] + NORM_EPS) * (d ** -0.5) for h in heads]
        kn = [k[h] * lax.rsqrt(ss[(GDN_HEADS + h) * gc:(GDN_HEADS + h + 1) * gc] + NORM_EPS) for h in heads]
        items = []
        for j in range(group):
            rs = slice(j * c, (j + 1) * c)
            for p in range(GDN_HEADS // 2):
                pair = (2 * p, 2 * p + 1)
                items.append(dict(
                    j=j, pair=pair, rs=rs,
                    ks2=jnp.concatenate([kn[h][rs] for h in pair], axis=0).astype(BF16),
                    qs2=jnp.concatenate([qn[h][rs] for h in pair], axis=0).astype(BF16)))
        kk2s = [_bdot_nt(it["ks2"], it["ks2"]) for it in items]
        qk2s = [_bdot_nt(it["qs2"], it["ks2"]) for it in items]
        for it, kk2, qk2 in zip(items, kk2s, qk2s):
            (h1, h2), rs = it["pair"], it["rs"]
            kk_cat = jnp.where(first, kk2[0:c], kk2[c:2 * c])
            qk_cat = jnp.where(first, qk2[0:c], qk2[c:2 * c])
            g_col_cat = jnp.where(first, g_all[rs, h1:h1 + 1], g_all[rs, h2:h2 + 1])
            g_row_cat = jnp.concatenate([g_rows[h1:h1 + 1, rs], g_rows[h2:h2 + 1, rs]], axis=1)
            beta_cat = jnp.where(first, beta_all[rs, GDN_HEADS + h1:GDN_HEADS + h1 + 1],
                                 beta_all[rs, GDN_HEADS + h2:GDN_HEADS + h2 + 1])
            gam = jnp.where(causal, jnp.exp(jnp.where(causal, g_col_cat - g_row_cat, 0.0)), 0.0)
            it["a_cat"] = jnp.where(strict, beta_cat * kk_cat * gam, 0.0)
            it["qk_cat"] = jnp.where(causal, qk_cat * gam, 0.0)
            rhs = []
            for h in (h1, h2):
                beta = beta_all[rs, GDN_HEADS + h:GDN_HEADS + h + 1]
                rhs.append(jnp.concatenate([v[h][rs] * beta, kn[h][rs] * (beta * eg_all[rs, h:h + 1])], axis=-1))
            it["rhs_bd"] = jnp.concatenate([jnp.concatenate([rhs[0], zero_2d], axis=-1),
                                            jnp.concatenate([zero_2d, rhs[1]], axis=-1)], axis=0)
        t_invs = _pair_unit_lower_inverse([it["a_cat"] for it in items], first, eye, refine=True)
        sols = [_bdot(t_inv, it["rhs_bd"]) for it, t_inv in zip(items, t_invs)]
        corrs = [_bdot(it["qk_cat"], jnp.concatenate(
            [jnp.concatenate([sol[:, d:2 * d], sol[:, 0:d], zero_2d], axis=-1),
             jnp.concatenate([zero_2d, sol[:, 3 * d:4 * d], sol[:, 2 * d:3 * d]], axis=-1)], axis=0))
            for it, sol in zip(items, sols)]
        mus = [[_bdot_tn(kn[h][it["rs"]] * dec_all[it["rs"], h:h + 1],
                         jnp.concatenate([sol[:, (2 * i + 1) * d:(2 * i + 2) * d],
                                          sol[:, 2 * i * d:(2 * i + 1) * d]], axis=-1))
                for i, h in enumerate(it["pair"])] for it, sol in zip(items, sols)]
        for it, corr, mu_pair in zip(items, corrs, mus):
            ci = step * group + it["j"]
            rs = it["rs"]
            for i, h in enumerate(it["pair"]):
                qd_scr[ci, h] = (qn[h][rs] * eg_all[rs, h:h + 1] - corr[:, 2 * i * d:(2 * i + 1) * d]).astype(BF16)
                ov_scr[pl.ds(r0 + it["j"] * c, c), h * d:(h + 1) * d] = corr[:, (2 * i + 1) * d:(2 * i + 2) * d]
                m_scr[ci, h] = mu_pair[i][:, 0:d].astype(BF16)
                u_scr[ci, h] = mu_pair[i][:, d:2 * d]
        return carry

    lax.fori_loop(0, bb * n_groups, pass1, 0)

    def pass2(ci, carry):
        cis = [b * n_chunks + ci for b in range(bb)]
        rows = [pl.ds(pl.multiple_of(b * tt + ci * c, c), c) for b in range(bb)]
        dls = [dl_scr[cb][0:1, :] for cb in cis]
        s_old = [[s_scr[b * GDN_HEADS + h] for h in heads] for b in range(bb)]
        s_bf = [[s.astype(BF16) for s in sb] for sb in s_old]
        o = [[jnp.dot(qd_scr[cis[b], h], s_bf[b][h], preferred_element_type=F32) for h in heads]
             for b in range(bb)]
        ms = [[jnp.dot(m_scr[cis[b], h], s_bf[b][h], preferred_element_type=F32) for h in heads]
              for b in range(bb)]
        for b in range(bb):
            ov_scr[rows[b], :] = jnp.concatenate(o[b], axis=-1) + ov_scr[rows[b], :]
        s_scr[...] = jnp.stack([s_old[b][h] * dls[b][:, h:h + 1] - ms[b][h] + u_scr[cis[b], h]
                                for b in range(bb) for h in heads])
        return carry

    lax.fori_loop(0, n_chunks, pass2, 0)

    def pass3(step, carry):
        b = step // n_groups
        t0 = pl.multiple_of((step - b * n_groups) * gc, gc)
        r0 = pl.multiple_of(step * gc, gc)
        o = ov_scr[pl.ds(r0, gc), :]
        o_all = jnp.concatenate([o[:, h * d:(h + 1) * d] for h in heads], axis=0)
        ms = _bdot(o_all * o_all, ones_d) * (1.0 / d)
        o_all = o_all * lax.rsqrt(ms + NORM_EPS) * nw
        z = pg_ref[b, pl.ds(t0, gc), GDN_QKV:GDN_QKV + GDN_WIDTH]
        og_ref[b, pl.ds(t0, gc), :] = jnp.concatenate(
            [o_all[h * gc:(h + 1) * gc] for h in heads], axis=-1) * _silu(z)
        return carry

    lax.fori_loop(0, bb * n_groups, pass3, 0)
    for b in range(bb):
        xbuf[b * xrows:b * xrows + HALO, :] = xbuf[b * xrows + tt:(b + 1) * xrows, :]

    @pl.when(t_idx == pl.num_programs(1) - 1)
    def _():
        for b in range(bb):
            sout_ref[b] = s_scr[b * GDN_HEADS:(b + 1) * GDN_HEADS]


def _gdn_call(pg, conv_halo, s0, conv_w, gate_params, norm_w, layer):
    bsz, seq, _ = pg.shape
    tt = min(seq, TIME_TILE)
    n_chunks = tt // CHUNK
    group = GDN_GROUP if n_chunks % GDN_GROUP == 0 else 1
    bb = SEQ_BLOCK if bsz % SEQ_BLOCK == 0 else 1
    assert seq % tt == 0 and tt % CHUNK == 0
    smap = lambda b, t: (b, 0, 0, 0)
    return pl.pallas_call(
        functools.partial(_gdn_kernel, tt=tt, group=group, bb=bb),
        grid=(bsz // bb, seq // tt),
        in_specs=[pl.BlockSpec((bb, tt, GDN_COLS), lambda b, t: (b, t, 0)),
                  pl.BlockSpec((bb, HALO, GDN_QKV), lambda b, t: (b, 0, 0)),
                  pl.BlockSpec((bb, GDN_HEADS, GDN_D, GDN_D), smap),
                  _layer_spec(conv_w, layer), _layer_spec(gate_params, layer), _layer_spec(norm_w, layer)],
        out_specs=[pl.BlockSpec((bb, tt, GDN_WIDTH), lambda b, t: (b, t, 0)),
                   pl.BlockSpec((bb, GDN_HEADS, GDN_D, GDN_D), smap)],
        out_shape=[jax.ShapeDtypeStruct((bsz, seq, GDN_WIDTH), F32),
                   jax.ShapeDtypeStruct((bsz, GDN_HEADS, GDN_D, GDN_D), F32)],
        scratch_shapes=[pltpu.VMEM((bb * (HALO + tt), GDN_QKV), F32),
                        pltpu.VMEM((bb * GDN_HEADS, GDN_D, GDN_D), F32),
                        pltpu.VMEM((bb * n_chunks, GDN_HEADS, CHUNK, GDN_D), BF16),
                        pltpu.VMEM((bb * tt, GDN_WIDTH), F32),
                        pltpu.VMEM((bb * n_chunks, GDN_HEADS, GDN_D, GDN_D), BF16),
                        pltpu.VMEM((bb * n_chunks, GDN_HEADS, GDN_D, GDN_D), F32),
                        pltpu.VMEM((bb * n_chunks, 8, GATE_COLS), F32)],
        compiler_params=pltpu.CompilerParams(dimension_semantics=("parallel", "arbitrary"),
                                             vmem_limit_bytes=VMEM_LIMIT),
        name="gated_deltanet",
    )(pg, conv_halo, s0, conv_w, gate_params, norm_w)


def _rwkv_kernel(pr_ref, sh0_ref, s0_ref, mu_ref, pv_ref, w2_ref, a2_ref, g2_ref,
                 or_ref, sout_ref, xbuf, s_scr, rq_scr, ov_scr, m_scr, u_scr, dec_scr, gate_scr, bonus_scr,
                 *, tt, group, bb):
    c = CHUNK
    gc = group * c
    lanes = 2 * RW_HD
    n_chunks = tt // c
    n_groups = n_chunks // group
    xrows = HALO + tt
    t_idx = pl.program_id(1)

    @pl.when(t_idx == 0)
    def _():
        for b in range(bb):
            xbuf[b * xrows:b * xrows + HALO, :] = sh0_ref[b]
            s_scr[b * RW_PAIRS:(b + 1) * RW_PAIRS] = s0_ref[b]

    for b in range(bb):
        xbuf[b * xrows + HALO:(b + 1) * xrows, :] = pr_ref[b]

    grow = lax.broadcasted_iota(jnp.int32, (gc, gc), 0)
    gcol = lax.broadcasted_iota(jnp.int32, (gc, gc), 1)
    same_chunk = (grow & -c) == (gcol & -c)
    ltri = jnp.where(same_chunk & (gcol <= grow), 1.0, 0.0).astype(BF16)
    prow = lax.broadcasted_iota(jnp.int32, (c, lanes), 0)
    plane = lax.broadcasted_iota(jnp.int32, (c, lanes), 1)
    first = plane < RW_HD
    pcol = plane & (RW_HD - 1)
    tri_strict = pcol < prow
    tri_incl = pcol <= prow
    eye = jnp.where(pcol == prow, 1.0, 0.0).astype(F32)
    srow = lax.broadcasted_iota(jnp.int32, (lanes, lanes), 0)
    scol = lax.broadcasted_iota(jnp.int32, (lanes, lanes), 1)
    same_head = (srow < RW_HD) == (scol < RW_HD)
    seg_ones = jnp.where(same_head, 1.0, 0.0).astype(BF16)
    mu = mu_ref[0]
    w0 = pv_ref[0, 0:1, :]
    a0 = pv_ref[0, 1:2, :]
    kk_p = pv_ref[0, 2:3, :]
    ka_p = pv_ref[0, 3:4, :]
    rk_p = pv_ref[0, 4:5, :]
    ln_w = pv_ref[0, 5:6, :]
    ln_b = pv_ref[0, 6:7, :]
    pairs = range(RW_PAIRS)
    sls = [slice(p * lanes, (p + 1) * lanes) for p in pairs]

    def per_chunk_row(x, row):
        return jnp.concatenate([jnp.broadcast_to(x[j * c + row:j * c + row + 1, :], (c, x.shape[1]))
                                for j in range(group)], axis=0)

    def pass1(step, carry):
        b = step // n_groups
        gi = step - b * n_groups
        r0 = pl.multiple_of(step * gc, gc)
        blk = xbuf[pl.ds(pl.multiple_of(b * xrows + gi * gc, 8), gc + HALO), :]
        xs = (blk + (pltpu.roll(blk, 1, axis=0) - blk) * mu)[HALO:, :]
        r = xs[:, 0:RW_WIDTH]
        kr = xs[:, RW_WIDTH:2 * RW_WIDTH]
        vr = xs[:, 2 * RW_WIDTH:3 * RW_WIDTH]
        lora_in = xs[:, 3 * RW_WIDTH:3 * RW_WIDTH + RW_LORA]
        xg = xs[:, 3 * RW_WIDTH + RW_LORA:RW_PROJ]
        w_log = -_softplus(-(w0 + _bdot(jnp.tanh(lora_in), w2_ref[0]))) - 0.5
        log_w = -jnp.exp(w_log)
        a = _sigmoid(a0 + _bdot(lora_in, a2_ref[0]))
        gate_scr[pl.ds(r0, gc), :] = _bdot(_sigmoid(xg), g2_ref[0])
        kk_raw = kr * kk_p
        k2 = kr * (1.0 + (a - 1.0) * ka_p)
        rk = r * k2 * rk_p
        g_cum = _cumsum_rows(ltri, log_w)
        mid_rows = [g_cum[j * c + c // 2 - 1:j * c + c // 2, :] for j in range(group)]
        end_rows = [g_cum[(j + 1) * c - 1:(j + 1) * c, :] for j in range(group)]
        g_mid = per_chunk_row(g_cum, c // 2 - 1)
        e_left = jnp.exp(g_cum - g_mid)
        e_left_prev = jnp.exp(g_cum - log_w - g_mid)
        e_right = jnp.exp(g_mid - g_cum)
        e_mid = [jnp.exp(m) for m in mid_rows]
        e_end_mid = [jnp.exp(e - m) for e, m in zip(end_rows, mid_rows)]
        for j in range(group):
            dec_scr[step * group + j] = jnp.broadcast_to(jnp.exp(end_rows[j]), (8, RW_WIDTH))
        seg = _bdot(jnp.concatenate([kk_raw[:, sl] * kk_raw[:, sl] for sl in sls]
                                    + [rk[:, sl] for sl in sls], axis=0), seg_ones)
        bonus_scr[pl.ds(r0, gc), :] = jnp.concatenate(
            [seg[(RW_PAIRS + p) * gc:(RW_PAIRS + p + 1) * gc] for p in pairs], axis=-1) * vr
        items = []
        for j in range(group):
            rs = slice(j * c, (j + 1) * c)
            for p, sl in enumerate(sls):
                kappa = kk_raw[rs, sl] * lax.rsqrt(seg[p * gc + j * c:p * gc + (j + 1) * c] + NORM_EPS)
                r_l = r[rs, sl] * e_left[rs, sl]
                kap_l = kappa * e_left_prev[rs, sl]
                k_r = k2[rs, sl] * e_right[rs, sl]
                b_r = kappa * a[rs, sl] * e_right[rs, sl]
                items.append(dict(
                    j=j, p=p, v=vr[rs, sl], kg=kap_l * e_mid[j][:, sl], rg=r_l * e_mid[j][:, sl],
                    k_d=k_r * e_end_mid[j][:, sl], b_d=b_r * e_end_mid[j][:, sl],
                    lhs=jnp.concatenate([kap_l, r_l], axis=0),
                    rhs=jnp.concatenate([jnp.where(first, b_r, 0.0), jnp.where(first, k_r, 0.0),
                                         jnp.where(first, 0.0, k_r), jnp.where(first, 0.0, b_r)], axis=0)))
        prods = [_bdot_nt(it["lhs"], it["rhs"]) for it in items]
        for it, res in zip(items, prods):
            kap_1, kap_2 = res[0:c, 0:lanes], res[0:c, lanes:2 * lanes]
            r_1, r_2 = res[c:2 * c, 0:lanes], res[c:2 * c, lanes:2 * lanes]
            it["a_cat"] = jnp.where(tri_strict, jnp.where(first, kap_1, kap_2), 0.0)
            kb_cat = jnp.where(tri_strict, jnp.where(first, kap_2, kap_1), 0.0)
            qk_cat = jnp.where(tri_incl, jnp.where(first, r_2, r_1), 0.0)
            it["kq"] = jnp.concatenate([kb_cat, qk_cat], axis=0)
            it["qb_cat"] = jnp.where(tri_incl, jnp.where(first, r_1, r_2), 0.0)
            v = it["v"]
            it["v_swap"] = jnp.concatenate([jnp.where(first, 0.0, v), jnp.where(first, v, 0.0)], axis=0)
        intras = [_bdot(it["kq"], it["v_swap"]) for it in items]
        t_invs = _pair_unit_lower_inverse([it["a_cat"] for it in items], first, eye, refine=False)
        sols = [_bdot(t_inv, jnp.concatenate([_pair_blockdiag(it["kg"], first),
                                              _pair_blockdiag(intra[0:c], first)], axis=-1))
                for it, t_inv, intra in zip(items, t_invs, intras)]
        corrs = [_bdot(it["qb_cat"], jnp.concatenate([_pair_blockdiag(sol[:, 0:lanes], first),
                                                      _pair_blockdiag(sol[:, lanes:2 * lanes], first)], axis=-1))
                 for it, sol in zip(items, sols)]
        us = [_bdot_tn(jnp.concatenate([it["v"], -sol[:, lanes:2 * lanes]], axis=0),
                       jnp.concatenate([it["k_d"], it["b_d"]], axis=0)) for it, sol in zip(items, sols)]
        ms = [_bdot_tn(sol[:, 0:lanes], it["b_d"]) for it, sol in zip(items, sols)]
        for item, intra, corr, u, m in zip(items, intras, corrs, us, ms):
            ci = step * group + item["j"]
            p = item["p"]
            rq_scr[ci, p] = (item["rg"] - corr[:, 0:lanes]).astype(BF16)
            ov_scr[pl.ds(r0 + item["j"] * c, c), sls[p]] = intra[c:2 * c] - corr[:, lanes:2 * lanes]
            u_scr[ci, p] = jnp.where(same_head, u, 0.0)
            m_scr[ci, p] = jnp.where(same_head, m, 0.0).astype(BF16)
        return carry

    lax.fori_loop(0, bb * n_groups, pass1, 0)

    def pass2(ci, carry):
        cis = [b * n_chunks + ci for b in range(bb)]
        rows = [pl.ds(pl.multiple_of(b * tt + ci * c, c), c) for b in range(bb)]
        decs = [dec_scr[cb][0:1, :] for cb in cis]
        s_old = [[s_scr[b * RW_PAIRS + p] for p in pairs] for b in range(bb)]
        s_bf = [[s.astype(BF16) for s in sb] for sb in s_old]
        o = [[_bdot_nt(rq_scr[cis[b], p], s_bf[b][p]) for p in pairs] for b in range(bb)]
        sm = [[jnp.dot(s_bf[b][p], m_scr[cis[b], p], preferred_element_type=F32) for p in pairs]
              for b in range(bb)]
        for b in range(bb):
            ov_scr[rows[b], :] = jnp.concatenate(o[b], axis=-1) + ov_scr[rows[b], :]
        s_scr[...] = jnp.stack([s_old[b][p] * decs[b][:, sls[p]] - sm[b][p] + u_scr[cis[b], p]
                                for b in range(bb) for p in pairs])
        return carry

    lax.fori_loop(0, n_chunks, pass2, 0)

    def pass3(step, carry):
        b = step // n_groups
        gi = step - b * n_groups
        r0 = pl.multiple_of(step * gc, gc)
        o = ov_scr[pl.ds(r0, gc), :]
        o_all = jnp.concatenate([o[:, sl] for sl in sls], axis=0)
        mean = _bdot(o_all, seg_ones) * (1.0 / RW_HD)
        dev = o_all - mean
        var = _bdot(dev * dev, seg_ones) * (1.0 / RW_HD)
        o_n = dev * lax.rsqrt(var + GN_EPS)
        o_n = jnp.concatenate([o_n[p * gc:(p + 1) * gc] for p in pairs], axis=-1) * ln_w + ln_b
        or_ref[b, pl.ds(pl.multiple_of(gi * gc, gc), gc), :] = (
            (o_n + bonus_scr[pl.ds(r0, gc), :]) * gate_scr[pl.ds(r0, gc), :])
        return carry

    lax.fori_loop(0, bb * n_groups, pass3, 0)
    for b in range(bb):
        xbuf[b * xrows:b * xrows + HALO, :] = xbuf[b * xrows + tt:(b + 1) * xrows, :]

    @pl.when(t_idx == pl.num_programs(1) - 1)
    def _():
        for b in range(bb):
            sout_ref[b] = s_scr[b * RW_PAIRS:(b + 1) * RW_PAIRS]


def _rwkv_call(pr, shift_halo, s0, mu, pvec, w2, a2, g2, layer):
    bsz, seq, _ = pr.shape
    tt = min(seq, TIME_TILE)
    n_chunks = tt // CHUNK
    group = RW_GROUP if n_chunks % RW_GROUP == 0 else 1
    bb = SEQ_BLOCK if bsz % SEQ_BLOCK == 0 else 1
    assert seq % tt == 0 and tt % CHUNK == 0
    lanes = 2 * RW_HD
    smap = lambda b, t: (b, 0, 0, 0)
    return pl.pallas_call(
        functools.partial(_rwkv_kernel, tt=tt, group=group, bb=bb),
        grid=(bsz // bb, seq // tt),
        in_specs=[pl.BlockSpec((bb, tt, RW_PROJ), lambda b, t: (b, t, 0)),
                  pl.BlockSpec((bb, HALO, RW_PROJ), lambda b, t: (b, 0, 0)),
                  pl.BlockSpec((bb, RW_PAIRS, lanes, lanes), smap),
                  _layer_spec(mu, layer), _layer_spec(pvec, layer),
                  _layer_spec(w2, layer), _layer_spec(a2, layer), _layer_spec(g2, layer)],
        out_specs=[pl.BlockSpec((bb, tt, RW_WIDTH), lambda b, t: (b, t, 0)),
                   pl.BlockSpec((bb, RW_PAIRS, lanes, lanes), smap)],
        out_shape=[jax.ShapeDtypeStruct((bsz, seq, RW_WIDTH), F32),
                   jax.ShapeDtypeStruct((bsz, RW_PAIRS, lanes, lanes), F32)],
        scratch_shapes=[pltpu.VMEM((bb * (HALO + tt), RW_PROJ), F32),
                        pltpu.VMEM((bb * RW_PAIRS, lanes, lanes), F32),
                        pltpu.VMEM((bb * n_chunks, RW_PAIRS, CHUNK, lanes), BF16),
                        pltpu.VMEM((bb * tt, RW_WIDTH), F32),
                        pltpu.VMEM((bb * n_chunks, RW_PAIRS, lanes, lanes), BF16),
                        pltpu.VMEM((bb * n_chunks, RW_PAIRS, lanes, lanes), F32),
                        pltpu.VMEM((bb * n_chunks, 8, RW_WIDTH), F32),
                        pltpu.VMEM((bb * tt, RW_WIDTH), F32),
                        pltpu.VMEM((bb * tt, RW_WIDTH), F32)],
        compiler_params=pltpu.CompilerParams(dimension_semantics=("parallel", "arbitrary"),
                                             vmem_limit_bytes=VMEM_LIMIT),
        name="rwkv7",
    )(pr, shift_halo, s0, mu, pvec, w2, a2, g2)


def _outffn_kernel(og_ref, or_ref, x_ref, g1_ref, sc_ref, sh_ref, g2_ref, wo_ref, lnp_ref,
                   w1_ref, w2_ref, o_ref, *, alpha):
    bb, tt, d = x_ref.shape
    rows = bb * tt
    y = (jnp.dot(og_ref[...].reshape(rows, GDN_WIDTH).astype(BF16), wo_ref[0, 0:GDN_WIDTH, :],
                 preferred_element_type=F32)
         + jnp.dot(or_ref[...].reshape(rows, RW_WIDTH).astype(BF16), wo_ref[0, GDN_WIDTH:, :],
                   preferred_element_type=F32)).reshape(bb, tt, d)
    x1 = _layer_norm(alpha * x_ref[...] + g1_ref[...] * y, lnp_ref[0, 0:1, :], lnp_ref[0, 1:2, :], LN_EPS)
    hb = (x1 * (1.0 + sc_ref[...]) + sh_ref[...]).reshape(rows, d).astype(BF16)
    f = jnp.zeros((rows, d), F32)
    for j in range(D_FF // FF_TILE):
        hid = jnp.dot(hb, w1_ref[0, :, j * FF_TILE:(j + 1) * FF_TILE], preferred_element_type=F32)
        hid = jnp.square(jnp.maximum(hid, 0.0)).astype(BF16)
        f = f + jnp.dot(hid, w2_ref[0, j * FF_TILE:(j + 1) * FF_TILE, :], preferred_element_type=F32)
    o_ref[...] = _layer_norm(alpha * x1 + g2_ref[...] * f.reshape(bb, tt, d),
                             lnp_ref[0, 2:3, :], lnp_ref[0, 3:4, :], LN_EPS)


def _outffn_call(og, orr, x, g1, sc2, sh2, g2, wo, lnp, w1, w2, alpha, layer):
    bsz, seq, d = x.shape
    bb, tt = _row_blocks(bsz, seq)
    xmap = lambda b, t: (b, t, 0)
    mmap = lambda b, t: (b, 0, 0)
    mod_spec = pl.BlockSpec((bb, 1, d), mmap)
    return pl.pallas_call(
        functools.partial(_outffn_kernel, alpha=alpha),
        grid=(bsz // bb, seq // tt),
        in_specs=[pl.BlockSpec((bb, tt, GDN_WIDTH), xmap),
                  pl.BlockSpec((bb, tt, RW_WIDTH), xmap),
                  pl.BlockSpec((bb, tt, d), xmap),
                  mod_spec, mod_spec, mod_spec, mod_spec,
                  _layer_spec(wo, layer, single_buffer=True), _layer_spec(lnp, layer),
                  _layer_spec(w1, layer, single_buffer=True), _layer_spec(w2, layer, single_buffer=True)],
        out_specs=pl.BlockSpec((bb, tt, d), xmap),
        out_shape=jax.ShapeDtypeStruct((bsz, seq, d), F32),
        compiler_params=pltpu.CompilerParams(dimension_semantics=("parallel", "parallel"),
                                             vmem_limit_bytes=VMEM_LIMIT),
        name="out_ffn",
    )(og, orr, x, g1, sc2, sh2, g2, wo, lnp, w1, w2)


def _prep_params(w_in, gdn_conv_w, gdn_a_log, gdn_dt_bias, gdn_norm_w, rwkv_mu, rwkv_w0, rwkv_w2,
                 rwkv_a0, rwkv_a2, rwkv_g2, rwkv_kk, rwkv_ka, rwkv_rk, rwkv_ln_w, rwkv_ln_b, w_out,
                 ln1_w, ln1_b, w_ff1, w_ff2, ln2_w, ln2_b):
    depth, d, _ = w_in.shape
    n_gdn = GDN_QKV + GDN_WIDTH + 2 * GDN_HEADS
    wg = jnp.concatenate([w_in[:, :, :n_gdn], jnp.zeros((depth, d, GDN_COLS - n_gdn), F32)], axis=2).astype(BF16)
    wr = w_in[:, :, n_gdn:].astype(BF16)
    gate_params = jnp.zeros((depth, 8, GATE_COLS), F32)
    gate_params = gate_params.at[:, 0, :GDN_HEADS].set(gdn_a_log).at[:, 1, :GDN_HEADS].set(gdn_dt_bias)
    pvec = jnp.stack([rwkv_w0, rwkv_a0, rwkv_kk, rwkv_ka, rwkv_rk, rwkv_ln_w, rwkv_ln_b,
                      jnp.zeros_like(rwkv_w0)], axis=1)
    zeros = jnp.zeros((depth, RW_LORA // 2, RW_WIDTH), F32)
    w2 = jnp.concatenate([rwkv_w2, zeros], axis=1).astype(BF16)
    a2 = jnp.concatenate([zeros, rwkv_a2], axis=1).astype(BF16)
    zero_row = jnp.zeros_like(ln1_w)
    lnp = jnp.stack([ln1_w, ln1_b, ln2_w, ln2_b] + [zero_row] * 4, axis=1)
    return dict(wg=wg, wr=wr, conv_w=gdn_conv_w, gate_params=gate_params,
                norm_w=gdn_norm_w[:, None, :], mu=rwkv_mu[:, None, :], pvec=pvec,
                w2=w2, a2=a2, g2=rwkv_g2.astype(BF16), wo=w_out.astype(BF16), lnp=lnp,
                w1=w_ff1.astype(BF16), w2f=w_ff2.astype(BF16))


def _rwkv_state_to_pairs(s):
    bsz = s.shape[0]
    s = s.reshape(bsz, RW_PAIRS, 2, RW_HD, RW_HD)
    z = jnp.zeros_like(s[:, :, 0])
    top = jnp.concatenate([s[:, :, 0], z], axis=-1)
    bot = jnp.concatenate([z, s[:, :, 1]], axis=-1)
    return jnp.concatenate([top, bot], axis=-2)


def _rwkv_state_from_pairs(sp):
    bsz = sp.shape[0]
    return jnp.stack([sp[:, :, :RW_HD, :RW_HD], sp[:, :, RW_HD:, RW_HD:]], axis=2).reshape(
        bsz, RW_HEADS, RW_HD, RW_HD)


def _trunk(x, mods, st_gdn, st_conv, st_rw, st_shift, p, alpha):
    bsz, seq, d = x.shape
    assert seq >= CONV_W - 1
    outs = ([], [], [], [])
    for l in range(mods.shape[0]):
        sh1, sc1, g1, sh2, sc2, g2 = [mods[l, :, None, i * d:(i + 1) * d] for i in range(6)]
        pg, pr = _inproj_call(x, sc1, sh1, p["wg"], p["wr"], l)
        conv_halo = jnp.pad(st_conv[l], ((0, 0), (HALO - (CONV_W - 1), 0), (0, 0)))
        og, n_gdn = _gdn_call(pg, conv_halo, st_gdn[l], p["conv_w"], p["gate_params"], p["norm_w"], l)
        shift_halo = jnp.pad(st_shift[l][:, None, :], ((0, 0), (HALO - 1, 0), (0, 0)))
        orr, n_rw = _rwkv_call(pr, shift_halo, _rwkv_state_to_pairs(st_rw[l]), p["mu"], p["pvec"],
                               p["w2"], p["a2"], p["g2"], l)
        outs[0].append(n_gdn)
        outs[1].append(pg[:, seq - (CONV_W - 1):, :GDN_QKV])
        outs[2].append(_rwkv_state_from_pairs(n_rw))
        outs[3].append(pr[:, seq - 1, :])
        x = _outffn_call(og, orr, x, g1, sc2, sh2, g2, p["wo"], p["lnp"], p["w1"], p["w2f"], alpha, l)
    return (x,) + tuple(jnp.stack(o) for o in outs)


def kernel(x_prompt, x_sample, c_prompt, c_sample, state_gdn, state_gdn_conv, state_rwkv, state_rwkv_shift, w_ada, b_ada, w_in, gdn_conv_w, gdn_a_log, gdn_dt_bias, gdn_norm_w, rwkv_mu, rwkv_w0, rwkv_w2, rwkv_a0, rwkv_a2, rwkv_g2, rwkv_kk, rwkv_ka, rwkv_rk, rwkv_ln_w, rwkv_ln_b, w_out, ln1_w, ln1_b, w_ff1, w_ff2, ln2_w, ln2_b):
    depth = w_in.shape[0]
    alpha = (2 * depth) ** DEPTH_ALPHA_POW
    bp = x_prompt.shape[0]
    params = _prep_params(w_in, gdn_conv_w, gdn_a_log, gdn_dt_bias, gdn_norm_w, rwkv_mu, rwkv_w0, rwkv_w2,
                          rwkv_a0, rwkv_a2, rwkv_g2, rwkv_kk, rwkv_ka, rwkv_rk, rwkv_ln_w, rwkv_ln_b,
                          w_out, ln1_w, ln1_b, w_ff1, w_ff2, ln2_w, ln2_b)
    mods = _mod_call(jnp.concatenate([c_prompt, c_sample], axis=0), w_ada, b_ada)
    zeros = lambda *shape: jnp.zeros((depth, bp) + shape, F32)
    y_p, p_gdn, p_conv, p_rw, p_shift = _trunk(
        x_prompt, mods[:, :bp],
        zeros(GDN_HEADS, GDN_D, GDN_D), zeros(CONV_W - 1, GDN_QKV),
        zeros(RW_HEADS, RW_HD, RW_HD), zeros(RW_PROJ), params, alpha)
    y_s, s_gdn, s_conv, s_rw, s_shift = _trunk(
        x_sample, mods[:, bp:], state_gdn, state_gdn_conv, state_rwkv, state_rwkv_shift, params, alpha)
    return (y_p, y_s, p_gdn, p_conv, p_rw, p_shift, s_gdn, s_conv, s_rw, s_shift)
```

```python
import functools

import jax
import jax.numpy as jnp
from jax import lax
from jax.experimental import pallas as pl
from jax.experimental.pallas import tpu as pltpu

F32 = jnp.float32
BF16 = jnp.bfloat16

D_MODEL = 1024
CHUNK = 64
GDN_HEADS = 4
GDN_D = 128
GDN_WIDTH = GDN_HEADS * GDN_D
GDN_QKV = 3 * GDN_WIDTH
CONV_W = 4
GATE_COLS = 128
GDN_COLS = GDN_QKV + GDN_WIDTH + GATE_COLS
RW_HEADS = 8
RW_HD = 64
RW_WIDTH = RW_HEADS * RW_HD
RW_PAIRS = RW_HEADS // 2
RW_LORA = 128
RW_PROJ = 3 * RW_WIDTH + 2 * RW_LORA
D_FF = 4 * D_MODEL
DEPTH_ALPHA_POW = 0.25
LN_EPS = 1e-5
GN_EPS = 64e-5
NORM_EPS = 1e-6
HALO = 8
ROW_TILE = 512
TIME_TILE = 512
FF_TILE = 1024
GDN_GROUP = 8
RW_GROUP = 8
SEQ_BLOCK = 2
INPROJ_QKV_GROUPS = tuple((i, i + 512) for i in range(0, 1536, 512))
INPROJ_RW_GROUPS = tuple((i, i + 512) for i in range(0, 1536, 512)) + ((1536, 1792),)
INPROJ_ROWS = 128
VMEM_LIMIT = 56 * 1024 * 1024


def _bdot(a, b):
    return jnp.dot(a.astype(BF16), b.astype(BF16), preferred_element_type=F32)


def _bdot_nt(a, b):
    return lax.dot_general(a.astype(BF16), b.astype(BF16), (((1,), (1,)), ((), ())),
                           preferred_element_type=F32)


def _bdot_tn(a, b):
    return lax.dot_general(a.astype(BF16), b.astype(BF16), (((0,), (0,)), ((), ())),
                           preferred_element_type=F32)


def _split2(x):
    hi = x.astype(BF16)
    return hi, (x - hi.astype(F32)).astype(BF16)


def _dot3(a, b):
    ah, al = _split2(a)
    bh, bl = _split2(b)
    return (jnp.dot(ah, bh, preferred_element_type=F32) + jnp.dot(ah, bl, preferred_element_type=F32)
            + jnp.dot(al, bh, preferred_element_type=F32))


def _cumsum_rows(ltri_bf16, x):
    hi = x.astype(BF16)
    rest = x - hi.astype(F32)
    mid = rest.astype(BF16)
    lo = (rest - mid.astype(F32)).astype(BF16)
    return (jnp.dot(ltri_bf16, hi, preferred_element_type=F32)
            + jnp.dot(ltri_bf16, mid, preferred_element_type=F32)
            + jnp.dot(ltri_bf16, lo, preferred_element_type=F32))


def _sigmoid(x):
    return 0.5 * jnp.tanh(0.5 * x) + 0.5


def _silu(x):
    return x * _sigmoid(x)


def _softplus(x):
    return jnp.maximum(x, 0.0) + jnp.log(1.0 + jnp.exp(-jnp.abs(x)))


def _pair_blockdiag(x, first):
    zero = jnp.zeros_like(x)
    return jnp.concatenate([jnp.where(first, x, zero), jnp.where(first, zero, x)], axis=0)


def _pair_unit_lower_inverse(a_list, first, eye, refine):
    n = eye.shape[0]
    xs = [eye - a for a in a_list]
    ps = list(a_list)
    bds = [_pair_blockdiag(p.astype(BF16), first) for p in ps]
    terms = 2
    while terms < n:
        ps = [jnp.dot(p.astype(BF16), bd, preferred_element_type=F32) for p, bd in zip(ps, bds)]
        bds = [_pair_blockdiag(p.astype(BF16), first) for p in ps]
        xs = [x + jnp.dot(x.astype(BF16), bd, preferred_element_type=F32) for x, bd in zip(xs, bds)]
        terms *= 2
    if not refine:
        return xs
    es = [_dot3(eye + a, _pair_blockdiag(x, first)) - eye for a, x in zip(a_list, xs)]
    return [x - _bdot(x, _pair_blockdiag(e, first)) for x, e in zip(xs, es)]


def _layer_norm(t, w, b, eps):
    mu = jnp.mean(t, axis=-1, keepdims=True)
    d = t - mu
    var = jnp.mean(d * d, axis=-1, keepdims=True)
    return d * lax.rsqrt(var + eps) * w + b


def _mod_kernel(c_ref, w_ref, b_ref, o_ref):
    o_ref[0] = _bdot(_silu(c_ref[...]), w_ref[0]) + b_ref[0]


def _mod_call(c_all, w_ada, b_ada):
    depth, d, n = w_ada.shape
    rows = c_all.shape[0]
    tn = 1536
    return pl.pallas_call(
        _mod_kernel,
        grid=(depth, n // tn),
        in_specs=[pl.BlockSpec((rows, d), lambda l, j: (0, 0)),
                  pl.BlockSpec((1, d, tn), lambda l, j: (l, 0, j)),
                  pl.BlockSpec((1, 1, tn), lambda l, j: (l, 0, j))],
        out_specs=pl.BlockSpec((1, rows, tn), lambda l, j: (l, 0, j)),
        out_shape=jax.ShapeDtypeStruct((depth, rows, n), F32),
        compiler_params=pltpu.CompilerParams(dimension_semantics=("parallel", "parallel"),
                                             vmem_limit_bytes=VMEM_LIMIT),
        name="adaln_mod",
    )(c_all, w_ada, b_ada.reshape(depth, 1, n))


def _inproj_kernel(x_ref, sc_ref, sh_ref, wg_ref, wr_ref, cw_ref, mu_ref, cst_ref, sst_ref,
                   og_ref, or_ref, craw_ref, sraw_ref, halo_g, halo_r):
    bb, tt, d = x_ref.shape

    @pl.when(pl.program_id(1) == 0)
    def _():
        halo_g[...] = cst_ref[...]
        halo_r[...] = sst_ref[...]

    h = x_ref[...] * (1.0 + sc_ref[...]) + sh_ref[...]
    hb = h.reshape(bb * tt, d).astype(BF16)
    conv_w = cw_ref[0]
    shift_mix = mu_ref[0]

    def conv_silu(raw, halo, w):
        blk = jnp.concatenate([halo, raw], axis=0)
        prev = pltpu.roll(blk, 1, axis=0)
        acc = (blk * w[3:4, :] + prev * w[2:3, :]
               + pltpu.roll(blk * w[1:2, :] + prev * w[0:1, :], 2, axis=0))
        return _silu(acc[HALO:, :])

    def token_shift(raw, halo, m):
        blk = jnp.concatenate([halo, raw], axis=0)
        return (blk + (pltpu.roll(blk, 1, axis=0) - blk) * m)[HALO:, :]

    rb_rows = min(bb * tt, INPROJ_ROWS)
    seg = min(tt, rb_rows)
    for r0 in range(0, bb * tt, rb_rows):
        hrows = hb[r0:r0 + rb_rows]
        pieces = [(s0,) + divmod(r0 + s0, tt) for s0 in range(0, rb_rows, seg)]
        for lo, hi in INPROJ_QKV_GROUPS:
            raw = jnp.dot(hrows, wg_ref[0, :, lo:hi], preferred_element_type=F32)
            for s0, b, t in pieces:
                og_ref[b, t:t + seg, lo:hi] = conv_silu(raw[s0:s0 + seg], halo_g[b, :, lo:hi], conv_w[:, lo:hi])
                halo_g[b, :, lo:hi] = raw[s0 + seg - HALO:s0 + seg, :]
        raw = jnp.dot(hrows, wg_ref[0, :, GDN_QKV:], preferred_element_type=F32)
        for s0, b, t in pieces:
            og_ref[b, t:t + seg, GDN_QKV:] = raw[s0:s0 + seg]
        for lo, hi in INPROJ_RW_GROUPS:
            raw = jnp.dot(hrows, wr_ref[0, :, lo:hi], preferred_element_type=F32)
            for s0, b, t in pieces:
                or_ref[b, t:t + seg, lo:hi] = token_shift(raw[s0:s0 + seg], halo_r[b, :, lo:hi], shift_mix[:, lo:hi])
                halo_r[b, :, lo:hi] = raw[s0 + seg - HALO:s0 + seg, :]
    craw_ref[...] = halo_g[...]
    sraw_ref[...] = halo_r[...]


def _row_blocks(bsz, seq):
    tt = min(seq, ROW_TILE)
    bb = max(1, min(bsz, ROW_TILE // tt))
    assert seq % tt == 0 and bsz % bb == 0 and tt % 8 == 0
    return bb, tt


def _layer_spec(arr, layer, single_buffer=False):
    kwargs = dict(pipeline_mode=pl.Buffered(1)) if single_buffer else {}
    return pl.BlockSpec((1,) + arr.shape[1:], lambda b, t: (layer,) + (0,) * (arr.ndim - 1), **kwargs)


def _inproj_call(x, sc, sh, wg, wr, conv_w, mu, conv_halo, shift_halo, layer):
    bsz, seq, d = x.shape
    bb, tt = _row_blocks(bsz, seq)
    assert tt >= HALO
    xmap = lambda b, t: (b, t, 0)
    mmap = lambda b, t: (b, 0, 0)
    return pl.pallas_call(
        _inproj_kernel,
        grid=(bsz // bb, seq // tt),
        in_specs=[pl.BlockSpec((bb, tt, d), xmap),
                  pl.BlockSpec((bb, 1, d), mmap),
                  pl.BlockSpec((bb, 1, d), mmap),
                  _layer_spec(wg, layer, single_buffer=True),
                  _layer_spec(wr, layer, single_buffer=True),
                  _layer_spec(conv_w, layer), _layer_spec(mu, layer),
                  pl.BlockSpec((bb, HALO, GDN_QKV), mmap),
                  pl.BlockSpec((bb, HALO, RW_PROJ), mmap)],
        out_specs=[pl.BlockSpec((bb, tt, GDN_COLS), xmap),
                   pl.BlockSpec((bb, tt, RW_PROJ), xmap),
                   pl.BlockSpec((bb, HALO, GDN_QKV), mmap),
                   pl.BlockSpec((bb, HALO, RW_PROJ), mmap)],
        out_shape=[jax.ShapeDtypeStruct((bsz, seq, GDN_COLS), F32),
                   jax.ShapeDtypeStruct((bsz, seq, RW_PROJ), F32),
                   jax.ShapeDtypeStruct((bsz, HALO, GDN_QKV), F32),
                   jax.ShapeDtypeStruct((bsz, HALO, RW_PROJ), F32)],
        scratch_shapes=[pltpu.VMEM((bb, HALO, GDN_QKV), F32),
                        pltpu.VMEM((bb, HALO, RW_PROJ), F32)],
        compiler_params=pltpu.CompilerParams(dimension_semantics=("parallel", "arbitrary"),
                                             vmem_limit_bytes=VMEM_LIMIT),
        name="in_proj",
    )(x, sc, sh, wg, wr, conv_w, mu, conv_halo, shift_halo)


def _gdn_kernel(pg_ref, s0_ref, gp_ref, nw_ref, og_ref, sout_ref,
                s_scr, qd_scr, ov_scr, m_scr, u_scr, dl_scr, *, tt, group, bb):
    c = CHUNK
    gc = group * c
    n_chunks = tt // c
    n_groups = n_chunks // group
    d = GDN_D
    t_idx = pl.program_id(1)

    @pl.when(t_idx == 0)
    def _():
        for b in range(bb):
            s_scr[b * GDN_HEADS:(b + 1) * GDN_HEADS] = s0_ref[b]

    row = lax.broadcasted_iota(jnp.int32, (c, 2 * c), 0)
    lane = lax.broadcasted_iota(jnp.int32, (c, 2 * c), 1)
    first = lane < c
    col = lane & (c - 1)
    causal = col <= row
    strict = col < row
    eye = jnp.where(col == row, 1.0, 0.0).astype(F32)
    grow = lax.broadcasted_iota(jnp.int32, (gc, gc), 0)
    gcol = lax.broadcasted_iota(jnp.int32, (gc, gc), 1)
    ltri = jnp.where(((grow & -c) == (gcol & -c)) & (gcol <= grow), 1.0, 0.0).astype(BF16)
    ones_d = jnp.ones((d, d), BF16)
    zero_2d = jnp.zeros((c, 2 * d), F32)
    neg_a = -jnp.exp(gp_ref[0, 0:1, :])
    dt_bias = gp_ref[0, 1:2, :]
    nw = nw_ref[0]
    heads = range(GDN_HEADS)

    def per_chunk_row(x, r):
        return jnp.concatenate([jnp.broadcast_to(x[j * c + r:j * c + r + 1, :], (c, x.shape[1]))
                                for j in range(group)], axis=0)

    def pass1(step, carry):
        b = step // n_groups
        gi = step - b * n_groups
        r0 = pl.multiple_of(step * gc, gc)
        t0 = pl.multiple_of(gi * gc, gc)
        qkv = pg_ref[b, pl.ds(t0, gc), 0:GDN_QKV]
        gates = pg_ref[b, pl.ds(t0, gc), GDN_QKV + GDN_WIDTH:GDN_COLS]
        log_a = neg_a * _softplus(gates + dt_bias)
        beta_all = _sigmoid(gates)
        g_all = _cumsum_rows(ltri, log_a)
        g_rows = g_all.T
        eg_all = jnp.exp(g_all)
        g_last = per_chunk_row(g_all, c - 1)
        dec_all = jnp.exp(g_last - g_all)
        for j in range(group):
            dl_scr[step * group + j] = jnp.broadcast_to(
                jnp.exp(g_all[(j + 1) * c - 1:(j + 1) * c, :]), (8, GATE_COLS))
        q = [qkv[:, h * d:(h + 1) * d] for h in heads]
        k = [qkv[:, GDN_WIDTH + h * d:GDN_WIDTH + (h + 1) * d] for h in heads]
        v = [qkv[:, 2 * GDN_WIDTH + h * d:2 * GDN_WIDTH + (h + 1) * d] for h in heads]
        ss = _bdot(jnp.concatenate([t * t for t in q + k], axis=0), ones_d)
        qn = [q[h] * lax.rsqrt(ss[h * gc:(h + 1) * gc] + NORM_EPS) * (d ** -0.5) for h in heads]
        kn = [k[h] * lax.rsqrt(ss[(GDN_HEADS + h) * gc:(GDN_HEADS + h + 1) * gc] + NORM_EPS) for h in heads]
        items = []
        for j in range(group):
            rs = slice(j * c, (j + 1) * c)
            for p in range(GDN_HEADS // 2):
                pair = (2 * p, 2 * p + 1)
                items.append(dict(
                    j=j, pair=pair, rs=rs,
                    ks2=jnp.concatenate([kn[h][rs] for h in pair], axis=0).astype(BF16),
                    qs2=jnp.concatenate([qn[h][rs] for h in pair], axis=0).astype(BF16)))
        kk2s = [_bdot_nt(it["ks2"], it["ks2"]) for it in items]
        qk2s = [_bdot_nt(it["qs2"], it["ks2"]) for it in items]
        for it, kk2, qk2 in zip(items, kk2s, qk2s):
            (h1, h2), rs = it["pair"], it["rs"]
            kk_cat = jnp.where(first, kk2[0:c], kk2[c:2 * c])
            qk_cat = jnp.where(first, qk2[0:c], qk2[c:2 * c])
            g_col_cat = jnp.where(first, g_all[rs, h1:h1 + 1], g_all[rs, h2:h2 + 1])
            g_row_cat = jnp.concatenate([g_rows[h1:h1 + 1, rs], g_rows[h2:h2 + 1, rs]], axis=1)
            beta_cat = jnp.where(first, beta_all[rs, GDN_HEADS + h1:GDN_HEADS + h1 + 1],
                                 beta_all[rs, GDN_HEADS + h2:GDN_HEADS + h2 + 1])
            gam = jnp.where(causal, jnp.exp(jnp.where(causal, g_col_cat - g_row_cat, 0.0)), 0.0)
            it["a_cat"] = jnp.where(strict, beta_cat * kk_cat * gam, 0.0)
            it["qk_cat"] = jnp.where(causal, qk_cat * gam, 0.0)
            rhs = []
            for h in (h1, h2):
                beta = beta_all[rs, GDN_HEADS + h:GDN_HEADS + h + 1]
                rhs.append(jnp.concatenate([v[h][rs] * beta, kn[h][rs] * (beta * eg_all[rs, h:h + 1])], axis=-1))
            it["rhs_bd"] = jnp.concatenate([jnp.concatenate([rhs[0], zero_2d], axis=-1),
                                            jnp.concatenate([zero_2d, rhs[1]], axis=-1)], axis=0)
        t_invs = _pair_unit_lower_inverse([it["a_cat"] for it in items], first, eye, refine=True)
        sols = [_bdot(t_inv, it["rhs_bd"]) for it, t_inv in zip(items, t_invs)]
        corrs = [_bdot(it["qk_cat"], jnp.concatenate(
            [jnp.concatenate([sol[:, d:2 * d], sol[:, 0:d], zero_2d], axis=-1),
             jnp.concatenate([zero_2d, sol[:, 3 * d:4 * d], sol[:, 2 * d:3 * d]], axis=-1)], axis=0))
            for it, sol in zip(items, sols)]
        mus = [[_bdot_tn(kn[h][it["rs"]] * dec_all[it["rs"], h:h + 1],
                         jnp.concatenate([sol[:, (2 * i + 1) * d:(2 * i + 2) * d],
                                          sol[:, 2 * i * d:(2 * i + 1) * d]], axis=-1))
                for i, h in enumerate(it["pair"])] for it, sol in zip(items, sols)]
        for it, corr, mu_pair in zip(items, corrs, mus):
            ci = step * group + it["j"]
            rs = it["rs"]
            for i, h in enumerate(it["pair"]):
                qd_scr[ci, h] = (qn[h][rs] * eg_all[rs, h:h + 1] - corr[:, 2 * i * d:(2 * i + 1) * d]).astype(BF16)
                ov_scr[pl.ds(r0 + it["j"] * c, c), h * d:(h + 1) * d] = corr[:, (2 * i + 1) * d:(2 * i + 2) * d]
                m_scr[ci, h] = mu_pair[i][:, 0:d].astype(BF16)
                u_scr[ci, h] = mu_pair[i][:, d:2 * d]
        return carry

    lax.fori_loop(0, bb * n_groups, pass1, 0)

    def pass2(ci, carry):
        cis = [b * n_chunks + ci for b in range(bb)]
        rows = [pl.ds(pl.multiple_of(b * tt + ci * c, c), c) for b in range(bb)]
        dls = [dl_scr[cb][0:1, :] for cb in cis]
        s_old = [[s_scr[b * GDN_HEADS + h] for h in heads] for b in range(bb)]
        s_bf = [[s.astype(BF16) for s in sb] for sb in s_old]
        o = [[jnp.dot(qd_scr[cis[b], h], s_bf[b][h], preferred_element_type=F32) for h in heads]
             for b in range(bb)]
        ms = [[jnp.dot(m_scr[cis[b], h], s_bf[b][h], preferred_element_type=F32) for h in heads]
              for b in range(bb)]
        for b in range(bb):
            ov_scr[rows[b], :] = jnp.concatenate(o[b], axis=-1) + ov_scr[rows[b], :]
        s_scr[...] = jnp.stack([s_old[b][h] * dls[b][:, h:h + 1] - ms[b][h] + u_scr[cis[b], h]
                                for b in range(bb) for h in heads])
        return carry

    lax.fori_loop(0, n_chunks, pass2, 0)

    def pass3(step, carry):
        b = step // n_groups
        t0 = pl.multiple_of((step - b * n_groups) * gc, gc)
        r0 = pl.multiple_of(step * gc, gc)
        o = ov_scr[pl.ds(r0, gc), :]
        o_all = jnp.concatenate([o[:, h * d:(h + 1) * d] for h in heads], axis=0)
        ms = _bdot(o_all * o_all, ones_d) * (1.0 / d)
        o_all = o_all * lax.rsqrt(ms + NORM_EPS) * nw
        z = pg_ref[b, pl.ds(t0, gc), GDN_QKV:GDN_QKV + GDN_WIDTH]
        og_ref[b, pl.ds(t0, gc), :] = jnp.concatenate(
            [o_all[h * gc:(h + 1) * gc] for h in heads], axis=-1) * _silu(z)
        return carry

    lax.fori_loop(0, bb * n_groups, pass3, 0)

    @pl.when(t_idx == pl.num_programs(1) - 1)
    def _():
        for b in range(bb):
            sout_ref[b] = s_scr[b * GDN_HEADS:(b + 1) * GDN_HEADS]


def _gdn_call(pg, s0, gate_params, norm_w, layer):
    bsz, seq, _ = pg.shape
    tt = min(seq, TIME_TILE)
    n_chunks = tt // CHUNK
    group = GDN_GROUP if n_chunks % GDN_GROUP == 0 else 1
    bb = SEQ_BLOCK if bsz % SEQ_BLOCK == 0 else 1
    assert seq % tt == 0 and tt % CHUNK == 0
    smap = lambda b, t: (b, 0, 0, 0)
    return pl.pallas_call(
        functools.partial(_gdn_kernel, tt=tt, group=group, bb=bb),
        grid=(bsz // bb, seq // tt),
        in_specs=[pl.BlockSpec((bb, tt, GDN_COLS), lambda b, t: (b, t, 0)),
                  pl.BlockSpec((bb, GDN_HEADS, GDN_D, GDN_D), smap),
                  _layer_spec(gate_params, layer), _layer_spec(norm_w, layer)],
        out_specs=[pl.BlockSpec((bb, tt, GDN_WIDTH), lambda b, t: (b, t, 0)),
                   pl.BlockSpec((bb, GDN_HEADS, GDN_D, GDN_D), smap)],
        out_shape=[jax.ShapeDtypeStruct((bsz, seq, GDN_WIDTH), F32),
                   jax.ShapeDtypeStruct((bsz, GDN_HEADS, GDN_D, GDN_D), F32)],
        scratch_shapes=[pltpu.VMEM((bb * GDN_HEADS, GDN_D, GDN_D), F32),
                        pltpu.VMEM((bb * n_chunks, GDN_HEADS, CHUNK, GDN_D), BF16),
                        pltpu.VMEM((bb * tt, GDN_WIDTH), F32),
                        pltpu.VMEM((bb * n_chunks, GDN_HEADS, GDN_D, GDN_D), BF16),
                        pltpu.VMEM((bb * n_chunks, GDN_HEADS, GDN_D, GDN_D), F32),
                        pltpu.VMEM((bb * n_chunks, 8, GATE_COLS), F32)],
        compiler_params=pltpu.CompilerParams(dimension_semantics=("parallel", "arbitrary"),
                                             vmem_limit_bytes=VMEM_LIMIT),
        name="gated_deltanet",
    )(pg, s0, gate_params, norm_w)


def _rwkv_kernel(pr_ref, s0_ref, pv_ref, w2_ref, a2_ref, g2_ref,
                 or_ref, sout_ref, s_scr, rq_scr, ov_scr, m_scr, u_scr, dec_scr, gate_scr, bonus_scr,
                 *, tt, group, bb):
    c = CHUNK
    gc = group * c
    lanes = 2 * RW_HD
    n_chunks = tt // c
    n_groups = n_chunks // group
    t_idx = pl.program_id(1)

    @pl.when(t_idx == 0)
    def _():
        for b in range(bb):
            s_scr[b * RW_PAIRS:(b + 1) * RW_PAIRS] = s0_ref[b]

    grow = lax.broadcasted_iota(jnp.int32, (gc, gc), 0)
    gcol = lax.broadcasted_iota(jnp.int32, (gc, gc), 1)
    same_chunk = (grow & -c) == (gcol & -c)
    ltri = jnp.where(same_chunk & (gcol <= grow), 1.0, 0.0).astype(BF16)
    prow = lax.broadcasted_iota(jnp.int32, (c, lanes), 0)
    plane = lax.broadcasted_iota(jnp.int32, (c, lanes), 1)
    first = plane < RW_HD
    pcol = plane & (RW_HD - 1)
    tri_strict = pcol < prow
    tri_incl = pcol <= prow
    eye = jnp.where(pcol == prow, 1.0, 0.0).astype(F32)
    srow = lax.broadcasted_iota(jnp.int32, (lanes, lanes), 0)
    scol = lax.broadcasted_iota(jnp.int32, (lanes, lanes), 1)
    same_head = (srow < RW_HD) == (scol < RW_HD)
    seg_ones = jnp.where(same_head, 1.0, 0.0).astype(BF16)
    w0 = pv_ref[0, 0:1, :]
    a0 = pv_ref[0, 1:2, :]
    kk_p = pv_ref[0, 2:3, :]
    ka_p = pv_ref[0, 3:4, :]
    rk_p = pv_ref[0, 4:5, :]
    ln_w = pv_ref[0, 5:6, :]
    ln_b = pv_ref[0, 6:7, :]
    pairs = range(RW_PAIRS)
    sls = [slice(p * lanes, (p + 1) * lanes) for p in pairs]

    def per_chunk_row(x, row):
        return jnp.concatenate([jnp.broadcast_to(x[j * c + row:j * c + row + 1, :], (c, x.shape[1]))
                                for j in range(group)], axis=0)

    def pass1(step, carry):
        b = step // n_groups
        gi = step - b * n_groups
        r0 = pl.multiple_of(step * gc, gc)
        xs = pr_ref[b, pl.ds(pl.multiple_of(gi * gc, gc), gc), :]
        r = xs[:, 0:RW_WIDTH]
        kr = xs[:, RW_WIDTH:2 * RW_WIDTH]
        vr = xs[:, 2 * RW_WIDTH:3 * RW_WIDTH]
        lora_in = xs[:, 3 * RW_WIDTH:3 * RW_WIDTH + RW_LORA]
        xg = xs[:, 3 * RW_WIDTH + RW_LORA:RW_PROJ]
        w_log = -_softplus(-(w0 + _bdot(jnp.tanh(lora_in), w2_ref[0]))) - 0.5
        log_w = -jnp.exp(w_log)
        a = _sigmoid(a0 + _bdot(lora_in, a2_ref[0]))
        gate_scr[pl.ds(r0, gc), :] = _bdot(_sigmoid(xg), g2_ref[0])
        kk_raw = kr * kk_p
        k2 = kr * (1.0 + (a - 1.0) * ka_p)
        rk = r * k2 * rk_p
        g_cum = _cumsum_rows(ltri, log_w)
        mid_rows = [g_cum[j * c + c // 2 - 1:j * c + c // 2, :] for j in range(group)]
        end_rows = [g_cum[(j + 1) * c - 1:(j + 1) * c, :] for j in range(group)]
        g_mid = per_chunk_row(g_cum, c // 2 - 1)
        e_left = jnp.exp(g_cum - g_mid)
        e_left_prev = jnp.exp(g_cum - log_w - g_mid)
        e_right = jnp.exp(g_mid - g_cum)
        e_mid = [jnp.exp(m) for m in mid_rows]
        e_end_mid = [jnp.exp(e - m) for e, m in zip(end_rows, mid_rows)]
        for j in range(group):
            dec_scr[step * group + j] = jnp.broadcast_to(jnp.exp(end_rows[j]), (8, RW_WIDTH))
        seg = _bdot(jnp.concatenate([kk_raw[:, sl] * kk_raw[:, sl] for sl in sls]
                                    + [rk[:, sl] for sl in sls], axis=0), seg_ones)
        bonus_scr[pl.ds(r0, gc), :] = jnp.concatenate(
            [seg[(RW_PAIRS + p) * gc:(RW_PAIRS + p + 1) * gc] for p in pairs], axis=-1) * vr
        items = []
        for j in range(group):
            rs = slice(j * c, (j + 1) * c)
            for p, sl in enumerate(sls):
                kappa = kk_raw[rs, sl] * lax.rsqrt(seg[p * gc + j * c:p * gc + (j + 1) * c] + NORM_EPS)
                r_l = r[rs, sl] * e_left[rs, sl]
                kap_l = kappa * e_left_prev[rs, sl]
                k_r = k2[rs, sl] * e_right[rs, sl]
                b_r = kappa * a[rs, sl] * e_right[rs, sl]
                items.append(dict(
                    j=j, p=p, v=vr[rs, sl], kg=kap_l * e_mid[j][:, sl], rg=r_l * e_mid[j][:, sl],
                    k_d=k_r * e_end_mid[j][:, sl], b_d=b_r * e_end_mid[j][:, sl],
                    lhs=jnp.concatenate([kap_l, r_l], axis=0),
                    rhs=jnp.concatenate([jnp.where(first, b_r, 0.0), jnp.where(first, k_r, 0.0),
                                         jnp.where(first, 0.0, k_r), jnp.where(first, 0.0, b_r)], axis=0)))
        prods = [_bdot_nt(it["lhs"], it["rhs"]) for it in items]
        for it, res in zip(items, prods):
            kap_1, kap_2 = res[0:c, 0:lanes], res[0:c, lanes:2 * lanes]
            r_1, r_2 = res[c:2 * c, 0:lanes], res[c:2 * c, lanes:2 * lanes]
            it["a_cat"] = jnp.where(tri_strict, jnp.where(first, kap_1, kap_2), 0.0)
            kb_cat = jnp.where(tri_strict, jnp.where(first, kap_2, kap_1), 0.0)
            qk_cat = jnp.where(tri_incl, jnp.where(first, r_2, r_1), 0.0)
            it["kq"] = jnp.concatenate([kb_cat, qk_cat], axis=0)
            it["qb_cat"] = jnp.where(tri_incl, jnp.where(first, r_1, r_2), 0.0)
            v = it["v"]
            it["v_swap"] = jnp.concatenate([jnp.where(first, 0.0, v), jnp.where(first, v, 0.0)], axis=0)
        intras = [_bdot(it["kq"], it["v_swap"]) for it in items]
        t_invs = _pair_unit_lower_inverse([it["a_cat"] for it in items], first, eye, refine=False)
        sols = [_bdot(t_inv, jnp.concatenate([_pair_blockdiag(it["kg"], first),
                                              _pair_blockdiag(intra[0:c], first)], axis=-1))
                for it, t_inv, intra in zip(items, t_invs, intras)]
        corrs = [_bdot(it["qb_cat"], jnp.concatenate([_pair_blockdiag(sol[:, 0:lanes], first),
                                                      _pair_blockdiag(sol[:, lanes:2 * lanes], first)], axis=-1))
                 for it, sol in zip(items, sols)]
        us = [_bdot_tn(jnp.concatenate([it["v"], -sol[:, lanes:2 * lanes]], axis=0),
                       jnp.concatenate([it["k_d"], it["b_d"]], axis=0)) for it, sol in zip(items, sols)]
        ms = [_bdot_tn(sol[:, 0:lanes], it["b_d"]) for it, sol in zip(items, sols)]
        for item, intra, corr, u, m in zip(items, intras, corrs, us, ms):
            ci = step * group + item["j"]
            p = item["p"]
            rq_scr[ci, p] = (item["rg"] - corr[:, 0:lanes]).astype(BF16)
            ov_scr[pl.ds(r0 + item["j"] * c, c), sls[p]] = intra[c:2 * c] - corr[:, lanes:2 * lanes]
            u_scr[ci, p] = jnp.where(same_head, u, 0.0)
            m_scr[ci, p] = jnp.where(same_head, m, 0.0).astype(BF16)
        return carry

    lax.fori_loop(0, bb * n_groups, pass1, 0)

    def pass2(ci, carry):
        cis = [b * n_chunks + ci for b in range(bb)]
        rows = [pl.ds(pl.multiple_of(b * tt + ci * c, c), c) for b in range(bb)]
        decs = [dec_scr[cb][0:1, :] for cb in cis]
        s_old = [[s_scr[b * RW_PAIRS + p] for p in pairs] for b in range(bb)]
        s_bf = [[s.astype(BF16) for s in sb] for sb in s_old]
        o = [[_bdot_nt(rq_scr[cis[b], p], s_bf[b][p]) for p in pairs] for b in range(bb)]
        sm = [[jnp.dot(s_bf[b][p], m_scr[cis[b], p], preferred_element_type=F32) for p in pairs]
              for b in range(bb)]
        for b in range(bb):
            ov_scr[rows[b], :] = jnp.concatenate(o[b], axis=-1) + ov_scr[rows[b], :]
        s_scr[...] = jnp.stack([s_old[b][p] * decs[b][:, sls[p]] - sm[b][p] + u_scr[cis[b], p]
                                for b in range(bb) for p in pairs])
        return carry

    lax.fori_loop(0, n_chunks, pass2, 0)

    def pass3(step, carry):
        b = step // n_groups
        gi = step - b * n_groups
        r0 = pl.multiple_of(step * gc, gc)
        o = ov_scr[pl.ds(r0, gc), :]
        o_all = jnp.concatenate([o[:, sl] for sl in sls], axis=0)
        mean = _bdot(o_all, seg_ones) * (1.0 / RW_HD)
        dev = o_all - mean
        var = _bdot(dev * dev, seg_ones) * (1.0 / RW_HD)
        o_n = dev * lax.rsqrt(var + GN_EPS)
        o_n = jnp.concatenate([o_n[p * gc:(p + 1) * gc] for p in pairs], axis=-1) * ln_w + ln_b
        or_ref[b, pl.ds(pl.multiple_of(gi * gc, gc), gc), :] = (
            (o_n + bonus_scr[pl.ds(r0, gc), :]) * gate_scr[pl.ds(r0, gc), :])
        return carry

    lax.fori_loop(0, bb * n_groups, pass3, 0)

    @pl.when(t_idx == pl.num_programs(1) - 1)
    def _():
        for b in range(bb):
            sout_ref[b] = s_scr[b * RW_PAIRS:(b + 1) * RW_PAIRS]


def _rwkv_call(pr, s0, pvec, w2, a2, g2, layer):
    bsz, seq, _ = pr.shape
    tt = min(seq, TIME_TILE)
    n_chunks = tt // CHUNK
    group = RW_GROUP if n_chunks % RW_GROUP == 0 else 1
    bb = SEQ_BLOCK if bsz % SEQ_BLOCK == 0 else 1
    assert seq % tt == 0 and tt % CHUNK == 0
    lanes = 2 * RW_HD
    smap = lambda b, t: (b, 0, 0, 0)
    return pl.pallas_call(
        functools.partial(_rwkv_kernel, tt=tt, group=group, bb=bb),
        grid=(bsz // bb, seq // tt),
        in_specs=[pl.BlockSpec((bb, tt, RW_PROJ), lambda b, t: (b, t, 0)),
                  pl.BlockSpec((bb, RW_PAIRS, lanes, lanes), smap),
                  _layer_spec(pvec, layer),
                  _layer_spec(w2, layer), _layer_spec(a2, layer), _layer_spec(g2, layer)],
        out_specs=[pl.BlockSpec((bb, tt, RW_WIDTH), lambda b, t: (b, t, 0)),
                   pl.BlockSpec((bb, RW_PAIRS, lanes, lanes), smap)],
        out_shape=[jax.ShapeDtypeStruct((bsz, seq, RW_WIDTH), F32),
                   jax.ShapeDtypeStruct((bsz, RW_PAIRS, lanes, lanes), F32)],
        scratch_shapes=[pltpu.VMEM((bb * RW_PAIRS, lanes, lanes), F32),
                        pltpu.VMEM((bb * n_chunks, RW_PAIRS, CHUNK, lanes), BF16),
                        pltpu.VMEM((bb * tt, RW_WIDTH), F32),
                        pltpu.VMEM((bb * n_chunks, RW_PAIRS, lanes, lanes), BF16),
                        pltpu.VMEM((bb * n_chunks, RW_PAIRS, lanes, lanes), F32),
                        pltpu.VMEM((bb * n_chunks, 8, RW_WIDTH), F32),
                        pltpu.VMEM((bb * tt, RW_WIDTH), F32),
                        pltpu.VMEM((bb * tt, RW_WIDTH), F32)],
        compiler_params=pltpu.CompilerParams(dimension_semantics=("parallel", "arbitrary"),
                                             vmem_limit_bytes=VMEM_LIMIT),
        name="rwkv7",
    )(pr, s0, pvec, w2, a2, g2)


def _outffn_kernel(og_ref, or_ref, x_ref, g1_ref, sc_ref, sh_ref, g2_ref, wo_ref, lnp_ref,
                   w1_ref, w2_ref, o_ref, *, alpha):
    bb, tt, d = x_ref.shape
    rows = bb * tt
    y = (jnp.dot(og_ref[...].reshape(rows, GDN_WIDTH).astype(BF16), wo_ref[0, 0:GDN_WIDTH, :],
                 preferred_element_type=F32)
         + jnp.dot(or_ref[...].reshape(rows, RW_WIDTH).astype(BF16), wo_ref[0, GDN_WIDTH:, :],
                   preferred_element_type=F32)).reshape(bb, tt, d)
    x1 = _layer_norm(alpha * x_ref[...] + g1_ref[...] * y, lnp_ref[0, 0:1, :], lnp_ref[0, 1:2, :], LN_EPS)
    hb = (x1 * (1.0 + sc_ref[...]) + sh_ref[...]).reshape(rows, d).astype(BF16)
    f = jnp.zeros((rows, d), F32)
    for j in range(D_FF // FF_TILE):
        hid = jnp.dot(hb, w1_ref[0, :, j * FF_TILE:(j + 1) * FF_TILE], preferred_element_type=F32)
        hid = jnp.square(jnp.maximum(hid, 0.0)).astype(BF16)
        f = f + jnp.dot(hid, w2_ref[0, j * FF_TILE:(j + 1) * FF_TILE, :], preferred_element_type=F32)
    o_ref[...] = _layer_norm(alpha * x1 + g2_ref[...] * f.reshape(bb, tt, d),
                             lnp_ref[0, 2:3, :], lnp_ref[0, 3:4, :], LN_EPS)


def _outffn_call(og, orr, x, g1, sc2, sh2, g2, wo, lnp, w1, w2, alpha, layer):
    bsz, seq, d = x.shape
    bb, tt = _row_blocks(bsz, seq)
    xmap = lambda b, t: (b, t, 0)
    mmap = lambda b, t: (b, 0, 0)
    mod_spec = pl.BlockSpec((bb, 1, d), mmap)
    return pl.pallas_call(
        functools.partial(_outffn_kernel, alpha=alpha),
        grid=(bsz // bb, seq // tt),
        in_specs=[pl.BlockSpec((bb, tt, GDN_WIDTH), xmap),
                  pl.BlockSpec((bb, tt, RW_WIDTH), xmap),
                  pl.BlockSpec((bb, tt, d), xmap),
                  mod_spec, mod_spec, mod_spec, mod_spec,
                  _layer_spec(wo, layer, single_buffer=True), _layer_spec(lnp, layer),
                  _layer_spec(w1, layer, single_buffer=True), _layer_spec(w2, layer, single_buffer=True)],
        out_specs=pl.BlockSpec((bb, tt, d), xmap),
        out_shape=jax.ShapeDtypeStruct((bsz, seq, d), F32),
        compiler_params=pltpu.CompilerParams(dimension_semantics=("parallel", "parallel"),
                                             vmem_limit_bytes=VMEM_LIMIT),
        name="out_ffn",
    )(og, orr, x, g1, sc2, sh2, g2, wo, lnp, w1, w2)


def _prep_params(w_in, gdn_conv_w, gdn_a_log, gdn_dt_bias, gdn_norm_w, rwkv_mu, rwkv_w0, rwkv_w2,
                 rwkv_a0, rwkv_a2, rwkv_g2, rwkv_kk, rwkv_ka, rwkv_rk, rwkv_ln_w, rwkv_ln_b, w_out,
                 ln1_w, ln1_b, w_ff1, w_ff2, ln2_w, ln2_b):
    depth, d, _ = w_in.shape
    n_gdn = GDN_QKV + GDN_WIDTH + 2 * GDN_HEADS
    wg = jnp.concatenate([w_in[:, :, :n_gdn], jnp.zeros((depth, d, GDN_COLS - n_gdn), F32)], axis=2).astype(BF16)
    wr = w_in[:, :, n_gdn:].astype(BF16)
    gate_params = jnp.zeros((depth, 8, GATE_COLS), F32)
    gate_params = gate_params.at[:, 0, :GDN_HEADS].set(gdn_a_log).at[:, 1, :GDN_HEADS].set(gdn_dt_bias)
    pvec = jnp.stack([rwkv_w0, rwkv_a0, rwkv_kk, rwkv_ka, rwkv_rk, rwkv_ln_w, rwkv_ln_b,
                      jnp.zeros_like(rwkv_w0)], axis=1)
    zeros = jnp.zeros((depth, RW_LORA // 2, RW_WIDTH), F32)
    w2 = jnp.concatenate([rwkv_w2, zeros], axis=1).astype(BF16)
    a2 = jnp.concatenate([zeros, rwkv_a2], axis=1).astype(BF16)
    zero_row = jnp.zeros_like(ln1_w)
    lnp = jnp.stack([ln1_w, ln1_b, ln2_w, ln2_b] + [zero_row] * 4, axis=1)
    return dict(wg=wg, wr=wr, conv_w=gdn_conv_w, gate_params=gate_params,
                norm_w=gdn_norm_w[:, None, :], mu=rwkv_mu[:, None, :], pvec=pvec,
                w2=w2, a2=a2, g2=rwkv_g2.astype(BF16), wo=w_out.astype(BF16), lnp=lnp,
                w1=w_ff1.astype(BF16), w2f=w_ff2.astype(BF16))


def _rwkv_state_to_pairs(s):
    bsz = s.shape[0]
    s = s.reshape(bsz, RW_PAIRS, 2, RW_HD, RW_HD)
    z = jnp.zeros_like(s[:, :, 0])
    top = jnp.concatenate([s[:, :, 0], z], axis=-1)
    bot = jnp.concatenate([z, s[:, :, 1]], axis=-1)
    return jnp.concatenate([top, bot], axis=-2)


def _rwkv_state_from_pairs(sp):
    bsz = sp.shape[0]
    return jnp.stack([sp[:, :, :RW_HD, :RW_HD], sp[:, :, RW_HD:, RW_HD:]], axis=2).reshape(
        bsz, RW_HEADS, RW_HD, RW_HD)


def _trunk(x, mods, st_gdn, st_conv, st_rw, st_shift, p, alpha):
    bsz, seq, d = x.shape
    assert seq >= HALO
    outs = ([], [], [], [])
    for l in range(mods.shape[0]):
        sh1, sc1, g1, sh2, sc2, g2 = [mods[l, :, None, i * d:(i + 1) * d] for i in range(6)]
        conv_halo = jnp.pad(st_conv[l], ((0, 0), (HALO - (CONV_W - 1), 0), (0, 0)))
        shift_halo = jnp.pad(st_shift[l][:, None, :], ((0, 0), (HALO - 1, 0), (0, 0)))
        pg, pr, conv_raw, shift_raw = _inproj_call(x, sc1, sh1, p["wg"], p["wr"], p["conv_w"], p["mu"],
                                                   conv_halo, shift_halo, l)
        og, n_gdn = _gdn_call(pg, st_gdn[l], p["gate_params"], p["norm_w"], l)
        orr, n_rw = _rwkv_call(pr, _rwkv_state_to_pairs(st_rw[l]), p["pvec"], p["w2"], p["a2"], p["g2"], l)
        outs[0].append(n_gdn)
        outs[1].append(conv_raw[:, HALO - (CONV_W - 1):, :])
        outs[2].append(_rwkv_state_from_pairs(n_rw))
        outs[3].append(shift_raw[:, HALO - 1, :])
        x = _outffn_call(og, orr, x, g1, sc2, sh2, g2, p["wo"], p["lnp"], p["w1"], p["w2f"], alpha, l)
    return (x,) + tuple(jnp.stack(o) for o in outs)


def kernel(x_prompt, x_sample, c_prompt, c_sample, state_gdn, state_gdn_conv, state_rwkv, state_rwkv_shift, w_ada, b_ada, w_in, gdn_conv_w, gdn_a_log, gdn_dt_bias, gdn_norm_w, rwkv_mu, rwkv_w0, rwkv_w2, rwkv_a0, rwkv_a2, rwkv_g2, rwkv_kk, rwkv_ka, rwkv_rk, rwkv_ln_w, rwkv_ln_b, w_out, ln1_w, ln1_b, w_ff1, w_ff2, ln2_w, ln2_b):
    depth = w_in.shape[0]
    alpha = (2 * depth) ** DEPTH_ALPHA_POW
    bp = x_prompt.shape[0]
    params = _prep_params(w_in, gdn_conv_w, gdn_a_log, gdn_dt_bias, gdn_norm_w, rwkv_mu, rwkv_w0, rwkv_w2,
                          rwkv_a0, rwkv_a2, rwkv_g2, rwkv_kk, rwkv_ka, rwkv_rk, rwkv_ln_w, rwkv_ln_b,
                          w_out, ln1_w, ln1_b, w_ff1, w_ff2, ln2_w, ln2_b)
    mods = _mod_call(jnp.concatenate([c_prompt, c_sample], axis=0), w_ada, b_ada)
    zeros = lambda *shape: jnp.zeros((depth, bp) + shape, F32)
    y_p, p_gdn, p_conv, p_rw, p_shift = _trunk(
        x_prompt, mods[:, :bp],
        zeros(GDN_HEADS, GDN_D, GDN_D), zeros(CONV_W - 1, GDN_QKV),
        zeros(RW_HEADS, RW_HD, RW_HD), zeros(RW_PROJ), params, alpha)
    y_s, s_gdn, s_conv, s_rw, s_shift = _trunk(
        x_sample, mods[:, bp:], state_gdn, state_gdn_conv, state_rwkv, state_rwkv_shift, params, alpha)
    return (y_p, y_s, p_gdn, p_conv, p_rw, p_shift, s_gdn, s_conv, s_rw, s_shift)
```

```python
import functools

import jax
import jax.numpy as jnp
from jax import lax
from jax.experimental import pallas as pl
from jax.experimental.pallas import tpu as pltpu

F32 = jnp.float32
BF16 = jnp.bfloat16

D_MODEL = 1024
CHUNK = 64
GDN_HEADS = 4
GDN_D = 128
GDN_WIDTH = GDN_HEADS * GDN_D
GDN_QKV = 3 * GDN_WIDTH
CONV_W = 4
GATE_COLS = 128
GDN_COLS = GDN_QKV + GDN_WIDTH + GATE_COLS
RW_HEADS = 8
RW_HD = 64
RW_WIDTH = RW_HEADS * RW_HD
RW_PAIRS = RW_HEADS // 2
RW_LORA = 128
RW_PROJ = 3 * RW_WIDTH + 2 * RW_LORA
D_FF = 4 * D_MODEL
DEPTH_ALPHA_POW = 0.25
LN_EPS = 1e-5
GN_EPS = 64e-5
NORM_EPS = 1e-6
HALO = 8
ROW_TILE = 512
TIME_TILE = 512
FF_TILE = 1024
GDN_GROUP = 8
RW_GROUP = 4
SEQ_BLOCK = 2
INPROJ_QKV_GROUPS = tuple((i, i + 512) for i in range(0, 1536, 512))
INPROJ_RW_GROUPS = tuple((i, i + 512) for i in range(0, 1536, 512)) + ((1536, 1792),)
INPROJ_ROWS = 128
VMEM_LIMIT = 56 * 1024 * 1024


def _bdot(a, b):
    return jnp.dot(a.astype(BF16), b.astype(BF16), preferred_element_type=F32)


def _bdot_nt(a, b):
    return lax.dot_general(a.astype(BF16), b.astype(BF16), (((1,), (1,)), ((), ())),
                           preferred_element_type=F32)


def _bdot_tn(a, b):
    return lax.dot_general(a.astype(BF16), b.astype(BF16), (((0,), (0,)), ((), ())),
                           preferred_element_type=F32)


def _split2(x):
    hi = x.astype(BF16)
    return hi, (x - hi.astype(F32)).astype(BF16)


def _dot3(a, b):
    ah, al = _split2(a)
    bh, bl = _split2(b)
    return (jnp.dot(ah, bh, preferred_element_type=F32) + jnp.dot(ah, bl, preferred_element_type=F32)
            + jnp.dot(al, bh, preferred_element_type=F32))


def _cumsum_rows(ltri_bf16, x):
    hi = x.astype(BF16)
    rest = x - hi.astype(F32)
    mid = rest.astype(BF16)
    lo = (rest - mid.astype(F32)).astype(BF16)
    return (jnp.dot(ltri_bf16, hi, preferred_element_type=F32)
            + jnp.dot(ltri_bf16, mid, preferred_element_type=F32)
            + jnp.dot(ltri_bf16, lo, preferred_element_type=F32))


def _sigmoid(x):
    return 0.5 * jnp.tanh(0.5 * x) + 0.5


def _silu(x):
    return x * _sigmoid(x)


def _softplus(x):
    return jnp.maximum(x, 0.0) + jnp.log(1.0 + jnp.exp(-jnp.abs(x)))


def _pair_blockdiag(x, first):
    zero = jnp.zeros_like(x)
    return jnp.concatenate([jnp.where(first, x, zero), jnp.where(first, zero, x)], axis=0)


def _pair_unit_lower_inverse(a_list, first, eye, refine):
    n = eye.shape[0]
    xs = [eye - a for a in a_list]
    ps = list(a_list)
    bds = [_pair_blockdiag(p.astype(BF16), first) for p in ps]
    terms = 2
    while terms < n:
        ps = [jnp.dot(p.astype(BF16), bd, preferred_element_type=F32) for p, bd in zip(ps, bds)]
        yield
        bds = [_pair_blockdiag(p.astype(BF16), first) for p in ps]
        xs = [x + jnp.dot(x.astype(BF16), bd, preferred_element_type=F32) for x, bd in zip(xs, bds)]
        yield
        terms *= 2
    if not refine:
        return xs
    es = [_dot3(eye + a, _pair_blockdiag(x, first)) - eye for a, x in zip(a_list, xs)]
    yield
    return [x - _bdot(x, _pair_blockdiag(e, first)) for x, e in zip(xs, es)]


def _run(gen):
    while True:
        try:
            next(gen)
        except StopIteration as stop:
            return stop.value


def _run_interleaved(main, side):
    main_val = side_val = None
    main_live = side_live = True
    while main_live or side_live:
        if main_live:
            try:
                next(main)
            except StopIteration as stop:
                main_val, main_live = stop.value, False
        if side_live:
            try:
                next(side)
            except StopIteration as stop:
                side_val, side_live = stop.value, False
    return main_val, side_val


def _layer_norm(t, w, b, eps):
    mu = jnp.mean(t, axis=-1, keepdims=True)
    d = t - mu
    var = jnp.mean(d * d, axis=-1, keepdims=True)
    return d * lax.rsqrt(var + eps) * w + b


def _mod_kernel(c_ref, w_ref, b_ref, o_ref):
    o_ref[0] = _bdot(_silu(c_ref[...]), w_ref[0]) + b_ref[0]


def _mod_call(c_all, w_ada, b_ada):
    depth, d, n = w_ada.shape
    rows = c_all.shape[0]
    tn = 1536
    return pl.pallas_call(
        _mod_kernel,
        grid=(depth, n // tn),
        in_specs=[pl.BlockSpec((rows, d), lambda l, j: (0, 0)),
                  pl.BlockSpec((1, d, tn), lambda l, j: (l, 0, j)),
                  pl.BlockSpec((1, 1, tn), lambda l, j: (l, 0, j))],
        out_specs=pl.BlockSpec((1, rows, tn), lambda l, j: (l, 0, j)),
        out_shape=jax.ShapeDtypeStruct((depth, rows, n), F32),
        compiler_params=pltpu.CompilerParams(dimension_semantics=("parallel", "parallel"),
                                             vmem_limit_bytes=VMEM_LIMIT),
        name="adaln_mod",
    )(c_all, w_ada, b_ada.reshape(depth, 1, n))


def _inproj_kernel(x_ref, sc_ref, sh_ref, wg_ref, wr_ref, cw_ref, mu_ref, cst_ref, sst_ref,
                   og_ref, or_ref, craw_ref, sraw_ref, halo_g, halo_r):
    bb, tt, d = x_ref.shape

    @pl.when(pl.program_id(1) == 0)
    def _():
        halo_g[...] = cst_ref[...]
        halo_r[...] = sst_ref[...]

    h = x_ref[...] * (1.0 + sc_ref[...]) + sh_ref[...]
    hb = h.reshape(bb * tt, d).astype(BF16)
    conv_w = cw_ref[0]
    shift_mix = mu_ref[0]

    def conv_silu(raw, halo, w):
        blk = jnp.concatenate([halo, raw], axis=0)
        prev = pltpu.roll(blk, 1, axis=0)
        acc = (blk * w[3:4, :] + prev * w[2:3, :]
               + pltpu.roll(blk * w[1:2, :] + prev * w[0:1, :], 2, axis=0))
        return _silu(acc[HALO:, :])

    def token_shift(raw, halo, m):
        blk = jnp.concatenate([halo, raw], axis=0)
        return (blk + (pltpu.roll(blk, 1, axis=0) - blk) * m)[HALO:, :]

    rb_rows = min(bb * tt, INPROJ_ROWS)
    seg = min(tt, rb_rows)
    for r0 in range(0, bb * tt, rb_rows):
        hrows = hb[r0:r0 + rb_rows]
        pieces = [(s0,) + divmod(r0 + s0, tt) for s0 in range(0, rb_rows, seg)]
        for lo, hi in INPROJ_QKV_GROUPS:
            raw = jnp.dot(hrows, wg_ref[0, :, lo:hi], preferred_element_type=F32)
            for s0, b, t in pieces:
                og_ref[b, t:t + seg, lo:hi] = conv_silu(raw[s0:s0 + seg], halo_g[b, :, lo:hi], conv_w[:, lo:hi])
                halo_g[b, :, lo:hi] = raw[s0 + seg - HALO:s0 + seg, :]
        raw = jnp.dot(hrows, wg_ref[0, :, GDN_QKV:], preferred_element_type=F32)
        for s0, b, t in pieces:
            og_ref[b, t:t + seg, GDN_QKV:] = raw[s0:s0 + seg]
        for lo, hi in INPROJ_RW_GROUPS:
            raw = jnp.dot(hrows, wr_ref[0, :, lo:hi], preferred_element_type=F32)
            for s0, b, t in pieces:
                or_ref[b, t:t + seg, lo:hi] = token_shift(raw[s0:s0 + seg], halo_r[b, :, lo:hi], shift_mix[:, lo:hi])
                halo_r[b, :, lo:hi] = raw[s0 + seg - HALO:s0 + seg, :]
    craw_ref[...] = halo_g[...]
    sraw_ref[...] = halo_r[...]


def _row_blocks(bsz, seq):
    tt = min(seq, ROW_TILE)
    bb = max(1, min(bsz, ROW_TILE // tt))
    assert seq % tt == 0 and bsz % bb == 0 and tt % 8 == 0
    return bb, tt


def _layer_spec(arr, layer, single_buffer=False):
    kwargs = dict(pipeline_mode=pl.Buffered(1)) if single_buffer else {}
    return pl.BlockSpec((1,) + arr.shape[1:], lambda b, t: (layer,) + (0,) * (arr.ndim - 1), **kwargs)


def _inproj_call(x, sc, sh, wg, wr, conv_w, mu, conv_halo, shift_halo, layer):
    bsz, seq, d = x.shape
    bb, tt = _row_blocks(bsz, seq)
    assert tt >= HALO
    xmap = lambda b, t: (b, t, 0)
    mmap = lambda b, t: (b, 0, 0)
    return pl.pallas_call(
        _inproj_kernel,
        grid=(bsz // bb, seq // tt),
        in_specs=[pl.BlockSpec((bb, tt, d), xmap),
                  pl.BlockSpec((bb, 1, d), mmap),
                  pl.BlockSpec((bb, 1, d), mmap),
                  _layer_spec(wg, layer, single_buffer=True),
                  _layer_spec(wr, layer, single_buffer=True),
                  _layer_spec(conv_w, layer), _layer_spec(mu, layer),
                  pl.BlockSpec((bb, HALO, GDN_QKV), mmap),
                  pl.BlockSpec((bb, HALO, RW_PROJ), mmap)],
        out_specs=[pl.BlockSpec((bb, tt, GDN_COLS), xmap),
                   pl.BlockSpec((bb, tt, RW_PROJ), xmap),
                   pl.BlockSpec((bb, HALO, GDN_QKV), mmap),
                   pl.BlockSpec((bb, HALO, RW_PROJ), mmap)],
        out_shape=[jax.ShapeDtypeStruct((bsz, seq, GDN_COLS), F32),
                   jax.ShapeDtypeStruct((bsz, seq, RW_PROJ), F32),
                   jax.ShapeDtypeStruct((bsz, HALO, GDN_QKV), F32),
                   jax.ShapeDtypeStruct((bsz, HALO, RW_PROJ), F32)],
        scratch_shapes=[pltpu.VMEM((bb, HALO, GDN_QKV), F32),
                        pltpu.VMEM((bb, HALO, RW_PROJ), F32)],
        compiler_params=pltpu.CompilerParams(dimension_semantics=("parallel", "arbitrary"),
                                             vmem_limit_bytes=VMEM_LIMIT),
        name="in_proj",
    )(x, sc, sh, wg, wr, conv_w, mu, conv_halo, shift_halo)


def _gdn_kernel(pg_ref, s0_ref, gp_ref, nw_ref, og_ref, sout_ref,
                s_scr, qd_scr, ov_scr, m_scr, u_scr, dl_scr, *, tt, group, bb):
    c = CHUNK
    gc = group * c
    n_chunks = tt // c
    n_groups = n_chunks // group
    d = GDN_D
    t_idx = pl.program_id(1)

    @pl.when(t_idx == 0)
    def _():
        for b in range(bb):
            s_scr[b * GDN_HEADS:(b + 1) * GDN_HEADS] = s0_ref[b]

    row = lax.broadcasted_iota(jnp.int32, (c, 2 * c), 0)
    lane = lax.broadcasted_iota(jnp.int32, (c, 2 * c), 1)
    first = lane < c
    col = lane & (c - 1)
    causal = col <= row
    strict = col < row
    eye = jnp.where(col == row, 1.0, 0.0).astype(F32)
    grow = lax.broadcasted_iota(jnp.int32, (gc, gc), 0)
    gcol = lax.broadcasted_iota(jnp.int32, (gc, gc), 1)
    ltri = jnp.where(((grow & -c) == (gcol & -c)) & (gcol <= grow), 1.0, 0.0).astype(BF16)
    ones_d = jnp.ones((d, d), BF16)
    zero_2d = jnp.zeros((c, 2 * d), F32)
    neg_a = -jnp.exp(gp_ref[0, 0:1, :])
    dt_bias = gp_ref[0, 1:2, :]
    nw = nw_ref[0]
    heads = range(GDN_HEADS)

    def per_chunk_row(x, r):
        return jnp.concatenate([jnp.broadcast_to(x[j * c + r:j * c + r + 1, :], (c, x.shape[1]))
                                for j in range(group)], axis=0)

    def front(step):
        b, gi = divmod(step, n_groups)
        t0 = gi * gc
        qkv = pg_ref[b, t0:t0 + gc, 0:GDN_QKV]
        gates = pg_ref[b, t0:t0 + gc, GDN_QKV + GDN_WIDTH:GDN_COLS]
        log_a = neg_a * _softplus(gates + dt_bias)
        beta_all = _sigmoid(gates)
        g_all = _cumsum_rows(ltri, log_a)
        yield
        g_rows = g_all.T
        eg_all = jnp.exp(g_all)
        g_last = per_chunk_row(g_all, c - 1)
        dec_all = jnp.exp(g_last - g_all)
        for j in range(group):
            dl_scr[step * group + j] = jnp.broadcast_to(
                jnp.exp(g_all[(j + 1) * c - 1:(j + 1) * c, :]), (8, GATE_COLS))
        yield
        q = [qkv[:, h * d:(h + 1) * d] for h in heads]
        k = [qkv[:, GDN_WIDTH + h * d:GDN_WIDTH + (h + 1) * d] for h in heads]
        v = [qkv[:, 2 * GDN_WIDTH + h * d:2 * GDN_WIDTH + (h + 1) * d] for h in heads]
        ss = _bdot(jnp.concatenate([t * t for t in q + k], axis=0), ones_d)
        yield
        qn, kn = [], []
        for h in heads:
            qn.append(q[h] * lax.rsqrt(ss[h * gc:(h + 1) * gc] + NORM_EPS) * (d ** -0.5))
            kn.append(k[h] * lax.rsqrt(ss[(GDN_HEADS + h) * gc:(GDN_HEADS + h + 1) * gc] + NORM_EPS))
            yield
        items = []
        for j in range(group):
            rs = slice(j * c, (j + 1) * c)
            for p in range(GDN_HEADS // 2):
                pair = (2 * p, 2 * p + 1)
                rhs = []
                for h in pair:
                    beta = beta_all[rs, GDN_HEADS + h:GDN_HEADS + h + 1]
                    rhs.append(jnp.concatenate([v[h][rs] * beta, kn[h][rs] * (beta * eg_all[rs, h:h + 1])], axis=-1))
                items.append(dict(
                    j=j, pair=pair, rs=rs,
                    ks2=jnp.concatenate([kn[h][rs] for h in pair], axis=0).astype(BF16),
                    qs2=jnp.concatenate([qn[h][rs] for h in pair], axis=0).astype(BF16),
                    rhs_bd=jnp.concatenate([jnp.concatenate([rhs[0], zero_2d], axis=-1),
                                            jnp.concatenate([zero_2d, rhs[1]], axis=-1)], axis=0),
                    qdec=[qn[h][rs] * eg_all[rs, h:h + 1] for h in pair],
                    kdec=[kn[h][rs] * dec_all[rs, h:h + 1] for h in pair]))
            yield
        return dict(items=items, g_all=g_all, g_rows=g_rows, beta_all=beta_all)

    def back(step, fr):
        items, g_all, g_rows, beta_all = fr["items"], fr["g_all"], fr["g_rows"], fr["beta_all"]
        r0 = step * gc
        kk2s = [_bdot_nt(it["ks2"], it["ks2"]) for it in items]
        yield
        qk2s = [_bdot_nt(it["qs2"], it["ks2"]) for it in items]
        yield
        for it, kk2, qk2 in zip(items, kk2s, qk2s):
            (h1, h2), rs = it["pair"], it["rs"]
            kk_cat = jnp.where(first, kk2[0:c], kk2[c:2 * c])
            qk_cat = jnp.where(first, qk2[0:c], qk2[c:2 * c])
            g_col_cat = jnp.where(first, g_all[rs, h1:h1 + 1], g_all[rs, h2:h2 + 1])
            g_row_cat = jnp.concatenate([g_rows[h1:h1 + 1, rs], g_rows[h2:h2 + 1, rs]], axis=1)
            beta_cat = jnp.where(first, beta_all[rs, GDN_HEADS + h1:GDN_HEADS + h1 + 1],
                                 beta_all[rs, GDN_HEADS + h2:GDN_HEADS + h2 + 1])
            gam = jnp.where(causal, jnp.exp(jnp.where(causal, g_col_cat - g_row_cat, 0.0)), 0.0)
            it["a_cat"] = jnp.where(strict, beta_cat * kk_cat * gam, 0.0)
            it["qk_cat"] = jnp.where(causal, qk_cat * gam, 0.0)
        yield
        t_invs = yield from _pair_unit_lower_inverse([it["a_cat"] for it in items], first, eye, refine=True)
        sols = [_bdot(t_inv, it["rhs_bd"]) for it, t_inv in zip(items, t_invs)]
        yield
        corrs = [_bdot(it["qk_cat"], jnp.concatenate(
            [jnp.concatenate([sol[:, d:2 * d], sol[:, 0:d], zero_2d], axis=-1),
             jnp.concatenate([zero_2d, sol[:, 3 * d:4 * d], sol[:, 2 * d:3 * d]], axis=-1)], axis=0))
            for it, sol in zip(items, sols)]
        yield
        mus = [[_bdot_tn(it["kdec"][i], jnp.concatenate([sol[:, (2 * i + 1) * d:(2 * i + 2) * d],
                                                         sol[:, 2 * i * d:(2 * i + 1) * d]], axis=-1))
                for i in range(2)] for it, sol in zip(items, sols)]
        yield
        for it, corr, mu_pair in zip(items, corrs, mus):
            ci = step * group + it["j"]
            row = r0 + it["j"] * c
            for i, h in enumerate(it["pair"]):
                qd_scr[ci, h] = (it["qdec"][i] - corr[:, 2 * i * d:(2 * i + 1) * d]).astype(BF16)
                ov_scr[row:row + c, h * d:(h + 1) * d] = corr[:, (2 * i + 1) * d:(2 * i + 2) * d]
                m_scr[ci, h] = mu_pair[i][:, 0:d].astype(BF16)
                u_scr[ci, h] = mu_pair[i][:, d:2 * d]

    n_steps = bb * n_groups
    fr = _run(front(0))
    for step in range(n_steps):
        if step + 1 < n_steps:
            _, fr = _run_interleaved(back(step, fr), front(step + 1))
        else:
            _run(back(step, fr))

    def pass2(ci, carry):
        cis = [b * n_chunks + ci for b in range(bb)]
        rows = [pl.ds(pl.multiple_of(b * tt + ci * c, c), c) for b in range(bb)]
        dls = [dl_scr[cb][0:1, :] for cb in cis]
        s_old = [[s_scr[b * GDN_HEADS + h] for h in heads] for b in range(bb)]
        s_bf = [[s.astype(BF16) for s in sb] for sb in s_old]
        o = [[jnp.dot(qd_scr[cis[b], h], s_bf[b][h], preferred_element_type=F32) for h in heads]
             for b in range(bb)]
        ms = [[jnp.dot(m_scr[cis[b], h], s_bf[b][h], preferred_element_type=F32) for h in heads]
              for b in range(bb)]
        for b in range(bb):
            ov_scr[rows[b], :] = jnp.concatenate(o[b], axis=-1) + ov_scr[rows[b], :]
        s_scr[...] = jnp.stack([s_old[b][h] * dls[b][:, h:h + 1] - ms[b][h] + u_scr[cis[b], h]
                                for b in range(bb) for h in heads])
        return carry

    lax.fori_loop(0, n_chunks, pass2, 0)

    def pass3(step, carry):
        b = step // n_groups
        t0 = pl.multiple_of((step - b * n_groups) * gc, gc)
        r0 = pl.multiple_of(step * gc, gc)
        o = ov_scr[pl.ds(r0, gc), :]
        o_all = jnp.concatenate([o[:, h * d:(h + 1) * d] for h in heads], axis=0)
        ms = _bdot(o_all * o_all, ones_d) * (1.0 / d)
        o_all = o_all * lax.rsqrt(ms + NORM_EPS) * nw
        z = pg_ref[b, pl.ds(t0, gc), GDN_QKV:GDN_QKV + GDN_WIDTH]
        og_ref[b, pl.ds(t0, gc), :] = jnp.concatenate(
            [o_all[h * gc:(h + 1) * gc] for h in heads], axis=-1) * _silu(z)
        return carry

    lax.fori_loop(0, bb * n_groups, pass3, 0)

    @pl.when(t_idx == pl.num_programs(1) - 1)
    def _():
        for b in range(bb):
            sout_ref[b] = s_scr[b * GDN_HEADS:(b + 1) * GDN_HEADS]


def _gdn_call(pg, s0, gate_params, norm_w, layer):
    bsz, seq, _ = pg.shape
    tt = min(seq, TIME_TILE)
    n_chunks = tt // CHUNK
    group = GDN_GROUP if n_chunks % GDN_GROUP == 0 else 1
    bb = SEQ_BLOCK if bsz % SEQ_BLOCK == 0 else 1
    assert seq % tt == 0 and tt % CHUNK == 0
    smap = lambda b, t: (b, 0, 0, 0)
    return pl.pallas_call(
        functools.partial(_gdn_kernel, tt=tt, group=group, bb=bb),
        grid=(bsz // bb, seq // tt),
        in_specs=[pl.BlockSpec((bb, tt, GDN_COLS), lambda b, t: (b, t, 0)),
                  pl.BlockSpec((bb, GDN_HEADS, GDN_D, GDN_D), smap),
                  _layer_spec(gate_params, layer), _layer_spec(norm_w, layer)],
        out_specs=[pl.BlockSpec((bb, tt, GDN_WIDTH), lambda b, t: (b, t, 0)),
                   pl.BlockSpec((bb, GDN_HEADS, GDN_D, GDN_D), smap)],
        out_shape=[jax.ShapeDtypeStruct((bsz, seq, GDN_WIDTH), F32),
                   jax.ShapeDtypeStruct((bsz, GDN_HEADS, GDN_D, GDN_D), F32)],
        scratch_shapes=[pltpu.VMEM((bb * GDN_HEADS, GDN_D, GDN_D), F32),
                        pltpu.VMEM((bb * n_chunks, GDN_HEADS, CHUNK, GDN_D), BF16),
                        pltpu.VMEM((bb * tt, GDN_WIDTH), F32),
                        pltpu.VMEM((bb * n_chunks, GDN_HEADS, GDN_D, GDN_D), BF16),
                        pltpu.VMEM((bb * n_chunks, GDN_HEADS, GDN_D, GDN_D), F32),
                        pltpu.VMEM((bb * n_chunks, 8, GATE_COLS), F32)],
        compiler_params=pltpu.CompilerParams(dimension_semantics=("parallel", "arbitrary"),
                                             vmem_limit_bytes=VMEM_LIMIT),
        name="gated_deltanet",
    )(pg, s0, gate_params, norm_w)


def _rwkv_kernel(pr_ref, s0_ref, pv_ref, w2_ref, a2_ref, g2_ref,
                 or_ref, sout_ref, s_scr, rq_scr, ov_scr, m_scr, u_scr, dec_scr, gate_scr, bonus_scr,
                 *, tt, group, bb):
    c = CHUNK
    gc = group * c
    lanes = 2 * RW_HD
    n_chunks = tt // c
    n_groups = n_chunks // group
    t_idx = pl.program_id(1)

    @pl.when(t_idx == 0)
    def _():
        for b in range(bb):
            s_scr[b * RW_PAIRS:(b + 1) * RW_PAIRS] = s0_ref[b]

    grow = lax.broadcasted_iota(jnp.int32, (gc, gc), 0)
    gcol = lax.broadcasted_iota(jnp.int32, (gc, gc), 1)
    same_chunk = (grow & -c) == (gcol & -c)
    ltri = jnp.where(same_chunk & (gcol <= grow), 1.0, 0.0).astype(BF16)
    prow = lax.broadcasted_iota(jnp.int32, (c, lanes), 0)
    plane = lax.broadcasted_iota(jnp.int32, (c, lanes), 1)
    first = plane < RW_HD
    pcol = plane & (RW_HD - 1)
    tri_strict = pcol < prow
    tri_incl = pcol <= prow
    eye = jnp.where(pcol == prow, 1.0, 0.0).astype(F32)
    srow = lax.broadcasted_iota(jnp.int32, (lanes, lanes), 0)
    scol = lax.broadcasted_iota(jnp.int32, (lanes, lanes), 1)
    same_head = (srow < RW_HD) == (scol < RW_HD)
    seg_ones = jnp.where(same_head, 1.0, 0.0).astype(BF16)
    w0 = pv_ref[0, 0:1, :]
    a0 = pv_ref[0, 1:2, :]
    kk_p = pv_ref[0, 2:3, :]
    ka_p = pv_ref[0, 3:4, :]
    rk_p = pv_ref[0, 4:5, :]
    ln_w = pv_ref[0, 5:6, :]
    ln_b = pv_ref[0, 6:7, :]
    pairs = range(RW_PAIRS)
    sls = [slice(p * lanes, (p + 1) * lanes) for p in pairs]

    def per_chunk_row(x, row):
        return jnp.concatenate([jnp.broadcast_to(x[j * c + row:j * c + row + 1, :], (c, x.shape[1]))
                                for j in range(group)], axis=0)

    def front(step):
        b, gi = divmod(step, n_groups)
        r0 = step * gc
        xs = pr_ref[b, gi * gc:(gi + 1) * gc, :]
        r = xs[:, 0:RW_WIDTH]
        kr = xs[:, RW_WIDTH:2 * RW_WIDTH]
        vr = xs[:, 2 * RW_WIDTH:3 * RW_WIDTH]
        lora_in = xs[:, 3 * RW_WIDTH:3 * RW_WIDTH + RW_LORA]
        xg = xs[:, 3 * RW_WIDTH + RW_LORA:RW_PROJ]
        w_log = -_softplus(-(w0 + _bdot(jnp.tanh(lora_in), w2_ref[0]))) - 0.5
        log_w = -jnp.exp(w_log)
        yield
        a = _sigmoid(a0 + _bdot(lora_in, a2_ref[0]))
        gate_scr[r0:r0 + gc, :] = _bdot(_sigmoid(xg), g2_ref[0])
        kk_raw = kr * kk_p
        k2 = kr * (1.0 + (a - 1.0) * ka_p)
        rk = r * k2 * rk_p
        yield
        g_cum = _cumsum_rows(ltri, log_w)
        mid_rows = [g_cum[j * c + c // 2 - 1:j * c + c // 2, :] for j in range(group)]
        end_rows = [g_cum[(j + 1) * c - 1:(j + 1) * c, :] for j in range(group)]
        g_mid = per_chunk_row(g_cum, c // 2 - 1)
        yield
        e_left = jnp.exp(g_cum - g_mid)
        yield
        e_left_prev = jnp.exp(g_cum - log_w - g_mid)
        yield
        e_right = jnp.exp(g_mid - g_cum)
        e_mid = [jnp.exp(m) for m in mid_rows]
        e_end_mid = [jnp.exp(e - m) for e, m in zip(end_rows, mid_rows)]
        for j in range(group):
            dec_scr[step * group + j] = jnp.broadcast_to(jnp.exp(end_rows[j]), (8, RW_WIDTH))
        yield
        seg = _bdot(jnp.concatenate([kk_raw[:, sl] * kk_raw[:, sl] for sl in sls]
                                    + [rk[:, sl] for sl in sls], axis=0), seg_ones)
        bonus_scr[r0:r0 + gc, :] = jnp.concatenate(
            [seg[(RW_PAIRS + p) * gc:(RW_PAIRS + p + 1) * gc] for p in pairs], axis=-1) * vr
        yield
        items = []
        for j in range(group):
            rs = slice(j * c, (j + 1) * c)
            for p, sl in enumerate(sls):
                kappa = kk_raw[rs, sl] * lax.rsqrt(seg[p * gc + j * c:p * gc + (j + 1) * c] + NORM_EPS)
                r_l = r[rs, sl] * e_left[rs, sl]
                kap_l = kappa * e_left_prev[rs, sl]
                k_r = k2[rs, sl] * e_right[rs, sl]
                b_r = kappa * a[rs, sl] * e_right[rs, sl]
                items.append(dict(
                    j=j, p=p, v=vr[rs, sl], kg=kap_l * e_mid[j][:, sl], rg=r_l * e_mid[j][:, sl],
                    k_d=k_r * e_end_mid[j][:, sl], b_d=b_r * e_end_mid[j][:, sl],
                    lhs=jnp.concatenate([kap_l, r_l], axis=0),
                    rhs=jnp.concatenate([jnp.where(first, b_r, 0.0), jnp.where(first, k_r, 0.0),
                                         jnp.where(first, 0.0, k_r), jnp.where(first, 0.0, b_r)], axis=0)))
            yield
        return items

    def back(step, items):
        r0 = step * gc
        prods = [_bdot_nt(it["lhs"], it["rhs"]) for it in items]
        yield
        for it, res in zip(items, prods):
            kap_1, kap_2 = res[0:c, 0:lanes], res[0:c, lanes:2 * lanes]
            r_1, r_2 = res[c:2 * c, 0:lanes], res[c:2 * c, lanes:2 * lanes]
            it["a_cat"] = jnp.where(tri_strict, jnp.where(first, kap_1, kap_2), 0.0)
            kb_cat = jnp.where(tri_strict, jnp.where(first, kap_2, kap_1), 0.0)
            qk_cat = jnp.where(tri_incl, jnp.where(first, r_2, r_1), 0.0)
            it["kq"] = jnp.concatenate([kb_cat, qk_cat], axis=0)
            it["qb_cat"] = jnp.where(tri_incl, jnp.where(first, r_1, r_2), 0.0)
            v = it["v"]
            it["v_swap"] = jnp.concatenate([jnp.where(first, 0.0, v), jnp.where(first, v, 0.0)], axis=0)
        intras = [_bdot(it["kq"], it["v_swap"]) for it in items]
        yield
        t_invs = yield from _pair_unit_lower_inverse([it["a_cat"] for it in items], first, eye, refine=False)
        sols = [_bdot(t_inv, jnp.concatenate([_pair_blockdiag(it["kg"], first),
                                              _pair_blockdiag(intra[0:c], first)], axis=-1))
                for it, t_inv, intra in zip(items, t_invs, intras)]
        yield
        corrs = [_bdot(it["qb_cat"], jnp.concatenate([_pair_blockdiag(sol[:, 0:lanes], first),
                                                      _pair_blockdiag(sol[:, lanes:2 * lanes], first)], axis=-1))
                 for it, sol in zip(items, sols)]
        yield
        us = [_bdot_tn(jnp.concatenate([it["v"], -sol[:, lanes:2 * lanes]], axis=0),
                       jnp.concatenate([it["k_d"], it["b_d"]], axis=0)) for it, sol in zip(items, sols)]
        yield
        ms = [_bdot_tn(sol[:, 0:lanes], it["b_d"]) for it, sol in zip(items, sols)]
        yield
        for item, intra, corr, u, m in zip(items, intras, corrs, us, ms):
            ci = step * group + item["j"]
            p = item["p"]
            rq_scr[ci, p] = (item["rg"] - corr[:, 0:lanes]).astype(BF16)
            row = r0 + item["j"] * c
            ov_scr[row:row + c, sls[p]] = intra[c:2 * c] - corr[:, lanes:2 * lanes]
            u_scr[ci, p] = jnp.where(same_head, u, 0.0)
            m_scr[ci, p] = jnp.where(same_head, m, 0.0).astype(BF16)

    n_steps = bb * n_groups
    items = _run(front(0))
    for step in range(n_steps):
        if step + 1 < n_steps:
            _, items = _run_interleaved(back(step, items), front(step + 1))
        else:
            _run(back(step, items))

    def pass2(ci, carry):
        cis = [b * n_chunks + ci for b in range(bb)]
        rows = [pl.ds(pl.multiple_of(b * tt + ci * c, c), c) for b in range(bb)]
        decs = [dec_scr[cb][0:1, :] for cb in cis]
        s_old = [[s_scr[b * RW_PAIRS + p] for p in pairs] for b in range(bb)]
        s_bf = [[s.astype(BF16) for s in sb] for sb in s_old]
        o = [[_bdot_nt(rq_scr[cis[b], p], s_bf[b][p]) for p in pairs] for b in range(bb)]
        sm = [[jnp.dot(s_bf[b][p], m_scr[cis[b], p], preferred_element_type=F32) for p in pairs]
              for b in range(bb)]
        for b in range(bb):
            ov_scr[rows[b], :] = jnp.concatenate(o[b], axis=-1) + ov_scr[rows[b], :]
        s_scr[...] = jnp.stack([s_old[b][p] * decs[b][:, sls[p]] - sm[b][p] + u_scr[cis[b], p]
                                for b in range(bb) for p in pairs])
        return carry

    lax.fori_loop(0, n_chunks, pass2, 0)

    def pass3(step, carry):
        b = step // n_groups
        gi = step - b * n_groups
        r0 = pl.multiple_of(step * gc, gc)
        o = ov_scr[pl.ds(r0, gc), :]
        o_all = jnp.concatenate([o[:, sl] for sl in sls], axis=0)
        mean = _bdot(o_all, seg_ones) * (1.0 / RW_HD)
        dev = o_all - mean
        var = _bdot(dev * dev, seg_ones) * (1.0 / RW_HD)
        o_n = dev * lax.rsqrt(var + GN_EPS)
        o_n = jnp.concatenate([o_n[p * gc:(p + 1) * gc] for p in pairs], axis=-1) * ln_w + ln_b
        or_ref[b, pl.ds(pl.multiple_of(gi * gc, gc), gc), :] = (
            (o_n + bonus_scr[pl.ds(r0, gc), :]) * gate_scr[pl.ds(r0, gc), :])
        return carry

    lax.fori_loop(0, bb * n_groups, pass3, 0)

    @pl.when(t_idx == pl.num_programs(1) - 1)
    def _():
        for b in range(bb):
            sout_ref[b] = s_scr[b * RW_PAIRS:(b + 1) * RW_PAIRS]


def _rwkv_call(pr, s0, pvec, w2, a2, g2, layer):
    bsz, seq, _ = pr.shape
    tt = min(seq, TIME_TILE)
    n_chunks = tt // CHUNK
    group = RW_GROUP if n_chunks % RW_GROUP == 0 else 1
    bb = SEQ_BLOCK if bsz % SEQ_BLOCK == 0 else 1
    assert seq % tt == 0 and tt % CHUNK == 0
    lanes = 2 * RW_HD
    smap = lambda b, t: (b, 0, 0, 0)
    return pl.pallas_call(
        functools.partial(_rwkv_kernel, tt=tt, group=group, bb=bb),
        grid=(bsz // bb, seq // tt),
        in_specs=[pl.BlockSpec((bb, tt, RW_PROJ), lambda b, t: (b, t, 0)),
                  pl.BlockSpec((bb, RW_PAIRS, lanes, lanes), smap),
                  _layer_spec(pvec, layer),
                  _layer_spec(w2, layer), _layer_spec(a2, layer), _layer_spec(g2, layer)],
        out_specs=[pl.BlockSpec((bb, tt, RW_WIDTH), lambda b, t: (b, t, 0)),
                   pl.BlockSpec((bb, RW_PAIRS, lanes, lanes), smap)],
        out_shape=[jax.ShapeDtypeStruct((bsz, seq, RW_WIDTH), F32),
                   jax.ShapeDtypeStruct((bsz, RW_PAIRS, lanes, lanes), F32)],
        scratch_shapes=[pltpu.VMEM((bb * RW_PAIRS, lanes, lanes), F32),
                        pltpu.VMEM((bb * n_chunks, RW_PAIRS, CHUNK, lanes), BF16),
                        pltpu.VMEM((bb * tt, RW_WIDTH), F32),
                        pltpu.VMEM((bb * n_chunks, RW_PAIRS, lanes, lanes), BF16),
                        pltpu.VMEM((bb * n_chunks, RW_PAIRS, lanes, lanes), F32),
                        pltpu.VMEM((bb * n_chunks, 8, RW_WIDTH), F32),
                        pltpu.VMEM((bb * tt, RW_WIDTH), F32),
                        pltpu.VMEM((bb * tt, RW_WIDTH), F32)],
        compiler_params=pltpu.CompilerParams(dimension_semantics=("parallel", "arbitrary"),
                                             vmem_limit_bytes=VMEM_LIMIT),
        name="rwkv7",
    )(pr, s0, pvec, w2, a2, g2)


def _outffn_kernel(og_ref, or_ref, x_ref, g1_ref, sc_ref, sh_ref, g2_ref, wo_ref, lnp_ref,
                   w1_ref, w2_ref, o_ref, *, alpha):
    bb, tt, d = x_ref.shape
    rows = bb * tt
    y = (jnp.dot(og_ref[...].reshape(rows, GDN_WIDTH).astype(BF16), wo_ref[0, 0:GDN_WIDTH, :],
                 preferred_element_type=F32)
         + jnp.dot(or_ref[...].reshape(rows, RW_WIDTH).astype(BF16), wo_ref[0, GDN_WIDTH:, :],
                   preferred_element_type=F32)).reshape(bb, tt, d)
    x1 = _layer_norm(alpha * x_ref[...] + g1_ref[...] * y, lnp_ref[0, 0:1, :], lnp_ref[0, 1:2, :], LN_EPS)
    hb = (x1 * (1.0 + sc_ref[...]) + sh_ref[...]).reshape(rows, d).astype(BF16)
    f = jnp.zeros((rows, d), F32)
    for j in range(D_FF // FF_TILE):
        hid = jnp.dot(hb, w1_ref[0, :, j * FF_TILE:(j + 1) * FF_TILE], preferred_element_type=F32)
        hid = jnp.square(jnp.maximum(hid, 0.0)).astype(BF16)
        f = f + jnp.dot(hid, w2_ref[0, j * FF_TILE:(j + 1) * FF_TILE, :], preferred_element_type=F32)
    o_ref[...] = _layer_norm(alpha * x1 + g2_ref[...] * f.reshape(bb, tt, d),
                             lnp_ref[0, 2:3, :], lnp_ref[0, 3:4, :], LN_EPS)


def _outffn_call(og, orr, x, g1, sc2, sh2, g2, wo, lnp, w1, w2, alpha, layer):
    bsz, seq, d = x.shape
    bb, tt = _row_blocks(bsz, seq)
    xmap = lambda b, t: (b, t, 0)
    mmap = lambda b, t: (b, 0, 0)
    mod_spec = pl.BlockSpec((bb, 1, d), mmap)
    return pl.pallas_call(
        functools.partial(_outffn_kernel, alpha=alpha),
        grid=(bsz // bb, seq // tt),
        in_specs=[pl.BlockSpec((bb, tt, GDN_WIDTH), xmap),
                  pl.BlockSpec((bb, tt, RW_WIDTH), xmap),
                  pl.BlockSpec((bb, tt, d), xmap),
                  mod_spec, mod_spec, mod_spec, mod_spec,
                  _layer_spec(wo, layer, single_buffer=True), _layer_spec(lnp, layer),
                  _layer_spec(w1, layer, single_buffer=True), _layer_spec(w2, layer, single_buffer=True)],
        out_specs=pl.BlockSpec((bb, tt, d), xmap),
        out_shape=jax.ShapeDtypeStruct((bsz, seq, d), F32),
        compiler_params=pltpu.CompilerParams(dimension_semantics=("parallel", "parallel"),
                                             vmem_limit_bytes=VMEM_LIMIT),
        name="out_ffn",
    )(og, orr, x, g1, sc2, sh2, g2, wo, lnp, w1, w2)


def _prep_params(w_in, gdn_conv_w, gdn_a_log, gdn_dt_bias, gdn_norm_w, rwkv_mu, rwkv_w0, rwkv_w2,
                 rwkv_a0, rwkv_a2, rwkv_g2, rwkv_kk, rwkv_ka, rwkv_rk, rwkv_ln_w, rwkv_ln_b, w_out,
                 ln1_w, ln1_b, w_ff1, w_ff2, ln2_w, ln2_b):
    depth, d, _ = w_in.shape
    n_gdn = GDN_QKV + GDN_WIDTH + 2 * GDN_HEADS
    w_in = w_in.astype(BF16)
    wg = jnp.concatenate([w_in[:, :, :n_gdn], jnp.zeros((depth, d, GDN_COLS - n_gdn), BF16)], axis=2)
    wr = w_in[:, :, n_gdn:]
    gate_params = jnp.zeros((depth, 8, GATE_COLS), F32)
    gate_params = gate_params.at[:, 0, :GDN_HEADS].set(gdn_a_log).at[:, 1, :GDN_HEADS].set(gdn_dt_bias)
    pvec = jnp.stack([rwkv_w0, rwkv_a0, rwkv_kk, rwkv_ka, rwkv_rk, rwkv_ln_w, rwkv_ln_b,
                      jnp.zeros_like(rwkv_w0)], axis=1)
    zeros = jnp.zeros((depth, RW_LORA // 2, RW_WIDTH), F32)
    w2 = jnp.concatenate([rwkv_w2, zeros], axis=1).astype(BF16)
    a2 = jnp.concatenate([zeros, rwkv_a2], axis=1).astype(BF16)
    zero_row = jnp.zeros_like(ln1_w)
    lnp = jnp.stack([ln1_w, ln1_b, ln2_w, ln2_b] + [zero_row] * 4, axis=1)
    return dict(wg=wg, wr=wr, conv_w=gdn_conv_w, gate_params=gate_params,
                norm_w=gdn_norm_w[:, None, :], mu=rwkv_mu[:, None, :], pvec=pvec,
                w2=w2, a2=a2, g2=rwkv_g2.astype(BF16), wo=w_out.astype(BF16), lnp=lnp,
                w1=w_ff1.astype(BF16), w2f=w_ff2.astype(BF16))


def _rwkv_state_to_pairs(s):
    bsz = s.shape[0]
    s = s.reshape(bsz, RW_PAIRS, 2, RW_HD, RW_HD)
    z = jnp.zeros_like(s[:, :, 0])
    top = jnp.concatenate([s[:, :, 0], z], axis=-1)
    bot = jnp.concatenate([z, s[:, :, 1]], axis=-1)
    return jnp.concatenate([top, bot], axis=-2)


def _rwkv_state_from_pairs(sp):
    bsz = sp.shape[0]
    return jnp.stack([sp[:, :, :RW_HD, :RW_HD], sp[:, :, RW_HD:, RW_HD:]], axis=2).reshape(
        bsz, RW_HEADS, RW_HD, RW_HD)


def _trunk(x, mods, st_gdn, st_conv, st_rw, st_shift, p, alpha):
    bsz, seq, d = x.shape
    assert seq >= HALO
    outs = ([], [], [], [])
    for l in range(mods.shape[0]):
        sh1, sc1, g1, sh2, sc2, g2 = [mods[l, :, None, i * d:(i + 1) * d] for i in range(6)]
        conv_halo = jnp.pad(st_conv[l], ((0, 0), (HALO - (CONV_W - 1), 0), (0, 0)))
        shift_halo = jnp.pad(st_shift[l][:, None, :], ((0, 0), (HALO - 1, 0), (0, 0)))
        pg, pr, conv_raw, shift_raw = _inproj_call(x, sc1, sh1, p["wg"], p["wr"], p["conv_w"], p["mu"],
                                                   conv_halo, shift_halo, l)
        og, n_gdn = _gdn_call(pg, st_gdn[l], p["gate_params"], p["norm_w"], l)
        orr, n_rw = _rwkv_call(pr, _rwkv_state_to_pairs(st_rw[l]), p["pvec"], p["w2"], p["a2"], p["g2"], l)
        outs[0].append(n_gdn)
        outs[1].append(conv_raw[:, HALO - (CONV_W - 1):, :])
        outs[2].append(_rwkv_state_from_pairs(n_rw))
        outs[3].append(shift_raw[:, HALO - 1, :])
        x = _outffn_call(og, orr, x, g1, sc2, sh2, g2, p["wo"], p["lnp"], p["w1"], p["w2f"], alpha, l)
    return (x,) + tuple(jnp.stack(o) for o in outs)


def kernel(x_prompt, x_sample, c_prompt, c_sample, state_gdn, state_gdn_conv, state_rwkv, state_rwkv_shift, w_ada, b_ada, w_in, gdn_conv_w, gdn_a_log, gdn_dt_bias, gdn_norm_w, rwkv_mu, rwkv_w0, rwkv_w2, rwkv_a0, rwkv_a2, rwkv_g2, rwkv_kk, rwkv_ka, rwkv_rk, rwkv_ln_w, rwkv_ln_b, w_out, ln1_w, ln1_b, w_ff1, w_ff2, ln2_w, ln2_b):
    depth = w_in.shape[0]
    alpha = (2 * depth) ** DEPTH_ALPHA_POW
    bp = x_prompt.shape[0]
    params = _prep_params(w_in, gdn_conv_w, gdn_a_log, gdn_dt_bias, gdn_norm_w, rwkv_mu, rwkv_w0, rwkv_w2,
                          rwkv_a0, rwkv_a2, rwkv_g2, rwkv_kk, rwkv_ka, rwkv_rk, rwkv_ln_w, rwkv_ln_b,
                          w_out, ln1_w, ln1_b, w_ff1, w_ff2, ln2_w, ln2_b)
    mods = _mod_call(jnp.concatenate([c_prompt, c_sample], axis=0), w_ada, b_ada)
    zeros = lambda *shape: jnp.zeros((depth, bp) + shape, F32)
    y_p, p_gdn, p_conv, p_rw, p_shift = _trunk(
        x_prompt, mods[:, :bp],
        zeros(GDN_HEADS, GDN_D, GDN_D), zeros(CONV_W - 1, GDN_QKV),
        zeros(RW_HEADS, RW_HD, RW_HD), zeros(RW_PROJ), params, alpha)
    y_s, s_gdn, s_conv, s_rw, s_shift = _trunk(
        x_sample, mods[:, bp:], state_gdn, state_gdn_conv, state_rwkv, state_rwkv_shift, params, alpha)
    return (y_p, y_s, p_gdn, p_conv, p_rw, p_shift, s_gdn, s_conv, s_rw, s_shift)
```

```python
import functools

import jax
import jax.numpy as jnp
from jax import lax
from jax.experimental import pallas as pl
from jax.experimental.pallas import tpu as pltpu

F32 = jnp.float32
BF16 = jnp.bfloat16

D_MODEL = 1024
CHUNK = 64
GDN_HEADS = 4
GDN_D = 128
GDN_WIDTH = GDN_HEADS * GDN_D
GDN_QKV = 3 * GDN_WIDTH
CONV_W = 4
GATE_COLS = 128
GDN_COLS = GDN_QKV + GDN_WIDTH + GATE_COLS
RW_HEADS = 8
RW_HD = 64
RW_WIDTH = RW_HEADS * RW_HD
RW_PAIRS = RW_HEADS // 2
RW_LORA = 128
RW_PROJ = 3 * RW_WIDTH + 2 * RW_LORA
D_FF = 4 * D_MODEL
DEPTH_ALPHA_POW = 0.25
LN_EPS = 1e-5
GN_EPS = 64e-5
NORM_EPS = 1e-6
HALO = 8
ROW_TILE = 512
TIME_TILE = 512
FF_TILE = 1024
GDN_GROUP = 8
RW_GROUP = 4
RW_TIME_TILE = 256
RW_SEQ_BLOCK = 4
SEQ_BLOCK = 2
INPROJ_QKV_GROUPS = tuple((i, i + 512) for i in range(0, 1536, 512))
INPROJ_RW_GROUPS = tuple((i, i + 512) for i in range(0, 1536, 512)) + ((1536, 1792),)
INPROJ_ROWS = 128
VMEM_LIMIT = 56 * 1024 * 1024


def _bdot(a, b):
    return jnp.dot(a.astype(BF16), b.astype(BF16), preferred_element_type=F32)


def _bdot_nt(a, b):
    return lax.dot_general(a.astype(BF16), b.astype(BF16), (((1,), (1,)), ((), ())),
                           preferred_element_type=F32)


def _bdot_tn(a, b):
    return lax.dot_general(a.astype(BF16), b.astype(BF16), (((0,), (0,)), ((), ())),
                           preferred_element_type=F32)


def _split2(x):
    hi = x.astype(BF16)
    return hi, (x - hi.astype(F32)).astype(BF16)


def _dot3(a, b):
    ah, al = _split2(a)
    bh, bl = _split2(b)
    return (jnp.dot(ah, bh, preferred_element_type=F32) + jnp.dot(ah, bl, preferred_element_type=F32)
            + jnp.dot(al, bh, preferred_element_type=F32))


def _cumsum_rows(ltri_bf16, x):
    hi = x.astype(BF16)
    rest = x - hi.astype(F32)
    mid = rest.astype(BF16)
    lo = (rest - mid.astype(F32)).astype(BF16)
    return (jnp.dot(ltri_bf16, hi, preferred_element_type=F32)
            + jnp.dot(ltri_bf16, mid, preferred_element_type=F32)
            + jnp.dot(ltri_bf16, lo, preferred_element_type=F32))


def _sigmoid(x):
    return 0.5 * jnp.tanh(0.5 * x) + 0.5


def _silu(x):
    return x * _sigmoid(x)


def _softplus(x):
    return jnp.maximum(x, 0.0) + jnp.log(1.0 + jnp.exp(-jnp.abs(x)))


def _pair_blockdiag(x, first):
    zero = jnp.zeros_like(x)
    return jnp.concatenate([jnp.where(first, x, zero), jnp.where(first, zero, x)], axis=0)


def _pair_unit_lower_inverse(a_list, first, eye, refine):
    n = eye.shape[0]
    xs = [eye - a for a in a_list]
    ps = list(a_list)
    bds = [_pair_blockdiag(p.astype(BF16), first) for p in ps]
    terms = 2
    while terms < n:
        ps = [jnp.dot(p.astype(BF16), bd, preferred_element_type=F32) for p, bd in zip(ps, bds)]
        yield
        bds = [_pair_blockdiag(p.astype(BF16), first) for p in ps]
        xs = [x + jnp.dot(x.astype(BF16), bd, preferred_element_type=F32) for x, bd in zip(xs, bds)]
        yield
        terms *= 2
    if not refine:
        return xs
    es = [_dot3(eye + a, _pair_blockdiag(x, first)) - eye for a, x in zip(a_list, xs)]
    yield
    return [x - _bdot(x, _pair_blockdiag(e, first)) for x, e in zip(xs, es)]


def _run(gen):
    while True:
        try:
            next(gen)
        except StopIteration as stop:
            return stop.value


def _run_interleaved(main, side):
    main_val = side_val = None
    main_live = side_live = True
    while main_live or side_live:
        if main_live:
            try:
                next(main)
            except StopIteration as stop:
                main_val, main_live = stop.value, False
        if side_live:
            try:
                next(side)
            except StopIteration as stop:
                side_val, side_live = stop.value, False
    return main_val, side_val


def _layer_norm(t, w, b, eps):
    mu = jnp.mean(t, axis=-1, keepdims=True)
    d = t - mu
    var = jnp.mean(d * d, axis=-1, keepdims=True)
    return d * lax.rsqrt(var + eps) * w + b


def _mod_kernel(c_ref, w_ref, b_ref, o_ref):
    o_ref[0] = _bdot(_silu(c_ref[...]), w_ref[0]) + b_ref[0]


def _mod_call(c_all, w_ada, b_ada):
    depth, d, n = w_ada.shape
    rows = c_all.shape[0]
    tn = 1536
    return pl.pallas_call(
        _mod_kernel,
        grid=(depth, n // tn),
        in_specs=[pl.BlockSpec((rows, d), lambda l, j: (0, 0)),
                  pl.BlockSpec((1, d, tn), lambda l, j: (l, 0, j)),
                  pl.BlockSpec((1, 1, tn), lambda l, j: (l, 0, j))],
        out_specs=pl.BlockSpec((1, rows, tn), lambda l, j: (l, 0, j)),
        out_shape=jax.ShapeDtypeStruct((depth, rows, n), F32),
        compiler_params=pltpu.CompilerParams(dimension_semantics=("parallel", "parallel"),
                                             vmem_limit_bytes=VMEM_LIMIT),
        name="adaln_mod",
    )(c_all, w_ada, b_ada.reshape(depth, 1, n))


def _inproj_kernel(x_ref, sc_ref, sh_ref, wg_ref, wr_ref, cw_ref, mu_ref, cst_ref, sst_ref,
                   og_ref, or_ref, craw_ref, sraw_ref, halo_g, halo_r):
    bb, tt, d = x_ref.shape

    @pl.when(pl.program_id(1) == 0)
    def _():
        halo_g[...] = cst_ref[...]
        halo_r[...] = sst_ref[...]

    h = x_ref[...] * (1.0 + sc_ref[...]) + sh_ref[...]
    hb = h.reshape(bb * tt, d).astype(BF16)
    conv_w = cw_ref[0]
    shift_mix = mu_ref[0]

    def conv_silu(raw, halo, w):
        blk = jnp.concatenate([halo, raw], axis=0)
        prev = pltpu.roll(blk, 1, axis=0)
        acc = (blk * w[3:4, :] + prev * w[2:3, :]
               + pltpu.roll(blk * w[1:2, :] + prev * w[0:1, :], 2, axis=0))
        return _silu(acc[HALO:, :])

    def token_shift(raw, halo, m):
        blk = jnp.concatenate([halo, raw], axis=0)
        return (blk + (pltpu.roll(blk, 1, axis=0) - blk) * m)[HALO:, :]

    rb_rows = min(bb * tt, INPROJ_ROWS)
    seg = min(tt, rb_rows)
    for r0 in range(0, bb * tt, rb_rows):
        hrows = hb[r0:r0 + rb_rows]
        pieces = [(s0,) + divmod(r0 + s0, tt) for s0 in range(0, rb_rows, seg)]
        for lo, hi in INPROJ_QKV_GROUPS:
            raw = jnp.dot(hrows, wg_ref[0, :, lo:hi], preferred_element_type=F32)
            for s0, b, t in pieces:
                og_ref[b, t:t + seg, lo:hi] = conv_silu(raw[s0:s0 + seg], halo_g[b, :, lo:hi], conv_w[:, lo:hi])
                halo_g[b, :, lo:hi] = raw[s0 + seg - HALO:s0 + seg, :]
        raw = jnp.dot(hrows, wg_ref[0, :, GDN_QKV:], preferred_element_type=F32)
        for s0, b, t in pieces:
            og_ref[b, t:t + seg, GDN_QKV:] = raw[s0:s0 + seg]
        for lo, hi in INPROJ_RW_GROUPS:
            raw = jnp.dot(hrows, wr_ref[0, :, lo:hi], preferred_element_type=F32)
            for s0, b, t in pieces:
                or_ref[b, t:t + seg, lo:hi] = token_shift(raw[s0:s0 + seg], halo_r[b, :, lo:hi], shift_mix[:, lo:hi])
                halo_r[b, :, lo:hi] = raw[s0 + seg - HALO:s0 + seg, :]
    craw_ref[...] = halo_g[...]
    sraw_ref[...] = halo_r[...]


def _row_blocks(bsz, seq):
    tt = min(seq, ROW_TILE)
    bb = max(1, min(bsz, ROW_TILE // tt))
    assert seq % tt == 0 and bsz % bb == 0 and tt % 8 == 0
    return bb, tt


def _layer_spec(arr, layer, single_buffer=False):
    kwargs = dict(pipeline_mode=pl.Buffered(1)) if single_buffer else {}
    return pl.BlockSpec((1,) + arr.shape[1:], lambda b, t: (layer,) + (0,) * (arr.ndim - 1), **kwargs)


def _inproj_call(x, sc, sh, wg, wr, conv_w, mu, conv_halo, shift_halo, layer):
    bsz, seq, d = x.shape
    bb, tt = _row_blocks(bsz, seq)
    assert tt >= HALO
    xmap = lambda b, t: (b, t, 0)
    mmap = lambda b, t: (b, 0, 0)
    return pl.pallas_call(
        _inproj_kernel,
        grid=(bsz // bb, seq // tt),
        in_specs=[pl.BlockSpec((bb, tt, d), xmap),
                  pl.BlockSpec((bb, 1, d), mmap),
                  pl.BlockSpec((bb, 1, d), mmap),
                  _layer_spec(wg, layer, single_buffer=True),
                  _layer_spec(wr, layer, single_buffer=True),
                  _layer_spec(conv_w, layer), _layer_spec(mu, layer),
                  pl.BlockSpec((bb, HALO, GDN_QKV), mmap),
                  pl.BlockSpec((bb, HALO, RW_PROJ), mmap)],
        out_specs=[pl.BlockSpec((bb, tt, GDN_COLS), xmap),
                   pl.BlockSpec((bb, tt, RW_PROJ), xmap),
                   pl.BlockSpec((bb, HALO, GDN_QKV), mmap),
                   pl.BlockSpec((bb, HALO, RW_PROJ), mmap)],
        out_shape=[jax.ShapeDtypeStruct((bsz, seq, GDN_COLS), F32),
                   jax.ShapeDtypeStruct((bsz, seq, RW_PROJ), F32),
                   jax.ShapeDtypeStruct((bsz, HALO, GDN_QKV), F32),
                   jax.ShapeDtypeStruct((bsz, HALO, RW_PROJ), F32)],
        scratch_shapes=[pltpu.VMEM((bb, HALO, GDN_QKV), F32),
                        pltpu.VMEM((bb, HALO, RW_PROJ), F32)],
        compiler_params=pltpu.CompilerParams(dimension_semantics=("parallel", "arbitrary"),
                                             vmem_limit_bytes=VMEM_LIMIT),
        name="in_proj",
    )(x, sc, sh, wg, wr, conv_w, mu, conv_halo, shift_halo)


def _gdn_kernel(pg_ref, s0_ref, gp_ref, nw_ref, og_ref, sout_ref,
                s_scr, qd_scr, ov_scr, m_scr, u_scr, dl_scr, *, tt, group, bb):
    c = CHUNK
    gc = group * c
    n_chunks = tt // c
    n_groups = n_chunks // group
    d = GDN_D
    t_idx = pl.program_id(1)

    @pl.when(t_idx == 0)
    def _():
        for b in range(bb):
            s_scr[b * GDN_HEADS:(b + 1) * GDN_HEADS] = s0_ref[b]

    row = lax.broadcasted_iota(jnp.int32, (c, 2 * c), 0)
    lane = lax.broadcasted_iota(jnp.int32, (c, 2 * c), 1)
    first = lane < c
    col = lane & (c - 1)
    causal = col <= row
    strict = col < row
    eye = jnp.where(col == row, 1.0, 0.0).astype(F32)
    grow = lax.broadcasted_iota(jnp.int32, (gc, gc), 0)
    gcol = lax.broadcasted_iota(jnp.int32, (gc, gc), 1)
    ltri = jnp.where(((grow & -c) == (gcol & -c)) & (gcol <= grow), 1.0, 0.0).astype(BF16)
    ones_d = jnp.ones((d, d), BF16)
    zero_2d = jnp.zeros((c, 2 * d), F32)
    neg_a = -jnp.exp(gp_ref[0, 0:1, :])
    dt_bias = gp_ref[0, 1:2, :]
    nw = nw_ref[0]
    heads = range(GDN_HEADS)

    def per_chunk_row(x, r):
        return jnp.concatenate([jnp.broadcast_to(x[j * c + r:j * c + r + 1, :], (c, x.shape[1]))
                                for j in range(group)], axis=0)

    def front(step):
        b, gi = divmod(step, n_groups)
        t0 = gi * gc
        qkv = pg_ref[b, t0:t0 + gc, 0:GDN_QKV]
        gates = pg_ref[b, t0:t0 + gc, GDN_QKV + GDN_WIDTH:GDN_COLS]
        log_a = neg_a * _softplus(gates + dt_bias)
        beta_all = _sigmoid(gates)
        g_all = _cumsum_rows(ltri, log_a)
        yield
        g_rows = g_all.T
        eg_all = jnp.exp(g_all)
        g_last = per_chunk_row(g_all, c - 1)
        dec_all = jnp.exp(g_last - g_all)
        for j in range(group):
            dl_scr[step * group + j] = jnp.broadcast_to(
                jnp.exp(g_all[(j + 1) * c - 1:(j + 1) * c, :]), (8, GATE_COLS))
        yield
        q = [qkv[:, h * d:(h + 1) * d] for h in heads]
        k = [qkv[:, GDN_WIDTH + h * d:GDN_WIDTH + (h + 1) * d] for h in heads]
        v = [qkv[:, 2 * GDN_WIDTH + h * d:2 * GDN_WIDTH + (h + 1) * d] for h in heads]
        ss = _bdot(jnp.concatenate([t * t for t in q + k], axis=0), ones_d)
        yield
        qn, kn = [], []
        for h in heads:
            qn.append(q[h] * lax.rsqrt(ss[h * gc:(h + 1) * gc] + NORM_EPS) * (d ** -0.5))
            kn.append(k[h] * lax.rsqrt(ss[(GDN_HEADS + h) * gc:(GDN_HEADS + h + 1) * gc] + NORM_EPS))
            yield
        items = []
        for j in range(group):
            rs = slice(j * c, (j + 1) * c)
            for p in range(GDN_HEADS // 2):
                pair = (2 * p, 2 * p + 1)
                rhs = []
                for h in pair:
                    beta = beta_all[rs, GDN_HEADS + h:GDN_HEADS + h + 1]
                    rhs.append(jnp.concatenate([v[h][rs] * beta, kn[h][rs] * (beta * eg_all[rs, h:h + 1])], axis=-1))
                items.append(dict(
                    j=j, pair=pair, rs=rs,
                    ks2=jnp.concatenate([kn[h][rs] for h in pair], axis=0).astype(BF16),
                    qs2=jnp.concatenate([qn[h][rs] for h in pair], axis=0).astype(BF16),
                    rhs_bd=jnp.concatenate([jnp.concatenate([rhs[0], zero_2d], axis=-1),
                                            jnp.concatenate([zero_2d, rhs[1]], axis=-1)], axis=0),
                    qdec=[qn[h][rs] * eg_all[rs, h:h + 1] for h in pair],
                    kdec=[kn[h][rs] * dec_all[rs, h:h + 1] for h in pair]))
            yield
        return dict(items=items, g_all=g_all, g_rows=g_rows, beta_all=beta_all)

    def back(step, fr):
        items, g_all, g_rows, beta_all = fr["items"], fr["g_all"], fr["g_rows"], fr["beta_all"]
        r0 = step * gc
        kk2s = [_bdot_nt(it["ks2"], it["ks2"]) for it in items]
        yield
        qk2s = [_bdot_nt(it["qs2"], it["ks2"]) for it in items]
        yield
        for it, kk2, qk2 in zip(items, kk2s, qk2s):
            (h1, h2), rs = it["pair"], it["rs"]
            kk_cat = jnp.where(first, kk2[0:c], kk2[c:2 * c])
            qk_cat = jnp.where(first, qk2[0:c], qk2[c:2 * c])
            g_col_cat = jnp.where(first, g_all[rs, h1:h1 + 1], g_all[rs, h2:h2 + 1])
            g_row_cat = jnp.concatenate([g_rows[h1:h1 + 1, rs], g_rows[h2:h2 + 1, rs]], axis=1)
            beta_cat = jnp.where(first, beta_all[rs, GDN_HEADS + h1:GDN_HEADS + h1 + 1],
                                 beta_all[rs, GDN_HEADS + h2:GDN_HEADS + h2 + 1])
            gam = jnp.where(causal, jnp.exp(jnp.where(causal, g_col_cat - g_row_cat, 0.0)), 0.0)
            it["a_cat"] = jnp.where(strict, beta_cat * kk_cat * gam, 0.0)
            it["qk_cat"] = jnp.where(causal, qk_cat * gam, 0.0)
        yield
        t_invs = yield from _pair_unit_lower_inverse([it["a_cat"] for it in items], first, eye, refine=True)
        sols = [_bdot(t_inv, it["rhs_bd"]) for it, t_inv in zip(items, t_invs)]
        yield
        corrs = [_bdot(it["qk_cat"], jnp.concatenate(
            [jnp.concatenate([sol[:, d:2 * d], sol[:, 0:d], zero_2d], axis=-1),
             jnp.concatenate([zero_2d, sol[:, 3 * d:4 * d], sol[:, 2 * d:3 * d]], axis=-1)], axis=0))
            for it, sol in zip(items, sols)]
        yield
        mus = [[_bdot_tn(it["kdec"][i], jnp.concatenate([sol[:, (2 * i + 1) * d:(2 * i + 2) * d],
                                                         sol[:, 2 * i * d:(2 * i + 1) * d]], axis=-1))
                for i in range(2)] for it, sol in zip(items, sols)]
        yield
        for it, corr, mu_pair in zip(items, corrs, mus):
            ci = step * group + it["j"]
            row = r0 + it["j"] * c
            for i, h in enumerate(it["pair"]):
                qd_scr[ci, h] = (it["qdec"][i] - corr[:, 2 * i * d:(2 * i + 1) * d]).astype(BF16)
                ov_scr[row:row + c, h * d:(h + 1) * d] = corr[:, (2 * i + 1) * d:(2 * i + 2) * d]
                m_scr[ci, h] = mu_pair[i][:, 0:d].astype(BF16)
                u_scr[ci, h] = mu_pair[i][:, d:2 * d]

    n_steps = bb * n_groups
    fr = _run(front(0))
    for step in range(n_steps):
        if step + 1 < n_steps:
            _, fr = _run_interleaved(back(step, fr), front(step + 1))
        else:
            _run(back(step, fr))

    def pass2(ci, carry):
        cis = [b * n_chunks + ci for b in range(bb)]
        rows = [pl.ds(pl.multiple_of(b * tt + ci * c, c), c) for b in range(bb)]
        dls = [dl_scr[cb][0:1, :] for cb in cis]
        s_old = [[s_scr[b * GDN_HEADS + h] for h in heads] for b in range(bb)]
        s_bf = [[s.astype(BF16) for s in sb] for sb in s_old]
        o = [[jnp.dot(qd_scr[cis[b], h], s_bf[b][h], preferred_element_type=F32) for h in heads]
             for b in range(bb)]
        ms = [[jnp.dot(m_scr[cis[b], h], s_bf[b][h], preferred_element_type=F32) for h in heads]
              for b in range(bb)]
        for b in range(bb):
            ov_scr[rows[b], :] = jnp.concatenate(o[b], axis=-1) + ov_scr[rows[b], :]
        s_scr[...] = jnp.stack([s_old[b][h] * dls[b][:, h:h + 1] - ms[b][h] + u_scr[cis[b], h]
                                for b in range(bb) for h in heads])
        return carry

    lax.fori_loop(0, n_chunks, pass2, 0)

    def pass3(step, carry):
        b = step // n_groups
        t0 = pl.multiple_of((step - b * n_groups) * gc, gc)
        r0 = pl.multiple_of(step * gc, gc)
        o = ov_scr[pl.ds(r0, gc), :]
        o_all = jnp.concatenate([o[:, h * d:(h + 1) * d] for h in heads], axis=0)
        ms = _bdot(o_all * o_all, ones_d) * (1.0 / d)
        o_all = o_all * lax.rsqrt(ms + NORM_EPS) * nw
        z = pg_ref[b, pl.ds(t0, gc), GDN_QKV:GDN_QKV + GDN_WIDTH]
        og_ref[b, pl.ds(t0, gc), :] = jnp.concatenate(
            [o_all[h * gc:(h + 1) * gc] for h in heads], axis=-1) * _silu(z)
        return carry

    lax.fori_loop(0, bb * n_groups, pass3, 0)

    @pl.when(t_idx == pl.num_programs(1) - 1)
    def _():
        for b in range(bb):
            sout_ref[b] = s_scr[b * GDN_HEADS:(b + 1) * GDN_HEADS]


def _seq_block(bsz, tt, rows):
    bb = max(1, min(bsz, rows // tt))
    return bb if bsz % bb == 0 else 1


def _gdn_call(pg, s0, gate_params, norm_w, layer):
    bsz, seq, _ = pg.shape
    tt = min(seq, TIME_TILE)
    n_chunks = tt // CHUNK
    group = GDN_GROUP if n_chunks % GDN_GROUP == 0 else 1
    bb = _seq_block(bsz, tt, SEQ_BLOCK * TIME_TILE)
    assert seq % tt == 0 and tt % CHUNK == 0
    smap = lambda b, t: (b, 0, 0, 0)
    return pl.pallas_call(
        functools.partial(_gdn_kernel, tt=tt, group=group, bb=bb),
        grid=(bsz // bb, seq // tt),
        in_specs=[pl.BlockSpec((bb, tt, GDN_COLS), lambda b, t: (b, t, 0)),
                  pl.BlockSpec((bb, GDN_HEADS, GDN_D, GDN_D), smap),
                  _layer_spec(gate_params, layer), _layer_spec(norm_w, layer)],
        out_specs=[pl.BlockSpec((bb, tt, GDN_WIDTH), lambda b, t: (b, t, 0)),
                   pl.BlockSpec((bb, GDN_HEADS, GDN_D, GDN_D), smap)],
        out_shape=[jax.ShapeDtypeStruct((bsz, seq, GDN_WIDTH), F32),
                   jax.ShapeDtypeStruct((bsz, GDN_HEADS, GDN_D, GDN_D), F32)],
        scratch_shapes=[pltpu.VMEM((bb * GDN_HEADS, GDN_D, GDN_D), F32),
                        pltpu.VMEM((bb * n_chunks, GDN_HEADS, CHUNK, GDN_D), BF16),
                        pltpu.VMEM((bb * tt, GDN_WIDTH), F32),
                        pltpu.VMEM((bb * n_chunks, GDN_HEADS, GDN_D, GDN_D), BF16),
                        pltpu.VMEM((bb * n_chunks, GDN_HEADS, GDN_D, GDN_D), F32),
                        pltpu.VMEM((bb * n_chunks, 8, GATE_COLS), F32)],
        compiler_params=pltpu.CompilerParams(dimension_semantics=("parallel", "arbitrary"),
                                             vmem_limit_bytes=VMEM_LIMIT),
        name="gated_deltanet",
    )(pg, s0, gate_params, norm_w)


def _rwkv_kernel(pr_ref, s0_ref, pv_ref, w2_ref, a2_ref, g2_ref,
                 or_ref, sout_ref, s_scr, rq_scr, ov_scr, m_scr, u_scr, dec_scr, gate_scr, bonus_scr,
                 *, tt, group, bb):
    c = CHUNK
    gc = group * c
    lanes = 2 * RW_HD
    n_chunks = tt // c
    n_groups = n_chunks // group
    t_idx = pl.program_id(1)

    @pl.when(t_idx == 0)
    def _():
        for b in range(bb):
            s_scr[b * RW_PAIRS:(b + 1) * RW_PAIRS] = s0_ref[b]

    grow = lax.broadcasted_iota(jnp.int32, (gc, gc), 0)
    gcol = lax.broadcasted_iota(jnp.int32, (gc, gc), 1)
    same_chunk = (grow & -c) == (gcol & -c)
    ltri = jnp.where(same_chunk & (gcol <= grow), 1.0, 0.0).astype(BF16)
    prow = lax.broadcasted_iota(jnp.int32, (c, lanes), 0)
    plane = lax.broadcasted_iota(jnp.int32, (c, lanes), 1)
    first = plane < RW_HD
    pcol = plane & (RW_HD - 1)
    tri_strict = pcol < prow
    tri_incl = pcol <= prow
    eye = jnp.where(pcol == prow, 1.0, 0.0).astype(F32)
    srow = lax.broadcasted_iota(jnp.int32, (lanes, lanes), 0)
    scol = lax.broadcasted_iota(jnp.int32, (lanes, lanes), 1)
    same_head = (srow < RW_HD) == (scol < RW_HD)
    seg_ones = jnp.where(same_head, 1.0, 0.0).astype(BF16)
    w0 = pv_ref[0, 0:1, :]
    a0 = pv_ref[0, 1:2, :]
    kk_p = pv_ref[0, 2:3, :]
    ka_p = pv_ref[0, 3:4, :]
    rk_p = pv_ref[0, 4:5, :]
    ln_w = pv_ref[0, 5:6, :]
    ln_b = pv_ref[0, 6:7, :]
    pairs = range(RW_PAIRS)
    sls = [slice(p * lanes, (p + 1) * lanes) for p in pairs]

    def per_chunk_row(x, row):
        return jnp.concatenate([jnp.broadcast_to(x[j * c + row:j * c + row + 1, :], (c, x.shape[1]))
                                for j in range(group)], axis=0)

    def front(step):
        b, gi = divmod(step, n_groups)
        r0 = step * gc
        xs = pr_ref[b, gi * gc:(gi + 1) * gc, :]
        r = xs[:, 0:RW_WIDTH]
        kr = xs[:, RW_WIDTH:2 * RW_WIDTH]
        vr = xs[:, 2 * RW_WIDTH:3 * RW_WIDTH]
        lora_in = xs[:, 3 * RW_WIDTH:3 * RW_WIDTH + RW_LORA]
        xg = xs[:, 3 * RW_WIDTH + RW_LORA:RW_PROJ]
        w_log = -_softplus(-(w0 + _bdot(jnp.tanh(lora_in), w2_ref[0]))) - 0.5
        log_w = -jnp.exp(w_log)
        yield
        a = _sigmoid(a0 + _bdot(lora_in, a2_ref[0]))
        gate_scr[r0:r0 + gc, :] = _bdot(_sigmoid(xg), g2_ref[0])
        kk_raw = kr * kk_p
        k2 = kr * (1.0 + (a - 1.0) * ka_p)
        rk = r * k2 * rk_p
        yield
        g_cum = _cumsum_rows(ltri, log_w)
        mid_rows = [g_cum[j * c + c // 2 - 1:j * c + c // 2, :] for j in range(group)]
        end_rows = [g_cum[(j + 1) * c - 1:(j + 1) * c, :] for j in range(group)]
        g_mid = per_chunk_row(g_cum, c // 2 - 1)
        yield
        e_left = jnp.exp(g_cum - g_mid)
        yield
        e_left_prev = jnp.exp(g_cum - log_w - g_mid)
        yield
        e_right = jnp.exp(g_mid - g_cum)
        e_mid = [jnp.exp(m) for m in mid_rows]
        e_end_mid = [jnp.exp(e - m) for e, m in zip(end_rows, mid_rows)]
        for j in range(group):
            dec_scr[step * group + j] = jnp.broadcast_to(jnp.exp(end_rows[j]), (8, RW_WIDTH))
        yield
        seg = _bdot(jnp.concatenate([kk_raw[:, sl] * kk_raw[:, sl] for sl in sls]
                                    + [rk[:, sl] for sl in sls], axis=0), seg_ones)
        bonus_scr[r0:r0 + gc, :] = jnp.concatenate(
            [seg[(RW_PAIRS + p) * gc:(RW_PAIRS + p + 1) * gc] for p in pairs], axis=-1) * vr
        yield
        items = []
        for j in range(group):
            rs = slice(j * c, (j + 1) * c)
            for p, sl in enumerate(sls):
                kappa = kk_raw[rs, sl] * lax.rsqrt(seg[p * gc + j * c:p * gc + (j + 1) * c] + NORM_EPS)
                r_l = r[rs, sl] * e_left[rs, sl]
                kap_l = kappa * e_left_prev[rs, sl]
                k_r = k2[rs, sl] * e_right[rs, sl]
                b_r = kappa * a[rs, sl] * e_right[rs, sl]
                items.append(dict(
                    j=j, p=p, v=vr[rs, sl], kg=kap_l * e_mid[j][:, sl], rg=r_l * e_mid[j][:, sl],
                    k_d=k_r * e_end_mid[j][:, sl], b_d=b_r * e_end_mid[j][:, sl],
                    lhs=jnp.concatenate([kap_l, r_l], axis=0),
                    rhs=jnp.concatenate([jnp.where(first, b_r, 0.0), jnp.where(first, k_r, 0.0),
                                         jnp.where(first, 0.0, k_r), jnp.where(first, 0.0, b_r)], axis=0)))
            yield
        return items

    def back(step, items):
        r0 = step * gc
        prods = [_bdot_nt(it["lhs"], it["rhs"]) for it in items]
        yield
        for it, res in zip(items, prods):
            kap_1, kap_2 = res[0:c, 0:lanes], res[0:c, lanes:2 * lanes]
            r_1, r_2 = res[c:2 * c, 0:lanes], res[c:2 * c, lanes:2 * lanes]
            it["a_cat"] = jnp.where(tri_strict, jnp.where(first, kap_1, kap_2), 0.0)
            kb_cat = jnp.where(tri_strict, jnp.where(first, kap_2, kap_1), 0.0)
            qk_cat = jnp.where(tri_incl, jnp.where(first, r_2, r_1), 0.0)
            it["kq"] = jnp.concatenate([kb_cat, qk_cat], axis=0)
            it["qb_cat"] = jnp.where(tri_incl, jnp.where(first, r_1, r_2), 0.0)
            v = it["v"]
            it["v_swap"] = jnp.concatenate([jnp.where(first, 0.0, v), jnp.where(first, v, 0.0)], axis=0)
        intras = [_bdot(it["kq"], it["v_swap"]) for it in items]
        yield
        t_invs = yield from _pair_unit_lower_inverse([it["a_cat"] for it in items], first, eye, refine=False)
        sols = [_bdot(t_inv, jnp.concatenate([_pair_blockdiag(it["kg"], first),
                                              _pair_blockdiag(intra[0:c], first)], axis=-1))
                for it, t_inv, intra in zip(items, t_invs, intras)]
        yield
        corrs = [_bdot(it["qb_cat"], jnp.concatenate([_pair_blockdiag(sol[:, 0:lanes], first),
                                                      _pair_blockdiag(sol[:, lanes:2 * lanes], first)], axis=-1))
                 for it, sol in zip(items, sols)]
        yield
        us = [_bdot_tn(jnp.concatenate([it["v"], -sol[:, lanes:2 * lanes]], axis=0),
                       jnp.concatenate([it["k_d"], it["b_d"]], axis=0)) for it, sol in zip(items, sols)]
        yield
        ms = [_bdot_tn(sol[:, 0:lanes], it["b_d"]) for it, sol in zip(items, sols)]
        yield
        for item, intra, corr, u, m in zip(items, intras, corrs, us, ms):
            ci = step * group + item["j"]
            p = item["p"]
            rq_scr[ci, p] = (item["rg"] - corr[:, 0:lanes]).astype(BF16)
            row = r0 + item["j"] * c
            ov_scr[row:row + c, sls[p]] = intra[c:2 * c] - corr[:, lanes:2 * lanes]
            u_scr[ci, p] = jnp.where(same_head, u, 0.0)
            m_scr[ci, p] = jnp.where(same_head, m, 0.0).astype(BF16)

    n_steps = bb * n_groups
    items = _run(front(0))
    for step in range(n_steps):
        if step + 1 < n_steps:
            _, items = _run_interleaved(back(step, items), front(step + 1))
        else:
            _run(back(step, items))

    def pass2(ci, carry):
        cis = [b * n_chunks + ci for b in range(bb)]
        rows = [pl.ds(pl.multiple_of(b * tt + ci * c, c), c) for b in range(bb)]
        decs = [dec_scr[cb][0:1, :] for cb in cis]
        s_old = [[s_scr[b * RW_PAIRS + p] for p in pairs] for b in range(bb)]
        s_bf = [[s.astype(BF16) for s in sb] for sb in s_old]
        o = [[_bdot_nt(rq_scr[cis[b], p], s_bf[b][p]) for p in pairs] for b in range(bb)]
        sm = [[jnp.dot(s_bf[b][p], m_scr[cis[b], p], preferred_element_type=F32) for p in pairs]
              for b in range(bb)]
        for b in range(bb):
            ov_scr[rows[b], :] = jnp.concatenate(o[b], axis=-1) + ov_scr[rows[b], :]
        s_scr[...] = jnp.stack([s_old[b][p] * decs[b][:, sls[p]] - sm[b][p] + u_scr[cis[b], p]
                                for b in range(bb) for p in pairs])
        return carry

    lax.fori_loop(0, n_chunks, pass2, 0)

    def pass3(step, carry):
        b = step // n_groups
        gi = step - b * n_groups
        r0 = pl.multiple_of(step * gc, gc)
        o = ov_scr[pl.ds(r0, gc), :]
        o_all = jnp.concatenate([o[:, sl] for sl in sls], axis=0)
        mean = _bdot(o_all, seg_ones) * (1.0 / RW_HD)
        dev = o_all - mean
        var = _bdot(dev * dev, seg_ones) * (1.0 / RW_HD)
        o_n = dev * lax.rsqrt(var + GN_EPS)
        o_n = jnp.concatenate([o_n[p * gc:(p + 1) * gc] for p in pairs], axis=-1) * ln_w + ln_b
        or_ref[b, pl.ds(pl.multiple_of(gi * gc, gc), gc), :] = (
            (o_n + bonus_scr[pl.ds(r0, gc), :]) * gate_scr[pl.ds(r0, gc), :])
        return carry

    lax.fori_loop(0, bb * n_groups, pass3, 0)

    @pl.when(t_idx == pl.num_programs(1) - 1)
    def _():
        for b in range(bb):
            sout_ref[b] = s_scr[b * RW_PAIRS:(b + 1) * RW_PAIRS]


def _rwkv_call(pr, s0, pvec, w2, a2, g2, layer):
    bsz, seq, _ = pr.shape
    tt = min(seq, RW_TIME_TILE)
    n_chunks = tt // CHUNK
    group = RW_GROUP if n_chunks % RW_GROUP == 0 else 1
    bb = _seq_block(bsz, tt, RW_SEQ_BLOCK * RW_TIME_TILE)
    assert seq % tt == 0 and tt % CHUNK == 0
    lanes = 2 * RW_HD
    smap = lambda b, t: (b, 0, 0, 0)
    return pl.pallas_call(
        functools.partial(_rwkv_kernel, tt=tt, group=group, bb=bb),
        grid=(bsz // bb, seq // tt),
        in_specs=[pl.BlockSpec((bb, tt, RW_PROJ), lambda b, t: (b, t, 0)),
                  pl.BlockSpec((bb, RW_PAIRS, lanes, lanes), smap),
                  _layer_spec(pvec, layer),
                  _layer_spec(w2, layer), _layer_spec(a2, layer), _layer_spec(g2, layer)],
        out_specs=[pl.BlockSpec((bb, tt, RW_WIDTH), lambda b, t: (b, t, 0)),
                   pl.BlockSpec((bb, RW_PAIRS, lanes, lanes), smap)],
        out_shape=[jax.ShapeDtypeStruct((bsz, seq, RW_WIDTH), F32),
                   jax.ShapeDtypeStruct((bsz, RW_PAIRS, lanes, lanes), F32)],
        scratch_shapes=[pltpu.VMEM((bb * RW_PAIRS, lanes, lanes), F32),
                        pltpu.VMEM((bb * n_chunks, RW_PAIRS, CHUNK, lanes), BF16),
                        pltpu.VMEM((bb * tt, RW_WIDTH), F32),
                        pltpu.VMEM((bb * n_chunks, RW_PAIRS, lanes, lanes), BF16),
                        pltpu.VMEM((bb * n_chunks, RW_PAIRS, lanes, lanes), F32),
                        pltpu.VMEM((bb * n_chunks, 8, RW_WIDTH), F32),
                        pltpu.VMEM((bb * tt, RW_WIDTH), F32),
                        pltpu.VMEM((bb * tt, RW_WIDTH), F32)],
        compiler_params=pltpu.CompilerParams(dimension_semantics=("parallel", "arbitrary"),
                                             vmem_limit_bytes=VMEM_LIMIT),
        name="rwkv7",
    )(pr, s0, pvec, w2, a2, g2)


def _outffn_kernel(og_ref, or_ref, x_ref, g1_ref, sc_ref, sh_ref, g2_ref, wo_ref, lnp_ref,
                   w1_ref, w2_ref, o_ref, *, alpha):
    bb, tt, d = x_ref.shape
    rows = bb * tt
    y = (jnp.dot(og_ref[...].reshape(rows, GDN_WIDTH).astype(BF16), wo_ref[0, 0:GDN_WIDTH, :],
                 preferred_element_type=F32)
         + jnp.dot(or_ref[...].reshape(rows, RW_WIDTH).astype(BF16), wo_ref[0, GDN_WIDTH:, :],
                   preferred_element_type=F32)).reshape(bb, tt, d)
    x1 = _layer_norm(alpha * x_ref[...] + g1_ref[...] * y, lnp_ref[0, 0:1, :], lnp_ref[0, 1:2, :], LN_EPS)
    hb = (x1 * (1.0 + sc_ref[...]) + sh_ref[...]).reshape(rows, d).astype(BF16)
    f = jnp.zeros((rows, d), F32)
    for j in range(D_FF // FF_TILE):
        hid = jnp.dot(hb, w1_ref[0, :, j * FF_TILE:(j + 1) * FF_TILE], preferred_element_type=F32)
        hid = jnp.square(jnp.maximum(hid, 0.0)).astype(BF16)
        f = f + jnp.dot(hid, w2_ref[0, j * FF_TILE:(j + 1) * FF_TILE, :], preferred_element_type=F32)
    o_ref[...] = _layer_norm(alpha * x1 + g2_ref[...] * f.reshape(bb, tt, d),
                             lnp_ref[0, 2:3, :], lnp_ref[0, 3:4, :], LN_EPS)


def _outffn_call(og, orr, x, g1, sc2, sh2, g2, wo, lnp, w1, w2, alpha, layer):
    bsz, seq, d = x.shape
    bb, tt = _row_blocks(bsz, seq)
    xmap = lambda b, t: (b, t, 0)
    mmap = lambda b, t: (b, 0, 0)
    mod_spec = pl.BlockSpec((bb, 1, d), mmap)
    return pl.pallas_call(
        functools.partial(_outffn_kernel, alpha=alpha),
        grid=(bsz // bb, seq // tt),
        in_specs=[pl.BlockSpec((bb, tt, GDN_WIDTH), xmap),
                  pl.BlockSpec((bb, tt, RW_WIDTH), xmap),
                  pl.BlockSpec((bb, tt, d), xmap),
                  mod_spec, mod_spec, mod_spec, mod_spec,
                  _layer_spec(wo, layer, single_buffer=True), _layer_spec(lnp, layer),
                  _layer_spec(w1, layer, single_buffer=True), _layer_spec(w2, layer, single_buffer=True)],
        out_specs=pl.BlockSpec((bb, tt, d), xmap),
        out_shape=jax.ShapeDtypeStruct((bsz, seq, d), F32),
        compiler_params=pltpu.CompilerParams(dimension_semantics=("parallel", "parallel"),
                                             vmem_limit_bytes=VMEM_LIMIT),
        name="out_ffn",
    )(og, orr, x, g1, sc2, sh2, g2, wo, lnp, w1, w2)


def _prep_params(w_in, gdn_conv_w, gdn_a_log, gdn_dt_bias, gdn_norm_w, rwkv_mu, rwkv_w0, rwkv_w2,
                 rwkv_a0, rwkv_a2, rwkv_g2, rwkv_kk, rwkv_ka, rwkv_rk, rwkv_ln_w, rwkv_ln_b, w_out,
                 ln1_w, ln1_b, w_ff1, w_ff2, ln2_w, ln2_b):
    depth, d, _ = w_in.shape
    n_gdn = GDN_QKV + GDN_WIDTH + 2 * GDN_HEADS
    w_in = w_in.astype(BF16)
    wg = jnp.concatenate([w_in[:, :, :n_gdn], jnp.zeros((depth, d, GDN_COLS - n_gdn), BF16)], axis=2)
    wr = w_in[:, :, n_gdn:]
    gate_params = jnp.zeros((depth, 8, GATE_COLS), F32)
    gate_params = gate_params.at[:, 0, :GDN_HEADS].set(gdn_a_log).at[:, 1, :GDN_HEADS].set(gdn_dt_bias)
    pvec = jnp.stack([rwkv_w0, rwkv_a0, rwkv_kk, rwkv_ka, rwkv_rk, rwkv_ln_w, rwkv_ln_b,
                      jnp.zeros_like(rwkv_w0)], axis=1)
    zeros = jnp.zeros((depth, RW_LORA // 2, RW_WIDTH), F32)
    w2 = jnp.concatenate([rwkv_w2, zeros], axis=1).astype(BF16)
    a2 = jnp.concatenate([zeros, rwkv_a2], axis=1).astype(BF16)
    zero_row = jnp.zeros_like(ln1_w)
    lnp = jnp.stack([ln1_w, ln1_b, ln2_w, ln2_b] + [zero_row] * 4, axis=1)
    return dict(wg=wg, wr=wr, conv_w=gdn_conv_w, gate_params=gate_params,
                norm_w=gdn_norm_w[:, None, :], mu=rwkv_mu[:, None, :], pvec=pvec,
                w2=w2, a2=a2, g2=rwkv_g2.astype(BF16), wo=w_out.astype(BF16), lnp=lnp,
                w1=w_ff1.astype(BF16), w2f=w_ff2.astype(BF16))


def _rwkv_state_to_pairs(s):
    bsz = s.shape[0]
    s = s.reshape(bsz, RW_PAIRS, 2, RW_HD, RW_HD)
    z = jnp.zeros_like(s[:, :, 0])
    top = jnp.concatenate([s[:, :, 0], z], axis=-1)
    bot = jnp.concatenate([z, s[:, :, 1]], axis=-1)
    return jnp.concatenate([top, bot], axis=-2)


def _rwkv_state_from_pairs(sp):
    bsz = sp.shape[0]
    return jnp.stack([sp[:, :, :RW_HD, :RW_HD], sp[:, :, RW_HD:, RW_HD:]], axis=2).reshape(
        bsz, RW_HEADS, RW_HD, RW_HD)


def _trunk(x, mods, st_gdn, st_conv, st_rw, st_shift, p, alpha):
    bsz, seq, d = x.shape
    assert seq >= HALO
    outs = ([], [], [], [])
    for l in range(mods.shape[0]):
        sh1, sc1, g1, sh2, sc2, g2 = [mods[l, :, None, i * d:(i + 1) * d] for i in range(6)]
        conv_halo = jnp.pad(st_conv[l], ((0, 0), (HALO - (CONV_W - 1), 0), (0, 0)))
        shift_halo = jnp.pad(st_shift[l][:, None, :], ((0, 0), (HALO - 1, 0), (0, 0)))
        pg, pr, conv_raw, shift_raw = _inproj_call(x, sc1, sh1, p["wg"], p["wr"], p["conv_w"], p["mu"],
                                                   conv_halo, shift_halo, l)
        og, n_gdn = _gdn_call(pg, st_gdn[l], p["gate_params"], p["norm_w"], l)
        orr, n_rw = _rwkv_call(pr, _rwkv_state_to_pairs(st_rw[l]), p["pvec"], p["w2"], p["a2"], p["g2"], l)
        outs[0].append(n_gdn)
        outs[1].append(conv_raw[:, HALO - (CONV_W - 1):, :])
        outs[2].append(_rwkv_state_from_pairs(n_rw))
        outs[3].append(shift_raw[:, HALO - 1, :])
        x = _outffn_call(og, orr, x, g1, sc2, sh2, g2, p["wo"], p["lnp"], p["w1"], p["w2f"], alpha, l)
    return (x,) + tuple(jnp.stack(o) for o in outs)


def kernel(x_prompt, x_sample, c_prompt, c_sample, state_gdn, state_gdn_conv, state_rwkv, state_rwkv_shift, w_ada, b_ada, w_in, gdn_conv_w, gdn_a_log, gdn_dt_bias, gdn_norm_w, rwkv_mu, rwkv_w0, rwkv_w2, rwkv_a0, rwkv_a2, rwkv_g2, rwkv_kk, rwkv_ka, rwkv_rk, rwkv_ln_w, rwkv_ln_b, w_out, ln1_w, ln1_b, w_ff1, w_ff2, ln2_w, ln2_b):
    depth = w_in.shape[0]
    alpha = (2 * depth) ** DEPTH_ALPHA_POW
    bp = x_prompt.shape[0]
    params = _prep_params(w_in, gdn_conv_w, gdn_a_log, gdn_dt_bias, gdn_norm_w, rwkv_mu, rwkv_w0, rwkv_w2,
                          rwkv_a0, rwkv_a2, rwkv_g2, rwkv_kk, rwkv_ka, rwkv_rk, rwkv_ln_w, rwkv_ln_b,
                          w_out, ln1_w, ln1_b, w_ff1, w_ff2, ln2_w, ln2_b)
    mods = _mod_call(jnp.concatenate([c_prompt, c_sample], axis=0), w_ada, b_ada)
    zeros = lambda *shape: jnp.zeros((depth, bp) + shape, F32)
    y_p, p_gdn, p_conv, p_rw, p_shift = _trunk(
        x_prompt, mods[:, :bp],
        zeros(GDN_HEADS, GDN_D, GDN_D), zeros(CONV_W - 1, GDN_QKV),
        zeros(RW_HEADS, RW_HD, RW_HD), zeros(RW_PROJ), params, alpha)
    y_s, s_gdn, s_conv, s_rw, s_shift = _trunk(
        x_sample, mods[:, bp:], state_gdn, state_gdn_conv, state_rwkv, state_rwkv_shift, params, alpha)
    return (y_p, y_s, p_gdn, p_conv, p_rw, p_shift, s_gdn, s_conv, s_rw, s_shift)
```

```python
import functools

import jax
import jax.numpy as jnp
from jax import lax
from jax.experimental import pallas as pl
from jax.experimental.pallas import tpu as pltpu

F32 = jnp.float32
BF16 = jnp.bfloat16

D_MODEL = 1024
CHUNK = 64
GDN_HEADS = 4
GDN_D = 128
GDN_WIDTH = GDN_HEADS * GDN_D
GDN_QKV = 3 * GDN_WIDTH
CONV_W = 4
GATE_COLS = 128
GDN_COLS = GDN_QKV + GDN_WIDTH + GATE_COLS
RW_HEADS = 8
RW_HD = 64
RW_WIDTH = RW_HEADS * RW_HD
RW_PAIRS = RW_HEADS // 2
RW_LORA = 128
RW_PROJ = 3 * RW_WIDTH + 2 * RW_LORA
D_FF = 4 * D_MODEL
DEPTH_ALPHA_POW = 0.25
LN_EPS = 1e-5
GN_EPS = 64e-5
NORM_EPS = 1e-6
HALO = 8
ROW_TILE = 512
TIME_TILE = 512
FF_TILE = 1024
GDN_GROUP = 8
RW_GROUP = 4
RW_TIME_TILE = 256
RW_SEQ_BLOCK = 4
SEQ_BLOCK = 2
INPROJ_QKV_GROUPS = tuple((i, i + 512) for i in range(0, 1536, 512))
INPROJ_RW_GROUPS = tuple((i, i + 512) for i in range(0, 1536, 512)) + ((1536, 1792),)
INPROJ_ROWS = 128
VMEM_LIMIT = 56 * 1024 * 1024


def _bdot(a, b):
    return jnp.dot(a.astype(BF16), b.astype(BF16), preferred_element_type=F32)


def _bdot_nt(a, b):
    return lax.dot_general(a.astype(BF16), b.astype(BF16), (((1,), (1,)), ((), ())),
                           preferred_element_type=F32)


def _bdot_tn(a, b):
    return lax.dot_general(a.astype(BF16), b.astype(BF16), (((0,), (0,)), ((), ())),
                           preferred_element_type=F32)


def _split2(x):
    hi = x.astype(BF16)
    return hi, (x - hi.astype(F32)).astype(BF16)


def _dot3(a, b):
    ah, al = _split2(a)
    bh, bl = _split2(b)
    return (jnp.dot(ah, bh, preferred_element_type=F32) + jnp.dot(ah, bl, preferred_element_type=F32)
            + jnp.dot(al, bh, preferred_element_type=F32))


def _cumsum_rows(ltri_bf16, x):
    hi = x.astype(BF16)
    rest = x - hi.astype(F32)
    mid = rest.astype(BF16)
    lo = (rest - mid.astype(F32)).astype(BF16)
    return (jnp.dot(ltri_bf16, hi, preferred_element_type=F32)
            + jnp.dot(ltri_bf16, mid, preferred_element_type=F32)
            + jnp.dot(ltri_bf16, lo, preferred_element_type=F32))


def _sigmoid(x):
    return 0.5 * jnp.tanh(0.5 * x) + 0.5


def _silu(x):
    return x * _sigmoid(x)


def _softplus(x):
    return jnp.maximum(x, 0.0) + jnp.log(1.0 + jnp.exp(-jnp.abs(x)))


def _pair_blockdiag(x, first):
    zero = jnp.zeros_like(x)
    return jnp.concatenate([jnp.where(first, x, zero), jnp.where(first, zero, x)], axis=0)


def _pair_unit_lower_inverse(a_list, first, eye, refine):
    n = eye.shape[0]
    xs = [eye - a for a in a_list]
    ps = list(a_list)
    bds = [_pair_blockdiag(p.astype(BF16), first) for p in ps]
    terms = 2
    while terms < n:
        ps = [jnp.dot(p.astype(BF16), bd, preferred_element_type=F32) for p, bd in zip(ps, bds)]
        yield
        bds = [_pair_blockdiag(p.astype(BF16), first) for p in ps]
        xs = [x + jnp.dot(x.astype(BF16), bd, preferred_element_type=F32) for x, bd in zip(xs, bds)]
        yield
        terms *= 2
    if not refine:
        return xs
    es = [_dot3(eye + a, _pair_blockdiag(x, first)) - eye for a, x in zip(a_list, xs)]
    yield
    return [x - _bdot(x, _pair_blockdiag(e, first)) for x, e in zip(xs, es)]


def _run(gen):
    while True:
        try:
            next(gen)
        except StopIteration as stop:
            return stop.value


def _run_interleaved(main, side):
    main_val = side_val = None
    main_live = side_live = True
    while main_live or side_live:
        if main_live:
            try:
                next(main)
            except StopIteration as stop:
                main_val, main_live = stop.value, False
        if side_live:
            try:
                next(side)
            except StopIteration as stop:
                side_val, side_live = stop.value, False
    return main_val, side_val


def _layer_norm(t, w, b, eps):
    mu = jnp.mean(t, axis=-1, keepdims=True)
    d = t - mu
    var = jnp.mean(d * d, axis=-1, keepdims=True)
    return d * lax.rsqrt(var + eps) * w + b


def _mod_kernel(c_ref, w_ref, b_ref, o_ref):
    o_ref[0] = _bdot(_silu(c_ref[...]), w_ref[0]) + b_ref[0]


def _mod_call(c_all, w_ada, b_ada):
    depth, d, n = w_ada.shape
    rows = c_all.shape[0]
    tn = 1536
    return pl.pallas_call(
        _mod_kernel,
        grid=(depth, n // tn),
        in_specs=[pl.BlockSpec((rows, d), lambda l, j: (0, 0)),
                  pl.BlockSpec((1, d, tn), lambda l, j: (l, 0, j)),
                  pl.BlockSpec((1, 1, tn), lambda l, j: (l, 0, j))],
        out_specs=pl.BlockSpec((1, rows, tn), lambda l, j: (l, 0, j)),
        out_shape=jax.ShapeDtypeStruct((depth, rows, n), F32),
        compiler_params=pltpu.CompilerParams(dimension_semantics=("parallel", "parallel"),
                                             vmem_limit_bytes=VMEM_LIMIT),
        name="adaln_mod",
    )(c_all, w_ada, b_ada.reshape(depth, 1, n))


def _inproj_kernel(x_ref, sc_ref, sh_ref, wg_ref, wr_ref, cw_ref, mu_ref, cst_ref, sst_ref,
                   og_ref, or_ref, craw_ref, sraw_ref, halo_g, halo_r):
    bb, tt, d = x_ref.shape

    @pl.when(pl.program_id(1) == 0)
    def _():
        halo_g[...] = cst_ref[...]
        halo_r[...] = sst_ref[...]

    h = x_ref[...] * (1.0 + sc_ref[...]) + sh_ref[...]
    hb = h.reshape(bb * tt, d).astype(BF16)
    conv_w = cw_ref[0]
    shift_mix = mu_ref[0]

    def conv_silu(raw, halo, w):
        blk = jnp.concatenate([halo, raw], axis=0)
        prev = pltpu.roll(blk, 1, axis=0)
        acc = (blk * w[3:4, :] + prev * w[2:3, :]
               + pltpu.roll(blk * w[1:2, :] + prev * w[0:1, :], 2, axis=0))
        return _silu(acc[HALO:, :])

    def token_shift(raw, halo, m):
        blk = jnp.concatenate([halo, raw], axis=0)
        return (blk + (pltpu.roll(blk, 1, axis=0) - blk) * m)[HALO:, :]

    rb_rows = min(bb * tt, INPROJ_ROWS)
    seg = min(tt, rb_rows)
    for r0 in range(0, bb * tt, rb_rows):
        hrows = hb[r0:r0 + rb_rows]
        pieces = [(s0,) + divmod(r0 + s0, tt) for s0 in range(0, rb_rows, seg)]
        for lo, hi in INPROJ_QKV_GROUPS:
            raw = jnp.dot(hrows, wg_ref[0, :, lo:hi], preferred_element_type=F32)
            for s0, b, t in pieces:
                og_ref[b, t:t + seg, lo:hi] = conv_silu(raw[s0:s0 + seg], halo_g[b, :, lo:hi], conv_w[:, lo:hi])
                halo_g[b, :, lo:hi] = raw[s0 + seg - HALO:s0 + seg, :]
        raw = jnp.dot(hrows, wg_ref[0, :, GDN_QKV:], preferred_element_type=F32)
        for s0, b, t in pieces:
            og_ref[b, t:t + seg, GDN_QKV:] = raw[s0:s0 + seg]
        for lo, hi in INPROJ_RW_GROUPS:
            raw = jnp.dot(hrows, wr_ref[0, :, lo:hi], preferred_element_type=F32)
            for s0, b, t in pieces:
                or_ref[b, t:t + seg, lo:hi] = token_shift(raw[s0:s0 + seg], halo_r[b, :, lo:hi], shift_mix[:, lo:hi])
                halo_r[b, :, lo:hi] = raw[s0 + seg - HALO:s0 + seg, :]
    craw_ref[...] = halo_g[...]
    sraw_ref[...] = halo_r[...]


def _row_blocks(bsz, seq):
    tt = min(seq, ROW_TILE)
    bb = max(1, min(bsz, ROW_TILE // tt))
    assert seq % tt == 0 and bsz % bb == 0 and tt % 8 == 0
    return bb, tt


def _layer_spec(arr, layer, single_buffer=False):
    kwargs = dict(pipeline_mode=pl.Buffered(1)) if single_buffer else {}
    return pl.BlockSpec((1,) + arr.shape[1:], lambda b, t: (layer,) + (0,) * (arr.ndim - 1), **kwargs)


def _inproj_call(x, sc, sh, wg, wr, conv_w, mu, conv_halo, shift_halo, layer):
    bsz, seq, d = x.shape
    bb, tt = _row_blocks(bsz, seq)
    assert tt >= HALO
    xmap = lambda b, t: (b, t, 0)
    mmap = lambda b, t: (b, 0, 0)
    return pl.pallas_call(
        _inproj_kernel,
        grid=(bsz // bb, seq // tt),
        in_specs=[pl.BlockSpec((bb, tt, d), xmap),
                  pl.BlockSpec((bb, 1, d), mmap),
                  pl.BlockSpec((bb, 1, d), mmap),
                  _layer_spec(wg, layer, single_buffer=True),
                  _layer_spec(wr, layer, single_buffer=True),
                  _layer_spec(conv_w, layer), _layer_spec(mu, layer),
                  pl.BlockSpec((bb, HALO, GDN_QKV), mmap),
                  pl.BlockSpec((bb, HALO, RW_PROJ), mmap)],
        out_specs=[pl.BlockSpec((bb, tt, GDN_COLS), xmap),
                   pl.BlockSpec((bb, tt, RW_PROJ), xmap),
                   pl.BlockSpec((bb, HALO, GDN_QKV), mmap),
                   pl.BlockSpec((bb, HALO, RW_PROJ), mmap)],
        out_shape=[jax.ShapeDtypeStruct((bsz, seq, GDN_COLS), F32),
                   jax.ShapeDtypeStruct((bsz, seq, RW_PROJ), F32),
                   jax.ShapeDtypeStruct((bsz, HALO, GDN_QKV), F32),
                   jax.ShapeDtypeStruct((bsz, HALO, RW_PROJ), F32)],
        scratch_shapes=[pltpu.VMEM((bb, HALO, GDN_QKV), F32),
                        pltpu.VMEM((bb, HALO, RW_PROJ), F32)],
        compiler_params=pltpu.CompilerParams(dimension_semantics=("parallel", "arbitrary"),
                                             vmem_limit_bytes=VMEM_LIMIT),
        name="in_proj",
    )(x, sc, sh, wg, wr, conv_w, mu, conv_halo, shift_halo)


def _gdn_kernel(pg_ref, s0_ref, gp_ref, nw_ref, og_ref, sout_ref,
                s_scr, qd_scr, ov_scr, m_scr, u_scr, dl_scr, *, tt, group, bb):
    c = CHUNK
    gc = group * c
    n_chunks = tt // c
    n_groups = n_chunks // group
    d = GDN_D
    t_idx = pl.program_id(1)

    @pl.when(t_idx == 0)
    def _():
        for b in range(bb):
            s_scr[b * GDN_HEADS:(b + 1) * GDN_HEADS] = s0_ref[b]

    row = lax.broadcasted_iota(jnp.int32, (c, 2 * c), 0)
    lane = lax.broadcasted_iota(jnp.int32, (c, 2 * c), 1)
    first = lane < c
    col = lane & (c - 1)
    causal = col <= row
    strict = col < row
    eye = jnp.where(col == row, 1.0, 0.0).astype(F32)
    grow = lax.broadcasted_iota(jnp.int32, (gc, gc), 0)
    gcol = lax.broadcasted_iota(jnp.int32, (gc, gc), 1)
    ltri = jnp.where(((grow & -c) == (gcol & -c)) & (gcol <= grow), 1.0, 0.0).astype(BF16)
    ones_d = jnp.ones((d, d), BF16)
    zero_2d = jnp.zeros((c, 2 * d), F32)
    neg_a = -jnp.exp(gp_ref[0, 0:1, :])
    dt_bias = gp_ref[0, 1:2, :]
    nw = nw_ref[0]
    heads = range(GDN_HEADS)

    def per_chunk_row(x, r):
        return jnp.concatenate([jnp.broadcast_to(x[j * c + r:j * c + r + 1, :], (c, x.shape[1]))
                                for j in range(group)], axis=0)

    def front(step):
        b, gi = divmod(step, n_groups)
        t0 = gi * gc
        qkv = pg_ref[b, t0:t0 + gc, 0:GDN_QKV]
        gates = pg_ref[b, t0:t0 + gc, GDN_QKV + GDN_WIDTH:GDN_COLS]
        log_a = neg_a * _softplus(gates + dt_bias)
        beta_all = _sigmoid(gates)
        g_all = _cumsum_rows(ltri, log_a)
        yield
        g_rows = g_all.T
        eg_all = jnp.exp(g_all)
        g_last = per_chunk_row(g_all, c - 1)
        dec_all = jnp.exp(g_last - g_all)
        for j in range(group):
            dl_scr[step * group + j] = jnp.broadcast_to(
                jnp.exp(g_all[(j + 1) * c - 1:(j + 1) * c, :]), (8, GATE_COLS))
        yield
        q = [qkv[:, h * d:(h + 1) * d] for h in heads]
        k = [qkv[:, GDN_WIDTH + h * d:GDN_WIDTH + (h + 1) * d] for h in heads]
        v = [qkv[:, 2 * GDN_WIDTH + h * d:2 * GDN_WIDTH + (h + 1) * d] for h in heads]
        ss = _bdot(jnp.concatenate([t * t for t in q + k], axis=0), ones_d)
        yield
        qn, kn = [], []
        for h in heads:
            qn.append(q[h] * lax.rsqrt(ss[h * gc:(h + 1) * gc] + NORM_EPS) * (d ** -0.5))
            kn.append(k[h] * lax.rsqrt(ss[(GDN_HEADS + h) * gc:(GDN_HEADS + h + 1) * gc] + NORM_EPS))
            yield
        items = []
        for j in range(group):
            rs = slice(j * c, (j + 1) * c)
            for p in range(GDN_HEADS // 2):
                pair = (2 * p, 2 * p + 1)
                rhs = []
                for h in pair:
                    beta = beta_all[rs, GDN_HEADS + h:GDN_HEADS + h + 1]
                    rhs.append(jnp.concatenate([v[h][rs] * beta, kn[h][rs] * (beta * eg_all[rs, h:h + 1])], axis=-1))
                items.append(dict(
                    j=j, pair=pair, rs=rs,
                    ks2=jnp.concatenate([kn[h][rs] for h in pair], axis=0).astype(BF16),
                    qs2=jnp.concatenate([qn[h][rs] for h in pair], axis=0).astype(BF16),
                    rhs_bd=jnp.concatenate([jnp.concatenate([rhs[0], zero_2d], axis=-1),
                                            jnp.concatenate([zero_2d, rhs[1]], axis=-1)], axis=0),
                    qdec=[qn[h][rs] * eg_all[rs, h:h + 1] for h in pair],
                    kdec=[kn[h][rs] * dec_all[rs, h:h + 1] for h in pair]))
            yield
        return dict(items=items, g_all=g_all, g_rows=g_rows, beta_all=beta_all)

    def back(step, fr):
        items, g_all, g_rows, beta_all = fr["items"], fr["g_all"], fr["g_rows"], fr["beta_all"]
        r0 = step * gc
        kk2s = [_bdot_nt(it["ks2"], it["ks2"]) for it in items]
        yield
        qk2s = [_bdot_nt(it["qs2"], it["ks2"]) for it in items]
        yield
        for it, kk2, qk2 in zip(items, kk2s, qk2s):
            (h1, h2), rs = it["pair"], it["rs"]
            kk_cat = jnp.where(first, kk2[0:c], kk2[c:2 * c])
            qk_cat = jnp.where(first, qk2[0:c], qk2[c:2 * c])
            g_col_cat = jnp.where(first, g_all[rs, h1:h1 + 1], g_all[rs, h2:h2 + 1])
            g_row_cat = jnp.concatenate([g_rows[h1:h1 + 1, rs], g_rows[h2:h2 + 1, rs]], axis=1)
            beta_cat = jnp.where(first, beta_all[rs, GDN_HEADS + h1:GDN_HEADS + h1 + 1],
                                 beta_all[rs, GDN_HEADS + h2:GDN_HEADS + h2 + 1])
            gam = jnp.where(causal, jnp.exp(jnp.where(causal, g_col_cat - g_row_cat, 0.0)), 0.0)
            it["a_cat"] = jnp.where(strict, beta_cat * kk_cat * gam, 0.0)
            it["qk_cat"] = jnp.where(causal, qk_cat * gam, 0.0)
        yield
        t_invs = yield from _pair_unit_lower_inverse([it["a_cat"] for it in items], first, eye, refine=True)
        sols = [_bdot(t_inv, it["rhs_bd"]) for it, t_inv in zip(items, t_invs)]
        yield
        corrs = [_bdot(it["qk_cat"], jnp.concatenate(
            [jnp.concatenate([sol[:, d:2 * d], sol[:, 0:d], zero_2d], axis=-1),
             jnp.concatenate([zero_2d, sol[:, 3 * d:4 * d], sol[:, 2 * d:3 * d]], axis=-1)], axis=0))
            for it, sol in zip(items, sols)]
        yield
        mus = [[_bdot_tn(it["kdec"][i], jnp.concatenate([sol[:, (2 * i + 1) * d:(2 * i + 2) * d],
                                                         sol[:, 2 * i * d:(2 * i + 1) * d]], axis=-1))
                for i in range(2)] for it, sol in zip(items, sols)]
        yield
        for it, corr, mu_pair in zip(items, corrs, mus):
            ci = step * group + it["j"]
            row = r0 + it["j"] * c
            for i, h in enumerate(it["pair"]):
                qd_scr[ci, h] = (it["qdec"][i] - corr[:, 2 * i * d:(2 * i + 1) * d]).astype(BF16)
                ov_scr[row:row + c, h * d:(h + 1) * d] = corr[:, (2 * i + 1) * d:(2 * i + 2) * d]
                m_scr[ci, h] = mu_pair[i][:, 0:d].astype(BF16)
                u_scr[ci, h] = mu_pair[i][:, d:2 * d]

    n_steps = bb * n_groups
    fr = _run(front(0))
    for step in range(n_steps):
        if step + 1 < n_steps:
            _, fr = _run_interleaved(back(step, fr), front(step + 1))
        else:
            _run(back(step, fr))

    def pass2(ci, carry):
        cis = [b * n_chunks + ci for b in range(bb)]
        rows = [pl.ds(pl.multiple_of(b * tt + ci * c, c), c) for b in range(bb)]
        dls = [dl_scr[cb][0:1, :] for cb in cis]
        s_old = [[s_scr[b * GDN_HEADS + h] for h in heads] for b in range(bb)]
        s_bf = [[s.astype(BF16) for s in sb] for sb in s_old]
        o = [[jnp.dot(qd_scr[cis[b], h], s_bf[b][h], preferred_element_type=F32) for h in heads]
             for b in range(bb)]
        ms = [[jnp.dot(m_scr[cis[b], h], s_bf[b][h], preferred_element_type=F32) for h in heads]
              for b in range(bb)]
        for b in range(bb):
            ov_scr[rows[b], :] = jnp.concatenate(o[b], axis=-1) + ov_scr[rows[b], :]
        s_scr[...] = jnp.stack([s_old[b][h] * dls[b][:, h:h + 1] - ms[b][h] + u_scr[cis[b], h]
                                for b in range(bb) for h in heads])
        return carry

    lax.fori_loop(0, n_chunks, pass2, 0)

    def pass3(step, carry):
        b = step // n_groups
        t0 = pl.multiple_of((step - b * n_groups) * gc, gc)
        r0 = pl.multiple_of(step * gc, gc)
        o = ov_scr[pl.ds(r0, gc), :]
        o_all = jnp.concatenate([o[:, h * d:(h + 1) * d] for h in heads], axis=0)
        ms = _bdot(o_all * o_all, ones_d) * (1.0 / d)
        o_all = o_all * lax.rsqrt(ms + NORM_EPS) * nw
        z = pg_ref[b, pl.ds(t0, gc), GDN_QKV:GDN_QKV + GDN_WIDTH]
        og_ref[b, pl.ds(t0, gc), :] = jnp.concatenate(
            [o_all[h * gc:(h + 1) * gc] for h in heads], axis=-1) * _silu(z)
        return carry

    lax.fori_loop(0, bb * n_groups, pass3, 0)

    @pl.when(t_idx == pl.num_programs(1) - 1)
    def _():
        for b in range(bb):
            sout_ref[b] = s_scr[b * GDN_HEADS:(b + 1) * GDN_HEADS]


def _seq_block(bsz, tt, rows):
    bb = max(1, min(bsz, rows // tt))
    return bb if bsz % bb == 0 else 1


def _gdn_call(pg, s0, gate_params, norm_w, layer):
    bsz, seq, _ = pg.shape
    tt = min(seq, TIME_TILE)
    n_chunks = tt // CHUNK
    group = GDN_GROUP if n_chunks % GDN_GROUP == 0 else 1
    bb = _seq_block(bsz, tt, SEQ_BLOCK * TIME_TILE)
    assert seq % tt == 0 and tt % CHUNK == 0
    smap = lambda b, t: (b, 0, 0, 0)
    return pl.pallas_call(
        functools.partial(_gdn_kernel, tt=tt, group=group, bb=bb),
        grid=(bsz // bb, seq // tt),
        in_specs=[pl.BlockSpec((bb, tt, GDN_COLS), lambda b, t: (b, t, 0)),
                  pl.BlockSpec((bb, GDN_HEADS, GDN_D, GDN_D), smap),
                  _layer_spec(gate_params, layer), _layer_spec(norm_w, layer)],
        out_specs=[pl.BlockSpec((bb, tt, GDN_WIDTH), lambda b, t: (b, t, 0)),
                   pl.BlockSpec((bb, GDN_HEADS, GDN_D, GDN_D), smap)],
        out_shape=[jax.ShapeDtypeStruct((bsz, seq, GDN_WIDTH), F32),
                   jax.ShapeDtypeStruct((bsz, GDN_HEADS, GDN_D, GDN_D), F32)],
        scratch_shapes=[pltpu.VMEM((bb * GDN_HEADS, GDN_D, GDN_D), F32),
                        pltpu.VMEM((bb * n_chunks, GDN_HEADS, CHUNK, GDN_D), BF16),
                        pltpu.VMEM((bb * tt, GDN_WIDTH), F32),
                        pltpu.VMEM((bb * n_chunks, GDN_HEADS, GDN_D, GDN_D), BF16),
                        pltpu.VMEM((bb * n_chunks, GDN_HEADS, GDN_D, GDN_D), F32),
                        pltpu.VMEM((bb * n_chunks, 8, GATE_COLS), F32)],
        compiler_params=pltpu.CompilerParams(dimension_semantics=("parallel", "arbitrary"),
                                             vmem_limit_bytes=VMEM_LIMIT),
        name="gated_deltanet",
    )(pg, s0, gate_params, norm_w)


def _rwkv_kernel(pr_ref, s0_ref, pv_ref, w2_ref, a2_ref, g2_ref,
                 or_ref, sout_ref, s_scr, rq_scr, ov_scr, m_scr, u_scr, dec_scr, gate_scr, bonus_scr,
                 *, tt, group, bb):
    c = CHUNK
    gc = group * c
    lanes = 2 * RW_HD
    n_chunks = tt // c
    n_groups = n_chunks // group
    t_idx = pl.program_id(1)

    @pl.when(t_idx == 0)
    def _():
        for b in range(bb):
            s_scr[b * RW_PAIRS:(b + 1) * RW_PAIRS] = s0_ref[b]

    grow = lax.broadcasted_iota(jnp.int32, (gc, gc), 0)
    gcol = lax.broadcasted_iota(jnp.int32, (gc, gc), 1)
    same_chunk = (grow & -c) == (gcol & -c)
    ltri = jnp.where(same_chunk & (gcol <= grow), 1.0, 0.0).astype(BF16)
    prow = lax.broadcasted_iota(jnp.int32, (c, lanes), 0)
    plane = lax.broadcasted_iota(jnp.int32, (c, lanes), 1)
    first = plane < RW_HD
    pcol = plane & (RW_HD - 1)
    tri_strict = pcol < prow
    tri_incl = pcol <= prow
    eye = jnp.where(pcol == prow, 1.0, 0.0).astype(F32)
    srow = lax.broadcasted_iota(jnp.int32, (lanes, lanes), 0)
    scol = lax.broadcasted_iota(jnp.int32, (lanes, lanes), 1)
    same_head = (srow < RW_HD) == (scol < RW_HD)
    seg_ones = jnp.where(same_head, 1.0, 0.0).astype(BF16)
    w0 = pv_ref[0, 0:1, :]
    a0 = pv_ref[0, 1:2, :]
    kk_p = pv_ref[0, 2:3, :]
    ka_p = pv_ref[0, 3:4, :]
    rk_p = pv_ref[0, 4:5, :]
    ln_w = pv_ref[0, 5:6, :]
    ln_b = pv_ref[0, 6:7, :]
    pairs = range(RW_PAIRS)
    sls = [slice(p * lanes, (p + 1) * lanes) for p in pairs]

    def per_chunk_row(x, row):
        return jnp.concatenate([jnp.broadcast_to(x[j * c + row:j * c + row + 1, :], (c, x.shape[1]))
                                for j in range(group)], axis=0)

    def front(step):
        b, gi = divmod(step, n_groups)
        r0 = step * gc
        xs = pr_ref[b, gi * gc:(gi + 1) * gc, :]
        r = xs[:, 0:RW_WIDTH]
        kr = xs[:, RW_WIDTH:2 * RW_WIDTH]
        vr = xs[:, 2 * RW_WIDTH:3 * RW_WIDTH]
        lora_in = xs[:, 3 * RW_WIDTH:3 * RW_WIDTH + RW_LORA]
        xg = xs[:, 3 * RW_WIDTH + RW_LORA:RW_PROJ]
        w_log = -_softplus(-(w0 + _bdot(jnp.tanh(lora_in), w2_ref[0]))) - 0.5
        log_w = -jnp.exp(w_log)
        yield
        a = _sigmoid(a0 + _bdot(lora_in, a2_ref[0]))
        gate_scr[r0:r0 + gc, :] = _bdot(_sigmoid(xg), g2_ref[0])
        kk_raw = kr * kk_p
        k2 = kr * (1.0 + (a - 1.0) * ka_p)
        rk = r * k2 * rk_p
        yield
        g_cum = _cumsum_rows(ltri, log_w)
        mid_rows = [g_cum[j * c + c // 2 - 1:j * c + c // 2, :] for j in range(group)]
        end_rows = [g_cum[(j + 1) * c - 1:(j + 1) * c, :] for j in range(group)]
        g_mid = per_chunk_row(g_cum, c // 2 - 1)
        yield
        e_left = jnp.exp(g_cum - g_mid)
        yield
        e_left_prev = jnp.exp(g_cum - log_w - g_mid)
        yield
        e_right = jnp.exp(g_mid - g_cum)
        e_mid = [jnp.exp(m) for m in mid_rows]
        e_end_mid = [jnp.exp(e - m) for e, m in zip(end_rows, mid_rows)]
        for j in range(group):
            dec_scr[step * group + j] = jnp.broadcast_to(jnp.exp(end_rows[j]), (8, RW_WIDTH)).T
        yield
        seg = _bdot(jnp.concatenate([kk_raw[:, sl] * kk_raw[:, sl] for sl in sls]
                                    + [rk[:, sl] for sl in sls], axis=0), seg_ones)
        bonus_scr[r0:r0 + gc, :] = jnp.concatenate(
            [seg[(RW_PAIRS + p) * gc:(RW_PAIRS + p + 1) * gc] for p in pairs], axis=-1) * vr
        yield
        items = []
        for j in range(group):
            rs = slice(j * c, (j + 1) * c)
            for p, sl in enumerate(sls):
                kappa = kk_raw[rs, sl] * lax.rsqrt(seg[p * gc + j * c:p * gc + (j + 1) * c] + NORM_EPS)
                r_l = r[rs, sl] * e_left[rs, sl]
                kap_l = kappa * e_left_prev[rs, sl]
                k_r = k2[rs, sl] * e_right[rs, sl]
                b_r = kappa * a[rs, sl] * e_right[rs, sl]
                items.append(dict(
                    j=j, p=p, v=vr[rs, sl], kg=kap_l * e_mid[j][:, sl], rg=r_l * e_mid[j][:, sl],
                    k_d=k_r * e_end_mid[j][:, sl], b_d=b_r * e_end_mid[j][:, sl],
                    lhs=jnp.concatenate([kap_l, r_l], axis=0),
                    rhs=jnp.concatenate([jnp.where(first, b_r, 0.0), jnp.where(first, k_r, 0.0),
                                         jnp.where(first, 0.0, k_r), jnp.where(first, 0.0, b_r)], axis=0)))
            yield
        return items

    def back(step, items):
        r0 = step * gc
        prods = [_bdot_nt(it["lhs"], it["rhs"]) for it in items]
        yield
        for it, res in zip(items, prods):
            kap_1, kap_2 = res[0:c, 0:lanes], res[0:c, lanes:2 * lanes]
            r_1, r_2 = res[c:2 * c, 0:lanes], res[c:2 * c, lanes:2 * lanes]
            it["a_cat"] = jnp.where(tri_strict, jnp.where(first, kap_1, kap_2), 0.0)
            kb_cat = jnp.where(tri_strict, jnp.where(first, kap_2, kap_1), 0.0)
            qk_cat = jnp.where(tri_incl, jnp.where(first, r_2, r_1), 0.0)
            it["kq"] = jnp.concatenate([kb_cat, qk_cat], axis=0)
            it["qb_cat"] = jnp.where(tri_incl, jnp.where(first, r_1, r_2), 0.0)
            v = it["v"]
            it["v_swap"] = jnp.concatenate([jnp.where(first, 0.0, v), jnp.where(first, v, 0.0)], axis=0)
        intras = [_bdot(it["kq"], it["v_swap"]) for it in items]
        yield
        t_invs = yield from _pair_unit_lower_inverse([it["a_cat"] for it in items], first, eye, refine=False)
        sols = [_bdot(t_inv, jnp.concatenate([_pair_blockdiag(it["kg"], first),
                                              _pair_blockdiag(intra[0:c], first)], axis=-1))
                for it, t_inv, intra in zip(items, t_invs, intras)]
        yield
        corrs = [_bdot(it["qb_cat"], jnp.concatenate([_pair_blockdiag(sol[:, 0:lanes], first),
                                                      _pair_blockdiag(sol[:, lanes:2 * lanes], first)], axis=-1))
                 for it, sol in zip(items, sols)]
        yield
        us = [_bdot_tn(jnp.concatenate([it["k_d"], it["b_d"]], axis=0),
                       jnp.concatenate([it["v"], -sol[:, lanes:2 * lanes]], axis=0)) for it, sol in zip(items, sols)]
        yield
        ms = [_bdot_tn(it["b_d"], sol[:, 0:lanes]) for it, sol in zip(items, sols)]
        yield
        for item, intra, corr, u, m in zip(items, intras, corrs, us, ms):
            ci = step * group + item["j"]
            p = item["p"]
            rq_scr[ci, p] = (item["rg"] - corr[:, 0:lanes]).astype(BF16)
            row = r0 + item["j"] * c
            ov_scr[row:row + c, sls[p]] = intra[c:2 * c] - corr[:, lanes:2 * lanes]
            u_scr[ci, p] = jnp.where(same_head, u, 0.0)
            m_scr[ci, p] = jnp.where(same_head, m, 0.0).astype(BF16)

    n_steps = bb * n_groups
    items = _run(front(0))
    for step in range(n_steps):
        if step + 1 < n_steps:
            _, items = _run_interleaved(back(step, items), front(step + 1))
        else:
            _run(back(step, items))

    def pass2(ci, carry):
        cis = [b * n_chunks + ci for b in range(bb)]
        rows = [pl.ds(pl.multiple_of(b * tt + ci * c, c), c) for b in range(bb)]
        s_old = [[s_scr[b * RW_PAIRS + p] for p in pairs] for b in range(bb)]
        s_bf = [[s.astype(BF16) for s in sb] for sb in s_old]
        o = [[jnp.dot(rq_scr[cis[b], p], s_bf[b][p], preferred_element_type=F32) for p in pairs]
             for b in range(bb)]
        sm = [[jnp.dot(m_scr[cis[b], p], s_bf[b][p], preferred_element_type=F32) for p in pairs]
              for b in range(bb)]
        for b in range(bb):
            ov_scr[rows[b], :] = jnp.concatenate(o[b], axis=-1) + ov_scr[rows[b], :]
        s_scr[...] = jnp.stack([s_old[b][p] * dec_scr[cis[b], p * lanes:(p + 1) * lanes, 0:1] - sm[b][p]
                                + u_scr[cis[b], p] for b in range(bb) for p in pairs])
        return carry

    lax.fori_loop(0, n_chunks, pass2, 0)

    def pass3(step, carry):
        b = step // n_groups
        gi = step - b * n_groups
        r0 = pl.multiple_of(step * gc, gc)
        o = ov_scr[pl.ds(r0, gc), :]
        o_all = jnp.concatenate([o[:, sl] for sl in sls], axis=0)
        mean = _bdot(o_all, seg_ones) * (1.0 / RW_HD)
        dev = o_all - mean
        var = _bdot(dev * dev, seg_ones) * (1.0 / RW_HD)
        o_n = dev * lax.rsqrt(var + GN_EPS)
        o_n = jnp.concatenate([o_n[p * gc:(p + 1) * gc] for p in pairs], axis=-1) * ln_w + ln_b
        or_ref[b, pl.ds(pl.multiple_of(gi * gc, gc), gc), :] = (
            (o_n + bonus_scr[pl.ds(r0, gc), :]) * gate_scr[pl.ds(r0, gc), :])
        return carry

    lax.fori_loop(0, bb * n_groups, pass3, 0)

    @pl.when(t_idx == pl.num_programs(1) - 1)
    def _():
        for b in range(bb):
            sout_ref[b] = s_scr[b * RW_PAIRS:(b + 1) * RW_PAIRS]


def _rwkv_call(pr, s0, pvec, w2, a2, g2, layer):
    bsz, seq, _ = pr.shape
    tt = min(seq, RW_TIME_TILE)
    n_chunks = tt // CHUNK
    group = RW_GROUP if n_chunks % RW_GROUP == 0 else 1
    bb = _seq_block(bsz, tt, RW_SEQ_BLOCK * RW_TIME_TILE)
    assert seq % tt == 0 and tt % CHUNK == 0
    lanes = 2 * RW_HD
    smap = lambda b, t: (b, 0, 0, 0)
    return pl.pallas_call(
        functools.partial(_rwkv_kernel, tt=tt, group=group, bb=bb),
        grid=(bsz // bb, seq // tt),
        in_specs=[pl.BlockSpec((bb, tt, RW_PROJ), lambda b, t: (b, t, 0)),
                  pl.BlockSpec((bb, RW_PAIRS, lanes, lanes), smap),
                  _layer_spec(pvec, layer),
                  _layer_spec(w2, layer), _layer_spec(a2, layer), _layer_spec(g2, layer)],
        out_specs=[pl.BlockSpec((bb, tt, RW_WIDTH), lambda b, t: (b, t, 0)),
                   pl.BlockSpec((bb, RW_PAIRS, lanes, lanes), smap)],
        out_shape=[jax.ShapeDtypeStruct((bsz, seq, RW_WIDTH), F32),
                   jax.ShapeDtypeStruct((bsz, RW_PAIRS, lanes, lanes), F32)],
        scratch_shapes=[pltpu.VMEM((bb * RW_PAIRS, lanes, lanes), F32),
                        pltpu.VMEM((bb * n_chunks, RW_PAIRS, CHUNK, lanes), BF16),
                        pltpu.VMEM((bb * tt, RW_WIDTH), F32),
                        pltpu.VMEM((bb * n_chunks, RW_PAIRS, lanes, lanes), BF16),
                        pltpu.VMEM((bb * n_chunks, RW_PAIRS, lanes, lanes), F32),
                        pltpu.VMEM((bb * n_chunks, RW_WIDTH, 8), F32),
                        pltpu.VMEM((bb * tt, RW_WIDTH), F32),
                        pltpu.VMEM((bb * tt, RW_WIDTH), F32)],
        compiler_params=pltpu.CompilerParams(dimension_semantics=("parallel", "arbitrary"),
                                             vmem_limit_bytes=VMEM_LIMIT),
        name="rwkv7",
    )(pr, s0, pvec, w2, a2, g2)


def _outffn_kernel(og_ref, or_ref, x_ref, g1_ref, sc_ref, sh_ref, g2_ref, wo_ref, lnp_ref,
                   w1_ref, w2_ref, o_ref, *, alpha):
    bb, tt, d = x_ref.shape
    rows = bb * tt
    y = (jnp.dot(og_ref[...].reshape(rows, GDN_WIDTH).astype(BF16), wo_ref[0, 0:GDN_WIDTH, :],
                 preferred_element_type=F32)
         + jnp.dot(or_ref[...].reshape(rows, RW_WIDTH).astype(BF16), wo_ref[0, GDN_WIDTH:, :],
                   preferred_element_type=F32)).reshape(bb, tt, d)
    x1 = _layer_norm(alpha * x_ref[...] + g1_ref[...] * y, lnp_ref[0, 0:1, :], lnp_ref[0, 1:2, :], LN_EPS)
    hb = (x1 * (1.0 + sc_ref[...]) + sh_ref[...]).reshape(rows, d).astype(BF16)
    f = jnp.zeros((rows, d), F32)
    for j in range(D_FF // FF_TILE):
        hid = jnp.dot(hb, w1_ref[0, :, j * FF_TILE:(j + 1) * FF_TILE], preferred_element_type=F32)
        hid = jnp.square(jnp.maximum(hid, 0.0)).astype(BF16)
        f = f + jnp.dot(hid, w2_ref[0, j * FF_TILE:(j + 1) * FF_TILE, :], preferred_element_type=F32)
    o_ref[...] = _layer_norm(alpha * x1 + g2_ref[...] * f.reshape(bb, tt, d),
                             lnp_ref[0, 2:3, :], lnp_ref[0, 3:4, :], LN_EPS)


def _outffn_call(og, orr, x, g1, sc2, sh2, g2, wo, lnp, w1, w2, alpha, layer):
    bsz, seq, d = x.shape
    bb, tt = _row_blocks(bsz, seq)
    xmap = lambda b, t: (b, t, 0)
    mmap = lambda b, t: (b, 0, 0)
    mod_spec = pl.BlockSpec((bb, 1, d), mmap)
    return pl.pallas_call(
        functools.partial(_outffn_kernel, alpha=alpha),
        grid=(bsz // bb, seq // tt),
        in_specs=[pl.BlockSpec((bb, tt, GDN_WIDTH), xmap),
                  pl.BlockSpec((bb, tt, RW_WIDTH), xmap),
                  pl.BlockSpec((bb, tt, d), xmap),
                  mod_spec, mod_spec, mod_spec, mod_spec,
                  _layer_spec(wo, layer, single_buffer=True), _layer_spec(lnp, layer),
                  _layer_spec(w1, layer, single_buffer=True), _layer_spec(w2, layer, single_buffer=True)],
        out_specs=pl.BlockSpec((bb, tt, d), xmap),
        out_shape=jax.ShapeDtypeStruct((bsz, seq, d), F32),
        compiler_params=pltpu.CompilerParams(dimension_semantics=("parallel", "parallel"),
                                             vmem_limit_bytes=VMEM_LIMIT),
        name="out_ffn",
    )(og, orr, x, g1, sc2, sh2, g2, wo, lnp, w1, w2)


def _prep_params(w_in, gdn_conv_w, gdn_a_log, gdn_dt_bias, gdn_norm_w, rwkv_mu, rwkv_w0, rwkv_w2,
                 rwkv_a0, rwkv_a2, rwkv_g2, rwkv_kk, rwkv_ka, rwkv_rk, rwkv_ln_w, rwkv_ln_b, w_out,
                 ln1_w, ln1_b, w_ff1, w_ff2, ln2_w, ln2_b):
    depth, d, _ = w_in.shape
    n_gdn = GDN_QKV + GDN_WIDTH + 2 * GDN_HEADS
    w_in = w_in.astype(BF16)
    wg = jnp.concatenate([w_in[:, :, :n_gdn], jnp.zeros((depth, d, GDN_COLS - n_gdn), BF16)], axis=2)
    wr = w_in[:, :, n_gdn:]
    gate_params = jnp.zeros((depth, 8, GATE_COLS), F32)
    gate_params = gate_params.at[:, 0, :GDN_HEADS].set(gdn_a_log).at[:, 1, :GDN_HEADS].set(gdn_dt_bias)
    pvec = jnp.stack([rwkv_w0, rwkv_a0, rwkv_kk, rwkv_ka, rwkv_rk, rwkv_ln_w, rwkv_ln_b,
                      jnp.zeros_like(rwkv_w0)], axis=1)
    zeros = jnp.zeros((depth, RW_LORA // 2, RW_WIDTH), F32)
    w2 = jnp.concatenate([rwkv_w2, zeros], axis=1).astype(BF16)
    a2 = jnp.concatenate([zeros, rwkv_a2], axis=1).astype(BF16)
    zero_row = jnp.zeros_like(ln1_w)
    lnp = jnp.stack([ln1_w, ln1_b, ln2_w, ln2_b] + [zero_row] * 4, axis=1)
    return dict(wg=wg, wr=wr, conv_w=gdn_conv_w, gate_params=gate_params,
                norm_w=gdn_norm_w[:, None, :], mu=rwkv_mu[:, None, :], pvec=pvec,
                w2=w2, a2=a2, g2=rwkv_g2.astype(BF16), wo=w_out.astype(BF16), lnp=lnp,
                w1=w_ff1.astype(BF16), w2f=w_ff2.astype(BF16))


def _rwkv_state_to_pairs(s):
    bsz = s.shape[0]
    s = jnp.swapaxes(s, -1, -2).reshape(bsz, RW_PAIRS, 2, RW_HD, RW_HD)
    z = jnp.zeros_like(s[:, :, 0])
    top = jnp.concatenate([s[:, :, 0], z], axis=-1)
    bot = jnp.concatenate([z, s[:, :, 1]], axis=-1)
    return jnp.concatenate([top, bot], axis=-2)


def _rwkv_state_from_pairs(sp):
    bsz = sp.shape[0]
    return jnp.swapaxes(jnp.stack([sp[:, :, :RW_HD, :RW_HD], sp[:, :, RW_HD:, RW_HD:]], axis=2).reshape(
        bsz, RW_HEADS, RW_HD, RW_HD), -1, -2)


def _trunk(x, mods, st_gdn, st_conv, st_rw, st_shift, p, alpha):
    bsz, seq, d = x.shape
    assert seq >= HALO
    outs = ([], [], [], [])
    for l in range(mods.shape[0]):
        sh1, sc1, g1, sh2, sc2, g2 = [mods[l, :, None, i * d:(i + 1) * d] for i in range(6)]
        conv_halo = jnp.pad(st_conv[l], ((0, 0), (HALO - (CONV_W - 1), 0), (0, 0)))
        shift_halo = jnp.pad(st_shift[l][:, None, :], ((0, 0), (HALO - 1, 0), (0, 0)))
        pg, pr, conv_raw, shift_raw = _inproj_call(x, sc1, sh1, p["wg"], p["wr"], p["conv_w"], p["mu"],
                                                   conv_halo, shift_halo, l)
        og, n_gdn = _gdn_call(pg, st_gdn[l], p["gate_params"], p["norm_w"], l)
        orr, n_rw = _rwkv_call(pr, _rwkv_state_to_pairs(st_rw[l]), p["pvec"], p["w2"], p["a2"], p["g2"], l)
        outs[0].append(n_gdn)
        outs[1].append(conv_raw[:, HALO - (CONV_W - 1):, :])
        outs[2].append(_rwkv_state_from_pairs(n_rw))
        outs[3].append(shift_raw[:, HALO - 1, :])
        x = _outffn_call(og, orr, x, g1, sc2, sh2, g2, p["wo"], p["lnp"], p["w1"], p["w2f"], alpha, l)
    return (x,) + tuple(jnp.stack(o) for o in outs)


def kernel(x_prompt, x_sample, c_prompt, c_sample, state_gdn, state_gdn_conv, state_rwkv, state_rwkv_shift, w_ada, b_ada, w_in, gdn_conv_w, gdn_a_log, gdn_dt_bias, gdn_norm_w, rwkv_mu, rwkv_w0, rwkv_w2, rwkv_a0, rwkv_a2, rwkv_g2, rwkv_kk, rwkv_ka, rwkv_rk, rwkv_ln_w, rwkv_ln_b, w_out, ln1_w, ln1_b, w_ff1, w_ff2, ln2_w, ln2_b):
    depth = w_in.shape[0]
    alpha = (2 * depth) ** DEPTH_ALPHA_POW
    bp = x_prompt.shape[0]
    params = _prep_params(w_in, gdn_conv_w, gdn_a_log, gdn_dt_bias, gdn_norm_w, rwkv_mu, rwkv_w0, rwkv_w2,
                          rwkv_a0, rwkv_a2, rwkv_g2, rwkv_kk, rwkv_ka, rwkv_rk, rwkv_ln_w, rwkv_ln_b,
                          w_out, ln1_w, ln1_b, w_ff1, w_ff2, ln2_w, ln2_b)
    mods = _mod_call(jnp.concatenate([c_prompt, c_sample], axis=0), w_ada, b_ada)
    zeros = lambda *shape: jnp.zeros((depth, bp) + shape, F32)
    y_p, p_gdn, p_conv, p_rw, p_shift = _trunk(
        x_prompt, mods[:, :bp],
        zeros(GDN_HEADS, GDN_D, GDN_D), zeros(CONV_W - 1, GDN_QKV),
        zeros(RW_HEADS, RW_HD, RW_HD), zeros(RW_PROJ), params, alpha)
    y_s, s_gdn, s_conv, s_rw, s_shift = _trunk(
        x_sample, mods[:, bp:], state_gdn, state_gdn_conv, state_rwkv, state_rwkv_shift, params, alpha)
    return (y_p, y_s, p_gdn, p_conv, p_rw, p_shift, s_gdn, s_conv, s_rw, s_shift)
```

```python
import functools

import jax
import jax.numpy as jnp
from jax import lax
from jax.experimental import pallas as pl
from jax.experimental.pallas import tpu as pltpu

F32 = jnp.float32
BF16 = jnp.bfloat16

D_MODEL = 1024
CHUNK = 64
GDN_HEADS = 4
GDN_D = 128
GDN_WIDTH = GDN_HEADS * GDN_D
GDN_QKV = 3 * GDN_WIDTH
CONV_W = 4
GATE_COLS = 128
GDN_COLS = GDN_QKV + GDN_WIDTH + GATE_COLS
RW_HEADS = 8
RW_HD = 64
RW_WIDTH = RW_HEADS * RW_HD
RW_PAIRS = RW_HEADS // 2
RW_LORA = 128
RW_PROJ = 3 * RW_WIDTH + 2 * RW_LORA
D_FF = 4 * D_MODEL
DEPTH_ALPHA_POW = 0.25
LN_EPS = 1e-5
GN_EPS = 64e-5
NORM_EPS = 1e-6
HALO = 8
ROW_TILE = 512
TIME_TILE = 512
FF_TILE = 1024
GDN_GROUP = 8
RW_GROUP = 4
RW_TIME_TILE = 256
RW_SEQ_BLOCK = 4
SEQ_BLOCK = 2
INPROJ_QKV_GROUPS = tuple((i, i + 512) for i in range(0, 1536, 512))
INPROJ_RW_GROUPS = tuple((i, i + 512) for i in range(0, 1536, 512)) + ((1536, 1792),)
INPROJ_ROWS = 128
VMEM_LIMIT = 56 * 1024 * 1024


def _bdot(a, b):
    return jnp.dot(a.astype(BF16), b.astype(BF16), preferred_element_type=F32)


def _bdot_nt(a, b):
    return lax.dot_general(a.astype(BF16), b.astype(BF16), (((1,), (1,)), ((), ())),
                           preferred_element_type=F32)


def _bdot_tn(a, b):
    return lax.dot_general(a.astype(BF16), b.astype(BF16), (((0,), (0,)), ((), ())),
                           preferred_element_type=F32)


def _split2(x):
    hi = x.astype(BF16)
    return hi, (x - hi.astype(F32)).astype(BF16)


def _dot3(a, b):
    ah, al = _split2(a)
    bh, bl = _split2(b)
    return (jnp.dot(ah, bh, preferred_element_type=F32) + jnp.dot(ah, bl, preferred_element_type=F32)
            + jnp.dot(al, bh, preferred_element_type=F32))


def _cumsum_rows(ltri_bf16, x):
    hi = x.astype(BF16)
    rest = x - hi.astype(F32)
    mid = rest.astype(BF16)
    lo = (rest - mid.astype(F32)).astype(BF16)
    return (jnp.dot(ltri_bf16, hi, preferred_element_type=F32)
            + jnp.dot(ltri_bf16, mid, preferred_element_type=F32)
            + jnp.dot(ltri_bf16, lo, preferred_element_type=F32))


def _sigmoid(x):
    return 0.5 * jnp.tanh(0.5 * x) + 0.5


def _silu(x):
    return x * _sigmoid(x)


def _softplus(x):
    return jnp.maximum(x, 0.0) + jnp.log(1.0 + jnp.exp(-jnp.abs(x)))


def _pair_blockdiag(x, first):
    zero = jnp.zeros_like(x)
    return jnp.concatenate([jnp.where(first, x, zero), jnp.where(first, zero, x)], axis=0)


def _pair_unit_lower_inverse(a_list, first, eye, refine):
    n = eye.shape[0]
    xs = [eye - a for a in a_list]
    ps = list(a_list)
    bds = [_pair_blockdiag(p.astype(BF16), first) for p in ps]
    terms = 2
    while terms < n:
        ps = [jnp.dot(p.astype(BF16), bd, preferred_element_type=F32) for p, bd in zip(ps, bds)]
        yield
        bds = [_pair_blockdiag(p.astype(BF16), first) for p in ps]
        xs = [x + jnp.dot(x.astype(BF16), bd, preferred_element_type=F32) for x, bd in zip(xs, bds)]
        yield
        terms *= 2
    if not refine:
        return xs
    es = [_dot3(eye + a, _pair_blockdiag(x, first)) - eye for a, x in zip(a_list, xs)]
    yield
    return [x - _bdot(x, _pair_blockdiag(e, first)) for x, e in zip(xs, es)]


def _run(gen):
    while True:
        try:
            next(gen)
        except StopIteration as stop:
            return stop.value


def _run_interleaved(main, side):
    main_val = side_val = None
    main_live = side_live = True
    while main_live or side_live:
        if main_live:
            try:
                next(main)
            except StopIteration as stop:
                main_val, main_live = stop.value, False
        if side_live:
            try:
                next(side)
            except StopIteration as stop:
                side_val, side_live = stop.value, False
    return main_val, side_val


def _layer_norm(t, w, b, eps):
    mu = jnp.mean(t, axis=-1, keepdims=True)
    d = t - mu
    var = jnp.mean(d * d, axis=-1, keepdims=True)
    return d * lax.rsqrt(var + eps) * w + b


def _mod_kernel(c_ref, w_ref, b_ref, o_ref):
    o_ref[0] = _bdot(_silu(c_ref[...]), w_ref[0]) + b_ref[0]


def _mod_call(c_all, w_ada, b_ada):
    depth, d, n = w_ada.shape
    rows = c_all.shape[0]
    tn = 1536
    return pl.pallas_call(
        _mod_kernel,
        grid=(depth, n // tn),
        in_specs=[pl.BlockSpec((rows, d), lambda l, j: (0, 0)),
                  pl.BlockSpec((1, d, tn), lambda l, j: (l, 0, j)),
                  pl.BlockSpec((1, 1, tn), lambda l, j: (l, 0, j))],
        out_specs=pl.BlockSpec((1, rows, tn), lambda l, j: (l, 0, j)),
        out_shape=jax.ShapeDtypeStruct((depth, rows, n), F32),
        compiler_params=pltpu.CompilerParams(dimension_semantics=("parallel", "parallel"),
                                             vmem_limit_bytes=VMEM_LIMIT),
        name="adaln_mod",
    )(c_all, w_ada, b_ada.reshape(depth, 1, n))


def _inproj_kernel(x_ref, sc_ref, sh_ref, wg_ref, wr_ref, cw_ref, mu_ref, cst_ref, sst_ref,
                   og_ref, or_ref, craw_ref, sraw_ref, halo_g, halo_r):
    bb, tt, d = x_ref.shape

    @pl.when(pl.program_id(1) == 0)
    def _():
        halo_g[...] = cst_ref[...]
        halo_r[...] = sst_ref[...]

    h = x_ref[...] * (1.0 + sc_ref[...]) + sh_ref[...]
    hb = h.reshape(bb * tt, d).astype(BF16)
    conv_w = cw_ref[0]
    shift_mix = mu_ref[0]

    def conv_silu(raw, halo, w):
        blk = jnp.concatenate([halo, raw], axis=0)
        prev = pltpu.roll(blk, 1, axis=0)
        acc = (blk * w[3:4, :] + prev * w[2:3, :]
               + pltpu.roll(blk * w[1:2, :] + prev * w[0:1, :], 2, axis=0))
        return _silu(acc[HALO:, :])

    def token_shift(raw, halo, m):
        blk = jnp.concatenate([halo, raw], axis=0)
        return (blk + (pltpu.roll(blk, 1, axis=0) - blk) * m)[HALO:, :]

    rb_rows = min(bb * tt, INPROJ_ROWS)
    seg = min(tt, rb_rows)
    for r0 in range(0, bb * tt, rb_rows):
        hrows = hb[r0:r0 + rb_rows]
        pieces = [(s0,) + divmod(r0 + s0, tt) for s0 in range(0, rb_rows, seg)]
        for lo, hi in INPROJ_QKV_GROUPS:
            raw = jnp.dot(hrows, wg_ref[0, :, lo:hi], preferred_element_type=F32)
            for s0, b, t in pieces:
                og_ref[b, t:t + seg, lo:hi] = conv_silu(raw[s0:s0 + seg], halo_g[b, :, lo:hi], conv_w[:, lo:hi])
                halo_g[b, :, lo:hi] = raw[s0 + seg - HALO:s0 + seg, :]
        raw = jnp.dot(hrows, wg_ref[0, :, GDN_QKV:], preferred_element_type=F32)
        for s0, b, t in pieces:
            og_ref[b, t:t + seg, GDN_QKV:] = raw[s0:s0 + seg]
    for r0 in range(0, bb * tt, rb_rows):
        hrows = hb[r0:r0 + rb_rows]
        pieces = [(s0,) + divmod(r0 + s0, tt) for s0 in range(0, rb_rows, seg)]
        for lo, hi in INPROJ_RW_GROUPS:
            raw = jnp.dot(hrows, wr_ref[0, :, lo:hi], preferred_element_type=F32)
            for s0, b, t in pieces:
                or_ref[b, t:t + seg, lo:hi] = token_shift(raw[s0:s0 + seg], halo_r[b, :, lo:hi], shift_mix[:, lo:hi])
                halo_r[b, :, lo:hi] = raw[s0 + seg - HALO:s0 + seg, :]
    craw_ref[...] = halo_g[...]
    sraw_ref[...] = halo_r[...]


def _row_blocks(bsz, seq):
    tt = min(seq, ROW_TILE)
    bb = max(1, min(bsz, ROW_TILE // tt))
    assert seq % tt == 0 and bsz % bb == 0 and tt % 8 == 0
    return bb, tt


def _layer_spec(arr, layer, single_buffer=False):
    kwargs = dict(pipeline_mode=pl.Buffered(1)) if single_buffer else {}
    return pl.BlockSpec((1,) + arr.shape[1:], lambda b, t: (layer,) + (0,) * (arr.ndim - 1), **kwargs)


def _inproj_call(x, sc, sh, wg, wr, conv_w, mu, conv_halo, shift_halo, layer):
    bsz, seq, d = x.shape
    bb, tt = _row_blocks(bsz, seq)
    assert tt >= HALO
    xmap = lambda b, t: (b, t, 0)
    mmap = lambda b, t: (b, 0, 0)
    return pl.pallas_call(
        _inproj_kernel,
        grid=(bsz // bb, seq // tt),
        in_specs=[pl.BlockSpec((bb, tt, d), xmap),
                  pl.BlockSpec((bb, 1, d), mmap),
                  pl.BlockSpec((bb, 1, d), mmap),
                  _layer_spec(wg, layer, single_buffer=True),
                  _layer_spec(wr, layer, single_buffer=True),
                  _layer_spec(conv_w, layer), _layer_spec(mu, layer),
                  pl.BlockSpec((bb, HALO, GDN_QKV), mmap),
                  pl.BlockSpec((bb, HALO, RW_PROJ), mmap)],
        out_specs=[pl.BlockSpec((bb, tt, GDN_COLS), xmap),
                   pl.BlockSpec((bb, tt, RW_PROJ), xmap),
                   pl.BlockSpec((bb, HALO, GDN_QKV), mmap),
                   pl.BlockSpec((bb, HALO, RW_PROJ), mmap)],
        out_shape=[jax.ShapeDtypeStruct((bsz, seq, GDN_COLS), F32),
                   jax.ShapeDtypeStruct((bsz, seq, RW_PROJ), F32),
                   jax.ShapeDtypeStruct((bsz, HALO, GDN_QKV), F32),
                   jax.ShapeDtypeStruct((bsz, HALO, RW_PROJ), F32)],
        scratch_shapes=[pltpu.VMEM((bb, HALO, GDN_QKV), F32),
                        pltpu.VMEM((bb, HALO, RW_PROJ), F32)],
        compiler_params=pltpu.CompilerParams(dimension_semantics=("parallel", "arbitrary"),
                                             vmem_limit_bytes=VMEM_LIMIT),
        name="in_proj",
    )(x, sc, sh, wg, wr, conv_w, mu, conv_halo, shift_halo)


def _gdn_kernel(pg_ref, s0_ref, gp_ref, nw_ref, og_ref, sout_ref,
                s_scr, qd_scr, ov_scr, m_scr, u_scr, dl_scr, *, tt, group, bb):
    c = CHUNK
    gc = group * c
    n_chunks = tt // c
    n_groups = n_chunks // group
    d = GDN_D
    t_idx = pl.program_id(1)

    @pl.when(t_idx == 0)
    def _():
        for b in range(bb):
            s_scr[b * GDN_HEADS:(b + 1) * GDN_HEADS] = s0_ref[b]

    row = lax.broadcasted_iota(jnp.int32, (c, 2 * c), 0)
    lane = lax.broadcasted_iota(jnp.int32, (c, 2 * c), 1)
    first = lane < c
    col = lane & (c - 1)
    causal = col <= row
    strict = col < row
    eye = jnp.where(col == row, 1.0, 0.0).astype(F32)
    grow = lax.broadcasted_iota(jnp.int32, (gc, gc), 0)
    gcol = lax.broadcasted_iota(jnp.int32, (gc, gc), 1)
    ltri = jnp.where(((grow & -c) == (gcol & -c)) & (gcol <= grow), 1.0, 0.0).astype(BF16)
    ones_d = jnp.ones((d, d), BF16)
    zero_2d = jnp.zeros((c, 2 * d), F32)
    neg_a = -jnp.exp(gp_ref[0, 0:1, :])
    dt_bias = gp_ref[0, 1:2, :]
    nw = nw_ref[0]
    heads = range(GDN_HEADS)

    def per_chunk_row(x, r):
        return jnp.concatenate([jnp.broadcast_to(x[j * c + r:j * c + r + 1, :], (c, x.shape[1]))
                                for j in range(group)], axis=0)

    def front(step):
        b, gi = divmod(step, n_groups)
        t0 = gi * gc
        qkv = pg_ref[b, t0:t0 + gc, 0:GDN_QKV]
        gates = pg_ref[b, t0:t0 + gc, GDN_QKV + GDN_WIDTH:GDN_COLS]
        log_a = neg_a * _softplus(gates + dt_bias)
        beta_all = _sigmoid(gates)
        g_all = _cumsum_rows(ltri, log_a)
        yield
        g_rows = g_all.T
        eg_all = jnp.exp(g_all)
        g_last = per_chunk_row(g_all, c - 1)
        dec_all = jnp.exp(g_last - g_all)
        for j in range(group):
            dl_scr[step * group + j] = jnp.broadcast_to(
                jnp.exp(g_all[(j + 1) * c - 1:(j + 1) * c, :]), (8, GATE_COLS))
        yield
        q = [qkv[:, h * d:(h + 1) * d] for h in heads]
        k = [qkv[:, GDN_WIDTH + h * d:GDN_WIDTH + (h + 1) * d] for h in heads]
        v = [qkv[:, 2 * GDN_WIDTH + h * d:2 * GDN_WIDTH + (h + 1) * d] for h in heads]
        ss = _bdot(jnp.concatenate([t * t for t in q + k], axis=0), ones_d)
        yield
        qn, kn = [], []
        for h in heads:
            qn.append(q[h] * lax.rsqrt(ss[h * gc:(h + 1) * gc] + NORM_EPS) * (d ** -0.5))
            kn.append(k[h] * lax.rsqrt(ss[(GDN_HEADS + h) * gc:(GDN_HEADS + h + 1) * gc] + NORM_EPS))
            yield
        items = []
        for j in range(group):
            rs = slice(j * c, (j + 1) * c)
            for p in range(GDN_HEADS // 2):
                pair = (2 * p, 2 * p + 1)
                rhs = []
                for h in pair:
                    beta = beta_all[rs, GDN_HEADS + h:GDN_HEADS + h + 1]
                    rhs.append(jnp.concatenate([v[h][rs] * beta, kn[h][rs] * (beta * eg_all[rs, h:h + 1])], axis=-1))
                items.append(dict(
                    j=j, pair=pair, rs=rs,
                    ks2=jnp.concatenate([kn[h][rs] for h in pair], axis=0).astype(BF16),
                    qs2=jnp.concatenate([qn[h][rs] for h in pair], axis=0).astype(BF16),
                    rhs_bd=jnp.concatenate([jnp.concatenate([rhs[0], zero_2d], axis=-1),
                                            jnp.concatenate([zero_2d, rhs[1]], axis=-1)], axis=0),
                    qdec=[qn[h][rs] * eg_all[rs, h:h + 1] for h in pair],
                    kdec=[kn[h][rs] * dec_all[rs, h:h + 1] for h in pair]))
            yield
        return dict(items=items, g_all=g_all, g_rows=g_rows, beta_all=beta_all)

    def back(step, fr):
        items, g_all, g_rows, beta_all = fr["items"], fr["g_all"], fr["g_rows"], fr["beta_all"]
        r0 = step * gc
        kk2s = [_bdot_nt(it["ks2"], it["ks2"]) for it in items]
        yield
        qk2s = [_bdot_nt(it["qs2"], it["ks2"]) for it in items]
        yield
        for it, kk2, qk2 in zip(items, kk2s, qk2s):
            (h1, h2), rs = it["pair"], it["rs"]
            kk_cat = jnp.where(first, kk2[0:c], kk2[c:2 * c])
            qk_cat = jnp.where(first, qk2[0:c], qk2[c:2 * c])
            g_col_cat = jnp.where(first, g_all[rs, h1:h1 + 1], g_all[rs, h2:h2 + 1])
            g_row_cat = jnp.concatenate([g_rows[h1:h1 + 1, rs], g_rows[h2:h2 + 1, rs]], axis=1)
            beta_cat = jnp.where(first, beta_all[rs, GDN_HEADS + h1:GDN_HEADS + h1 + 1],
                                 beta_all[rs, GDN_HEADS + h2:GDN_HEADS + h2 + 1])
            gam = jnp.where(causal, jnp.exp(jnp.where(causal, g_col_cat - g_row_cat, 0.0)), 0.0)
            it["a_cat"] = jnp.where(strict, beta_cat * kk_cat * gam, 0.0)
            it["qk_cat"] = jnp.where(causal, qk_cat * gam, 0.0)
        yield
        t_invs = yield from _pair_unit_lower_inverse([it["a_cat"] for it in items], first, eye, refine=True)
        sols = [_bdot(t_inv, it["rhs_bd"]) for it, t_inv in zip(items, t_invs)]
        yield
        corrs = [_bdot(it["qk_cat"], jnp.concatenate(
            [jnp.concatenate([sol[:, d:2 * d], sol[:, 0:d], zero_2d], axis=-1),
             jnp.concatenate([zero_2d, sol[:, 3 * d:4 * d], sol[:, 2 * d:3 * d]], axis=-1)], axis=0))
            for it, sol in zip(items, sols)]
        yield
        mus = [[_bdot_tn(it["kdec"][i], jnp.concatenate([sol[:, (2 * i + 1) * d:(2 * i + 2) * d],
                                                         sol[:, 2 * i * d:(2 * i + 1) * d]], axis=-1))
                for i in range(2)] for it, sol in zip(items, sols)]
        yield
        for it, corr, mu_pair in zip(items, corrs, mus):
            ci = step * group + it["j"]
            row = r0 + it["j"] * c
            for i, h in enumerate(it["pair"]):
                qd_scr[ci, h] = (it["qdec"][i] - corr[:, 2 * i * d:(2 * i + 1) * d]).astype(BF16)
                ov_scr[row:row + c, h * d:(h + 1) * d] = corr[:, (2 * i + 1) * d:(2 * i + 2) * d]
                m_scr[ci, h] = mu_pair[i][:, 0:d].astype(BF16)
                u_scr[ci, h] = mu_pair[i][:, d:2 * d]

    n_steps = bb * n_groups
    fr = _run(front(0))
    for step in range(n_steps):
        if step + 1 < n_steps:
            _, fr = _run_interleaved(back(step, fr), front(step + 1))
        else:
            _run(back(step, fr))

    def pass2(ci, carry):
        cis = [b * n_chunks + ci for b in range(bb)]
        rows = [pl.ds(pl.multiple_of(b * tt + ci * c, c), c) for b in range(bb)]
        dls = [dl_scr[cb][0:1, :] for cb in cis]
        s_old = [[s_scr[b * GDN_HEADS + h] for h in heads] for b in range(bb)]
        s_bf = [[s.astype(BF16) for s in sb] for sb in s_old]
        o = [[jnp.dot(qd_scr[cis[b], h], s_bf[b][h], preferred_element_type=F32) for h in heads]
             for b in range(bb)]
        ms = [[jnp.dot(m_scr[cis[b], h], s_bf[b][h], preferred_element_type=F32) for h in heads]
              for b in range(bb)]
        for b in range(bb):
            ov_scr[rows[b], :] = jnp.concatenate(o[b], axis=-1) + ov_scr[rows[b], :]
        s_scr[...] = jnp.stack([s_old[b][h] * dls[b][:, h:h + 1] - ms[b][h] + u_scr[cis[b], h]
                                for b in range(bb) for h in heads])
        return carry

    lax.fori_loop(0, n_chunks, pass2, 0)

    def pass3(step, carry):
        b = step // n_groups
        t0 = pl.multiple_of((step - b * n_groups) * gc, gc)
        r0 = pl.multiple_of(step * gc, gc)
        o = ov_scr[pl.ds(r0, gc), :]
        o_all = jnp.concatenate([o[:, h * d:(h + 1) * d] for h in heads], axis=0)
        ms = _bdot(o_all * o_all, ones_d) * (1.0 / d)
        o_all = o_all * lax.rsqrt(ms + NORM_EPS) * nw
        z = pg_ref[b, pl.ds(t0, gc), GDN_QKV:GDN_QKV + GDN_WIDTH]
        og_ref[b, pl.ds(t0, gc), :] = jnp.concatenate(
            [o_all[h * gc:(h + 1) * gc] for h in heads], axis=-1) * _silu(z)
        return carry

    lax.fori_loop(0, bb * n_groups, pass3, 0)

    @pl.when(t_idx == pl.num_programs(1) - 1)
    def _():
        for b in range(bb):
            sout_ref[b] = s_scr[b * GDN_HEADS:(b + 1) * GDN_HEADS]


def _seq_block(bsz, tt, rows):
    bb = max(1, min(bsz, rows // tt))
    return bb if bsz % bb == 0 else 1


def _gdn_call(pg, s0, gate_params, norm_w, layer):
    bsz, seq, _ = pg.shape
    tt = min(seq, TIME_TILE)
    n_chunks = tt // CHUNK
    group = GDN_GROUP if n_chunks % GDN_GROUP == 0 else 1
    bb = _seq_block(bsz, tt, SEQ_BLOCK * TIME_TILE)
    assert seq % tt == 0 and tt % CHUNK == 0
    smap = lambda b, t: (b, 0, 0, 0)
    return pl.pallas_call(
        functools.partial(_gdn_kernel, tt=tt, group=group, bb=bb),
        grid=(bsz // bb, seq // tt),
        in_specs=[pl.BlockSpec((bb, tt, GDN_COLS), lambda b, t: (b, t, 0)),
                  pl.BlockSpec((bb, GDN_HEADS, GDN_D, GDN_D), smap),
                  _layer_spec(gate_params, layer), _layer_spec(norm_w, layer)],
        out_specs=[pl.BlockSpec((bb, tt, GDN_WIDTH), lambda b, t: (b, t, 0)),
                   pl.BlockSpec((bb, GDN_HEADS, GDN_D, GDN_D), smap)],
        out_shape=[jax.ShapeDtypeStruct((bsz, seq, GDN_WIDTH), F32),
                   jax.ShapeDtypeStruct((bsz, GDN_HEADS, GDN_D, GDN_D), F32)],
        scratch_shapes=[pltpu.VMEM((bb * GDN_HEADS, GDN_D, GDN_D), F32),
                        pltpu.VMEM((bb * n_chunks, GDN_HEADS, CHUNK, GDN_D), BF16),
                        pltpu.VMEM((bb * tt, GDN_WIDTH), F32),
                        pltpu.VMEM((bb * n_chunks, GDN_HEADS, GDN_D, GDN_D), BF16),
                        pltpu.VMEM((bb * n_chunks, GDN_HEADS, GDN_D, GDN_D), F32),
                        pltpu.VMEM((bb * n_chunks, 8, GATE_COLS), F32)],
        compiler_params=pltpu.CompilerParams(dimension_semantics=("parallel", "arbitrary"),
                                             vmem_limit_bytes=VMEM_LIMIT),
        name="gated_deltanet",
    )(pg, s0, gate_params, norm_w)


def _rwkv_kernel(pr_ref, s0_ref, pv_ref, w2_ref, a2_ref, g2_ref,
                 or_ref, sout_ref, s_scr, rq_scr, ov_scr, m_scr, u_scr, dec_scr, gate_scr, bonus_scr,
                 *, tt, group, bb):
    c = CHUNK
    gc = group * c
    lanes = 2 * RW_HD
    n_chunks = tt // c
    n_groups = n_chunks // group
    t_idx = pl.program_id(1)

    @pl.when(t_idx == 0)
    def _():
        for b in range(bb):
            s_scr[b * RW_PAIRS:(b + 1) * RW_PAIRS] = s0_ref[b]

    grow = lax.broadcasted_iota(jnp.int32, (gc, gc), 0)
    gcol = lax.broadcasted_iota(jnp.int32, (gc, gc), 1)
    same_chunk = (grow & -c) == (gcol & -c)
    ltri = jnp.where(same_chunk & (gcol <= grow), 1.0, 0.0).astype(BF16)
    prow = lax.broadcasted_iota(jnp.int32, (c, lanes), 0)
    plane = lax.broadcasted_iota(jnp.int32, (c, lanes), 1)
    first = plane < RW_HD
    pcol = plane & (RW_HD - 1)
    tri_strict = pcol < prow
    tri_incl = pcol <= prow
    eye = jnp.where(pcol == prow, 1.0, 0.0).astype(F32)
    srow = lax.broadcasted_iota(jnp.int32, (lanes, lanes), 0)
    scol = lax.broadcasted_iota(jnp.int32, (lanes, lanes), 1)
    same_head = (srow < RW_HD) == (scol < RW_HD)
    seg_ones = jnp.where(same_head, 1.0, 0.0).astype(BF16)
    w0 = pv_ref[0, 0:1, :]
    a0 = pv_ref[0, 1:2, :]
    kk_p = pv_ref[0, 2:3, :]
    ka_p = pv_ref[0, 3:4, :]
    rk_p = pv_ref[0, 4:5, :]
    ln_w = pv_ref[0, 5:6, :]
    ln_b = pv_ref[0, 6:7, :]
    pairs = range(RW_PAIRS)
    sls = [slice(p * lanes, (p + 1) * lanes) for p in pairs]

    def per_chunk_row(x, row):
        return jnp.concatenate([jnp.broadcast_to(x[j * c + row:j * c + row + 1, :], (c, x.shape[1]))
                                for j in range(group)], axis=0)

    def front(step):
        b, gi = divmod(step, n_groups)
        r0 = step * gc
        xs = pr_ref[b, gi * gc:(gi + 1) * gc, :]
        r = xs[:, 0:RW_WIDTH]
        kr = xs[:, RW_WIDTH:2 * RW_WIDTH]
        vr = xs[:, 2 * RW_WIDTH:3 * RW_WIDTH]
        lora_in = xs[:, 3 * RW_WIDTH:3 * RW_WIDTH + RW_LORA]
        xg = xs[:, 3 * RW_WIDTH + RW_LORA:RW_PROJ]
        w_log = -_softplus(-(w0 + _bdot(jnp.tanh(lora_in), w2_ref[0]))) - 0.5
        log_w = -jnp.exp(w_log)
        yield
        a = _sigmoid(a0 + _bdot(lora_in, a2_ref[0]))
        gate_scr[r0:r0 + gc, :] = _bdot(_sigmoid(xg), g2_ref[0])
        kk_raw = kr * kk_p
        k2 = kr * (1.0 + (a - 1.0) * ka_p)
        rk = r * k2 * rk_p
        yield
        g_cum = _cumsum_rows(ltri, log_w)
        mid_rows = [g_cum[j * c + c // 2 - 1:j * c + c // 2, :] for j in range(group)]
        end_rows = [g_cum[(j + 1) * c - 1:(j + 1) * c, :] for j in range(group)]
        g_mid = per_chunk_row(g_cum, c // 2 - 1)
        yield
        e_left = jnp.exp(g_cum - g_mid)
        yield
        e_left_prev = jnp.exp(g_cum - log_w - g_mid)
        yield
        e_right = jnp.exp(g_mid - g_cum)
        e_mid = [jnp.exp(m) for m in mid_rows]
        e_end_mid = [jnp.exp(e - m) for e, m in zip(end_rows, mid_rows)]
        for j in range(group):
            dec_scr[step * group + j] = jnp.broadcast_to(jnp.exp(end_rows[j]), (8, RW_WIDTH)).T
        yield
        seg = _bdot(jnp.concatenate([kk_raw[:, sl] * kk_raw[:, sl] for sl in sls]
                                    + [rk[:, sl] for sl in sls], axis=0), seg_ones)
        bonus_scr[r0:r0 + gc, :] = jnp.concatenate(
            [seg[(RW_PAIRS + p) * gc:(RW_PAIRS + p + 1) * gc] for p in pairs], axis=-1) * vr
        yield
        items = []
        for j in range(group):
            rs = slice(j * c, (j + 1) * c)
            for p, sl in enumerate(sls):
                kappa = kk_raw[rs, sl] * lax.rsqrt(seg[p * gc + j * c:p * gc + (j + 1) * c] + NORM_EPS)
                r_l = r[rs, sl] * e_left[rs, sl]
                kap_l = kappa * e_left_prev[rs, sl]
                k_r = k2[rs, sl] * e_right[rs, sl]
                b_r = kappa * a[rs, sl] * e_right[rs, sl]
                items.append(dict(
                    j=j, p=p, v=vr[rs, sl], kg=kap_l * e_mid[j][:, sl], rg=r_l * e_mid[j][:, sl],
                    k_d=k_r * e_end_mid[j][:, sl], b_d=b_r * e_end_mid[j][:, sl],
                    lhs=jnp.concatenate([kap_l, r_l], axis=0),
                    rhs=jnp.concatenate([jnp.where(first, b_r, 0.0), jnp.where(first, k_r, 0.0),
                                         jnp.where(first, 0.0, k_r), jnp.where(first, 0.0, b_r)], axis=0)))
            yield
        return items

    def back(step, items):
        r0 = step * gc
        prods = [_bdot_nt(it["lhs"], it["rhs"]) for it in items]
        yield
        for it, res in zip(items, prods):
            kap_1, kap_2 = res[0:c, 0:lanes], res[0:c, lanes:2 * lanes]
            r_1, r_2 = res[c:2 * c, 0:lanes], res[c:2 * c, lanes:2 * lanes]
            it["a_cat"] = jnp.where(tri_strict, jnp.where(first, kap_1, kap_2), 0.0)
            kb_cat = jnp.where(tri_strict, jnp.where(first, kap_2, kap_1), 0.0)
            qk_cat = jnp.where(tri_incl, jnp.where(first, r_2, r_1), 0.0)
            it["kq"] = jnp.concatenate([kb_cat, qk_cat], axis=0)
            it["qb_cat"] = jnp.where(tri_incl, jnp.where(first, r_1, r_2), 0.0)
            v = it["v"]
            it["v_swap"] = jnp.concatenate([jnp.where(first, 0.0, v), jnp.where(first, v, 0.0)], axis=0)
        intras = [_bdot(it["kq"], it["v_swap"]) for it in items]
        yield
        t_invs = yield from _pair_unit_lower_inverse([it["a_cat"] for it in items], first, eye, refine=False)
        sols = [_bdot(t_inv, jnp.concatenate([_pair_blockdiag(it["kg"], first),
                                              _pair_blockdiag(intra[0:c], first)], axis=-1))
                for it, t_inv, intra in zip(items, t_invs, intras)]
        yield
        corrs = [_bdot(it["qb_cat"], jnp.concatenate([_pair_blockdiag(sol[:, 0:lanes], first),
                                                      _pair_blockdiag(sol[:, lanes:2 * lanes], first)], axis=-1))
                 for it, sol in zip(items, sols)]
        yield
        us = [_bdot_tn(jnp.concatenate([it["k_d"], it["b_d"]], axis=0),
                       jnp.concatenate([it["v"], -sol[:, lanes:2 * lanes]], axis=0)) for it, sol in zip(items, sols)]
        yield
        ms = [_bdot_tn(it["b_d"], sol[:, 0:lanes]) for it, sol in zip(items, sols)]
        yield
        for item, intra, corr, u, m in zip(items, intras, corrs, us, ms):
            ci = step * group + item["j"]
            p = item["p"]
            rq_scr[ci, p] = (item["rg"] - corr[:, 0:lanes]).astype(BF16)
            row = r0 + item["j"] * c
            ov_scr[row:row + c, sls[p]] = intra[c:2 * c] - corr[:, lanes:2 * lanes]
            u_scr[ci, p] = jnp.where(same_head, u, 0.0)
            m_scr[ci, p] = jnp.where(same_head, m, 0.0).astype(BF16)

    n_steps = bb * n_groups
    items = _run(front(0))
    for step in range(n_steps):
        if step + 1 < n_steps:
            _, items = _run_interleaved(back(step, items), front(step + 1))
        else:
            _run(back(step, items))

    def pass2(ci, carry):
        cis = [b * n_chunks + ci for b in range(bb)]
        rows = [pl.ds(pl.multiple_of(b * tt + ci * c, c), c) for b in range(bb)]
        s_old = [[s_scr[b * RW_PAIRS + p] for p in pairs] for b in range(bb)]
        s_bf = [[s.astype(BF16) for s in sb] for sb in s_old]
        o = [[jnp.dot(rq_scr[cis[b], p], s_bf[b][p], preferred_element_type=F32) for p in pairs]
             for b in range(bb)]
        sm = [[jnp.dot(m_scr[cis[b], p], s_bf[b][p], preferred_element_type=F32) for p in pairs]
              for b in range(bb)]
        for b in range(bb):
            ov_scr[rows[b], :] = jnp.concatenate(o[b], axis=-1) + ov_scr[rows[b], :]
        s_scr[...] = jnp.stack([s_old[b][p] * dec_scr[cis[b], p * lanes:(p + 1) * lanes, 0:1] - sm[b][p]
                                + u_scr[cis[b], p] for b in range(bb) for p in pairs])
        return carry

    lax.fori_loop(0, n_chunks, pass2, 0)

    def pass3(step, carry):
        b = step // n_groups
        gi = step - b * n_groups
        r0 = pl.multiple_of(step * gc, gc)
        o = ov_scr[pl.ds(r0, gc), :]
        o_all = jnp.concatenate([o[:, sl] for sl in sls], axis=0)
        mean = _bdot(o_all, seg_ones) * (1.0 / RW_HD)
        dev = o_all - mean
        var = _bdot(dev * dev, seg_ones) * (1.0 / RW_HD)
        o_n = dev * lax.rsqrt(var + GN_EPS)
        o_n = jnp.concatenate([o_n[p * gc:(p + 1) * gc] for p in pairs], axis=-1) * ln_w + ln_b
        or_ref[b, pl.ds(pl.multiple_of(gi * gc, gc), gc), :] = (
            (o_n + bonus_scr[pl.ds(r0, gc), :]) * gate_scr[pl.ds(r0, gc), :])
        return carry

    lax.fori_loop(0, bb * n_groups, pass3, 0)

    @pl.when(t_idx == pl.num_programs(1) - 1)
    def _():
        for b in range(bb):
            sout_ref[b] = s_scr[b * RW_PAIRS:(b + 1) * RW_PAIRS]


def _rwkv_call(pr, s0, pvec, w2, a2, g2, layer):
    bsz, seq, _ = pr.shape
    tt = min(seq, RW_TIME_TILE)
    n_chunks = tt // CHUNK
    group = RW_GROUP if n_chunks % RW_GROUP == 0 else 1
    bb = _seq_block(bsz, tt, RW_SEQ_BLOCK * RW_TIME_TILE)
    assert seq % tt == 0 and tt % CHUNK == 0
    lanes = 2 * RW_HD
    smap = lambda b, t: (b, 0, 0, 0)
    return pl.pallas_call(
        functools.partial(_rwkv_kernel, tt=tt, group=group, bb=bb),
        grid=(bsz // bb, seq // tt),
        in_specs=[pl.BlockSpec((bb, tt, RW_PROJ), lambda b, t: (b, t, 0)),
                  pl.BlockSpec((bb, RW_PAIRS, lanes, lanes), smap),
                  _layer_spec(pvec, layer),
                  _layer_spec(w2, layer), _layer_spec(a2, layer), _layer_spec(g2, layer)],
        out_specs=[pl.BlockSpec((bb, tt, RW_WIDTH), lambda b, t: (b, t, 0)),
                   pl.BlockSpec((bb, RW_PAIRS, lanes, lanes), smap)],
        out_shape=[jax.ShapeDtypeStruct((bsz, seq, RW_WIDTH), F32),
                   jax.ShapeDtypeStruct((bsz, RW_PAIRS, lanes, lanes), F32)],
        scratch_shapes=[pltpu.VMEM((bb * RW_PAIRS, lanes, lanes), F32),
                        pltpu.VMEM((bb * n_chunks, RW_PAIRS, CHUNK, lanes), BF16),
                        pltpu.VMEM((bb * tt, RW_WIDTH), F32),
                        pltpu.VMEM((bb * n_chunks, RW_PAIRS, lanes, lanes), BF16),
                        pltpu.VMEM((bb * n_chunks, RW_PAIRS, lanes, lanes), F32),
                        pltpu.VMEM((bb * n_chunks, RW_WIDTH, 8), F32),
                        pltpu.VMEM((bb * tt, RW_WIDTH), F32),
                        pltpu.VMEM((bb * tt, RW_WIDTH), F32)],
        compiler_params=pltpu.CompilerParams(dimension_semantics=("parallel", "arbitrary"),
                                             vmem_limit_bytes=VMEM_LIMIT),
        name="rwkv7",
    )(pr, s0, pvec, w2, a2, g2)


def _outffn_kernel(og_ref, or_ref, x_ref, g1_ref, sc_ref, sh_ref, g2_ref, wo_ref, lnp_ref,
                   w1_ref, w2_ref, o_ref, *, alpha):
    bb, tt, d = x_ref.shape
    rows = bb * tt
    y = (jnp.dot(og_ref[...].reshape(rows, GDN_WIDTH).astype(BF16), wo_ref[0, 0:GDN_WIDTH, :],
                 preferred_element_type=F32)
         + jnp.dot(or_ref[...].reshape(rows, RW_WIDTH).astype(BF16), wo_ref[0, GDN_WIDTH:, :],
                   preferred_element_type=F32)).reshape(bb, tt, d)
    x1 = _layer_norm(alpha * x_ref[...] + g1_ref[...] * y, lnp_ref[0, 0:1, :], lnp_ref[0, 1:2, :], LN_EPS)
    hb = (x1 * (1.0 + sc_ref[...]) + sh_ref[...]).reshape(rows, d).astype(BF16)
    f = jnp.zeros((rows, d), F32)
    for j in range(D_FF // FF_TILE):
        hid = jnp.dot(hb, w1_ref[0, :, j * FF_TILE:(j + 1) * FF_TILE], preferred_element_type=F32)
        hid = jnp.square(jnp.maximum(hid, 0.0)).astype(BF16)
        f = f + jnp.dot(hid, w2_ref[0, j * FF_TILE:(j + 1) * FF_TILE, :], preferred_element_type=F32)
    o_ref[...] = _layer_norm(alpha * x1 + g2_ref[...] * f.reshape(bb, tt, d),
                             lnp_ref[0, 2:3, :], lnp_ref[0, 3:4, :], LN_EPS)


def _outffn_call(og, orr, x, g1, sc2, sh2, g2, wo, lnp, w1, w2, alpha, layer):
    bsz, seq, d = x.shape
    bb, tt = _row_blocks(bsz, seq)
    xmap = lambda b, t: (b, t, 0)
    mmap = lambda b, t: (b, 0, 0)
    mod_spec = pl.BlockSpec((bb, 1, d), mmap)
    return pl.pallas_call(
        functools.partial(_outffn_kernel, alpha=alpha),
        grid=(bsz // bb, seq // tt),
        in_specs=[pl.BlockSpec((bb, tt, GDN_WIDTH), xmap),
                  pl.BlockSpec((bb, tt, RW_WIDTH), xmap),
                  pl.BlockSpec((bb, tt, d), xmap),
                  mod_spec, mod_spec, mod_spec, mod_spec,
                  _layer_spec(wo, layer, single_buffer=True), _layer_spec(lnp, layer),
                  _layer_spec(w1, layer, single_buffer=True), _layer_spec(w2, layer, single_buffer=True)],
        out_specs=pl.BlockSpec((bb, tt, d), xmap),
        out_shape=jax.ShapeDtypeStruct((bsz, seq, d), F32),
        compiler_params=pltpu.CompilerParams(dimension_semantics=("parallel", "parallel"),
                                             vmem_limit_bytes=VMEM_LIMIT),
        name="out_ffn",
    )(og, orr, x, g1, sc2, sh2, g2, wo, lnp, w1, w2)


def _prep_params(w_in, gdn_conv_w, gdn_a_log, gdn_dt_bias, gdn_norm_w, rwkv_mu, rwkv_w0, rwkv_w2,
                 rwkv_a0, rwkv_a2, rwkv_g2, rwkv_kk, rwkv_ka, rwkv_rk, rwkv_ln_w, rwkv_ln_b, w_out,
                 ln1_w, ln1_b, w_ff1, w_ff2, ln2_w, ln2_b):
    depth, d, _ = w_in.shape
    n_gdn = GDN_QKV + GDN_WIDTH + 2 * GDN_HEADS
    w_in = w_in.astype(BF16)
    wg = jnp.concatenate([w_in[:, :, :n_gdn], jnp.zeros((depth, d, GDN_COLS - n_gdn), BF16)], axis=2)
    wr = w_in[:, :, n_gdn:]
    gate_params = jnp.zeros((depth, 8, GATE_COLS), F32)
    gate_params = gate_params.at[:, 0, :GDN_HEADS].set(gdn_a_log).at[:, 1, :GDN_HEADS].set(gdn_dt_bias)
    pvec = jnp.stack([rwkv_w0, rwkv_a0, rwkv_kk, rwkv_ka, rwkv_rk, rwkv_ln_w, rwkv_ln_b,
                      jnp.zeros_like(rwkv_w0)], axis=1)
    zeros = jnp.zeros((depth, RW_LORA // 2, RW_WIDTH), F32)
    w2 = jnp.concatenate([rwkv_w2, zeros], axis=1).astype(BF16)
    a2 = jnp.concatenate([zeros, rwkv_a2], axis=1).astype(BF16)
    zero_row = jnp.zeros_like(ln1_w)
    lnp = jnp.stack([ln1_w, ln1_b, ln2_w, ln2_b] + [zero_row] * 4, axis=1)
    return dict(wg=wg, wr=wr, conv_w=gdn_conv_w, gate_params=gate_params,
                norm_w=gdn_norm_w[:, None, :], mu=rwkv_mu[:, None, :], pvec=pvec,
                w2=w2, a2=a2, g2=rwkv_g2.astype(BF16), wo=w_out.astype(BF16), lnp=lnp,
                w1=w_ff1.astype(BF16), w2f=w_ff2.astype(BF16))


def _rwkv_state_to_pairs(s):
    bsz = s.shape[0]
    s = jnp.swapaxes(s, -1, -2).reshape(bsz, RW_PAIRS, 2, RW_HD, RW_HD)
    z = jnp.zeros_like(s[:, :, 0])
    top = jnp.concatenate([s[:, :, 0], z], axis=-1)
    bot = jnp.concatenate([z, s[:, :, 1]], axis=-1)
    return jnp.concatenate([top, bot], axis=-2)


def _rwkv_state_from_pairs(sp):
    bsz = sp.shape[0]
    return jnp.swapaxes(jnp.stack([sp[:, :, :RW_HD, :RW_HD], sp[:, :, RW_HD:, RW_HD:]], axis=2).reshape(
        bsz, RW_HEADS, RW_HD, RW_HD), -1, -2)


def _trunk(x, mods, st_gdn, st_conv, st_rw, st_shift, p, alpha):
    bsz, seq, d = x.shape
    assert seq >= HALO
    outs = ([], [], [], [])
    for l in range(mods.shape[0]):
        sh1, sc1, g1, sh2, sc2, g2 = [mods[l, :, None, i * d:(i + 1) * d] for i in range(6)]
        conv_halo = jnp.pad(st_conv[l], ((0, 0), (HALO - (CONV_W - 1), 0), (0, 0)))
        shift_halo = jnp.pad(st_shift[l][:, None, :], ((0, 0), (HALO - 1, 0), (0, 0)))
        pg, pr, conv_raw, shift_raw = _inproj_call(x, sc1, sh1, p["wg"], p["wr"], p["conv_w"], p["mu"],
                                                   conv_halo, shift_halo, l)
        og, n_gdn = _gdn_call(pg, st_gdn[l], p["gate_params"], p["norm_w"], l)
        orr, n_rw = _rwkv_call(pr, _rwkv_state_to_pairs(st_rw[l]), p["pvec"], p["w2"], p["a2"], p["g2"], l)
        outs[0].append(n_gdn)
        outs[1].append(conv_raw[:, HALO - (CONV_W - 1):, :])
        outs[2].append(_rwkv_state_from_pairs(n_rw))
        outs[3].append(shift_raw[:, HALO - 1, :])
        x = _outffn_call(og, orr, x, g1, sc2, sh2, g2, p["wo"], p["lnp"], p["w1"], p["w2f"], alpha, l)
    return (x,) + tuple(jnp.stack(o) for o in outs)


def kernel(x_prompt, x_sample, c_prompt, c_sample, state_gdn, state_gdn_conv, state_rwkv, state_rwkv_shift, w_ada, b_ada, w_in, gdn_conv_w, gdn_a_log, gdn_dt_bias, gdn_norm_w, rwkv_mu, rwkv_w0, rwkv_w2, rwkv_a0, rwkv_a2, rwkv_g2, rwkv_kk, rwkv_ka, rwkv_rk, rwkv_ln_w, rwkv_ln_b, w_out, ln1_w, ln1_b, w_ff1, w_ff2, ln2_w, ln2_b):
    depth = w_in.shape[0]
    alpha = (2 * depth) ** DEPTH_ALPHA_POW
    bp = x_prompt.shape[0]
    params = _prep_params(w_in, gdn_conv_w, gdn_a_log, gdn_dt_bias, gdn_norm_w, rwkv_mu, rwkv_w0, rwkv_w2,
                          rwkv_a0, rwkv_a2, rwkv_g2, rwkv_kk, rwkv_ka, rwkv_rk, rwkv_ln_w, rwkv_ln_b,
                          w_out, ln1_w, ln1_b, w_ff1, w_ff2, ln2_w, ln2_b)
    mods = _mod_call(jnp.concatenate([c_prompt, c_sample], axis=0), w_ada, b_ada)
    zeros = lambda *shape: jnp.zeros((depth, bp) + shape, F32)
    y_p, p_gdn, p_conv, p_rw, p_shift = _trunk(
        x_prompt, mods[:, :bp],
        zeros(GDN_HEADS, GDN_D, GDN_D), zeros(CONV_W - 1, GDN_QKV),
        zeros(RW_HEADS, RW_HD, RW_HD), zeros(RW_PROJ), params, alpha)
    y_s, s_gdn, s_conv, s_rw, s_shift = _trunk(
        x_sample, mods[:, bp:], state_gdn, state_gdn_conv, state_rwkv, state_rwkv_shift, params, alpha)
    return (y_p, y_s, p_gdn, p_conv, p_rw, p_shift, s_gdn, s_conv, s_rw, s_shift)
```
